```python
import math
import jax, jax.numpy as jnp
from jax import lax
import numpy as np

D_MODEL = 2048
BATCH = 16
SEQ = 2048
DEPTH = 2

N_EVEN = (DEPTH + 1) // 2
N_ODD = DEPTH // 2
EPS = 1e-6
ROPE_BASE = 10000.0
S5_WIDTH = 512
S5_GROUP = 16
S5_GROUPS = S5_WIDTH // S5_GROUP
S5_STATE = 64
RET_HEADS = 6
RET_HEAD_DIM = 256
RET_WIDTH = RET_HEADS * RET_HEAD_DIM
RET_CHUNK = 128
EVEN_IN = S5_WIDTH + 4 * RET_WIDTH
MLA_HEADS = 16
MLA_NOPE = 128
MLA_ROPE = 64
MLA_V = 128
MLA_Q_RANK = 512
MLA_KV_RANK = 512
MLA_DOWN = MLA_Q_RANK + MLA_KV_RANK + MLA_ROPE
ATTN_BLOCK = 128
D_FF = 5632
CONV_WIDTH = 3

kernel_name = "hybrid_s5_retention_mla_convffn"


def rms_norm(x, g):
    xf = x.astype(jnp.float32)
    y = xf * lax.rsqrt(jnp.mean(xf * xf, axis=-1, keepdims=True) + EPS)
    return (y * g.astype(jnp.float32)).astype(x.dtype)


def rope_tables(n, dim):
    inv = ROPE_BASE ** (-jnp.arange(0, dim, 2, dtype=jnp.float32) / dim)
    ang = jnp.arange(n, dtype=jnp.float32)[:, None] * inv[None, :]
    return jnp.cos(ang), jnp.sin(ang)


def apply_rope(x, cos, sin):
    x1, x2 = jnp.split(x, 2, axis=-1)
    return jnp.concatenate([x1 * cos - x2 * sin, x2 * cos + x1 * sin], axis=-1).astype(x.dtype)


def _ssm_combine(e1, e2):
    a1r, a1i, b1r, b1i = e1
    a2r, a2i, b2r, b2i = e2
    return (a2r * a1r - a2i * a1i,
            a2r * a1i + a2i * a1r,
            a2r * b1r - a2i * b1i + b2r,
            a2r * b1i + a2i * b1r + b2i)


def s5_mixer(u, lam_re, lam_im, log_dt, b_re, b_im, c_re, c_im, d_skip, glu_w, glu_b):
    f32 = jnp.float32
    bsz, s, _ = u.shape
    lr, li = lam_re.astype(f32), lam_im.astype(f32)
    dt = jnp.exp(log_dt.astype(f32))[:, None]
    mag = jnp.exp(lr * dt)
    ab_re, ab_im = mag * jnp.cos(li * dt), mag * jnp.sin(li * dt)
    den = lr * lr + li * li
    nr, ni = ab_re - 1.0, ab_im
    f_re = (nr * lr + ni * li) / den
    f_im = (ni * lr - nr * li) / den
    br, bi = b_re.astype(f32), b_im.astype(f32)
    bb_re = f_re[..., None] * br - f_im[..., None] * bi
    bb_im = f_re[..., None] * bi + f_im[..., None] * br
    uf = u.astype(f32)
    ug = uf.reshape(bsz, s, S5_GROUPS, S5_GROUP)
    bu_re = jnp.einsum('bsgh,gph->sbgp', ug, bb_re)
    bu_im = jnp.einsum('bsgh,gph->sbgp', ug, bb_im)
    a_re = jnp.broadcast_to(ab_re[None, None], bu_re.shape)
    a_im = jnp.broadcast_to(ab_im[None, None], bu_im.shape)
    _, _, x_re, x_im = lax.associative_scan(_ssm_combine, (a_re, a_im, bu_re, bu_im), axis=0)
    y = (jnp.einsum('sbgp,ghp->bsgh', x_re, c_re.astype(f32))
         - jnp.einsum('sbgp,ghp->bsgh', x_im, c_im.astype(f32)))
    y = y.reshape(bsz, s, S5_WIDTH) + d_skip.astype(f32) * uf
    y = jax.nn.gelu(y)
    y = y * jax.nn.sigmoid(y @ glu_w.astype(f32) + glu_b.astype(f32))
    return y.astype(u.dtype)


def retention(q, k, v, gamma_log):
    bsz, s, h, dk = q.shape
    dv = v.shape[-1]
    c = RET_CHUNK
    n = s // c
    idx = jnp.arange(c, dtype=jnp.float32)
    diff = idx[:, None] - idx[None, :]
    lg = gamma_log[:, None, None]
    dmask = jnp.where(diff[None] >= 0, jnp.exp(jnp.maximum(diff[None], 0.0) * lg), 0.0)
    xi = jnp.exp((idx[None, :] + 1.0) * gamma_log[:, None])
    zeta = jnp.exp((c - 1.0 - idx[None, :]) * gamma_log[:, None])
    g_chunk = jnp.exp(c * gamma_log)

    def to_chunks(t):
        return t.reshape(bsz, n, c, h, t.shape[-1]).transpose(1, 0, 3, 2, 4)

    def step(r, qkv):
        qc, kc, vc = qkv
        inner = jnp.einsum('bhid,bhjd->bhij', qc, kc) * dmask
        o = (jnp.einsum('bhij,bhjd->bhid', inner, vc)
             + jnp.einsum('bhid,bhde->bhie', qc * xi[None, :, :, None], r))
        r = r * g_chunk[None, :, None, None] + jnp.einsum('bhjd,bhje->bhde', kc * zeta[None, :, :, None], vc)
        return r, o

    r0 = jnp.zeros((bsz, h, dk, dv), jnp.float32)
    _, o = lax.scan(step, r0, (to_chunks(q), to_chunks(k), to_chunks(v)))
    return o.transpose(1, 0, 3, 2, 4).reshape(bsz, s, h, dv)


def even_mixer(hn, w_in, lam_re, lam_im, log_dt, b_re, b_im, c_re, c_im, d_skip, glu_w, glu_b,
               ret_norm, w_out, cos, sin, gamma_log):
    bsz, s, _ = hn.shape
    f32 = jnp.float32
    proj = hn @ w_in
    u, q, k, v, g = jnp.split(proj, [S5_WIDTH, S5_WIDTH + RET_WIDTH, S5_WIDTH + 2 * RET_WIDTH,
                                     S5_WIDTH + 3 * RET_WIDTH], axis=-1)
    y_s5 = s5_mixer(u, lam_re, lam_im, log_dt, b_re, b_im, c_re, c_im, d_skip, glu_w, glu_b)
    q = apply_rope(q.astype(f32).reshape(bsz, s, RET_HEADS, RET_HEAD_DIM), cos[:, None], sin[:, None])
    k = apply_rope(k.astype(f32).reshape(bsz, s, RET_HEADS, RET_HEAD_DIM), cos[:, None], sin[:, None])
    k = k * (RET_HEAD_DIM ** -0.5)
    v = v.astype(f32).reshape(bsz, s, RET_HEADS, RET_HEAD_DIM)
    o = retention(q, k, v, gamma_log)
    o = o * lax.rsqrt(jnp.mean(o * o, axis=-1, keepdims=True) + EPS)
    o = o * ret_norm.astype(f32).reshape(RET_HEADS, RET_HEAD_DIM)
    y_ret = (jax.nn.silu(g.astype(f32)) * o.reshape(bsz, s, RET_WIDTH)).astype(hn.dtype)
    return jnp.concatenate([y_s5, y_ret], axis=-1) @ w_out


def mla_mixer(hn, w_down, q_norm, w_uq, kv_norm, w_ukv, w_out, cos, sin):
    bsz, s, _ = hn.shape
    c = hn @ w_down
    cq, ckv, kr = jnp.split(c, [MLA_Q_RANK, MLA_Q_RANK + MLA_KV_RANK], axis=-1)
    q = (rms_norm(cq, q_norm) @ w_uq).reshape(bsz, s, MLA_HEADS, MLA_NOPE + MLA_ROPE)
    qn = q[..., :MLA_NOPE]
    qr = apply_rope(q[..., MLA_NOPE:], cos[:, None], sin[:, None])
    kr = apply_rope(kr, cos, sin)
    kv = (rms_norm(ckv, kv_norm) @ w_ukv).reshape(bsz, s, MLA_HEADS, MLA_NOPE + MLA_V)
    kn, v = kv[..., :MLA_NOPE], kv[..., MLA_NOPE:]
    scale = (MLA_NOPE + MLA_ROPE) ** -0.5
    outs = []
    for blk in range(s // ATTN_BLOCK):
        s0, s1 = blk * ATTN_BLOCK, (blk + 1) * ATTN_BLOCK
        sc = (jnp.einsum('bqhd,bkhd->bhqk', qn[:, s0:s1], kn[:, :s1])
              + jnp.einsum('bqhr,bkr->bhqk', qr[:, s0:s1], kr[:, :s1])).astype(jnp.float32) * scale
        mask = jnp.arange(s0, s1)[:, None] >= jnp.arange(s1)[None, :]
        sc = jnp.where(mask, sc, -1e30)
        p = jax.nn.softmax(sc, axis=-1).astype(v.dtype)
        outs.append(jnp.einsum('bhqk,bkhd->bqhd', p, v[:, :s1]))
    o = jnp.concatenate(outs, axis=1).reshape(bsz, s, MLA_HEADS * MLA_V)
    return o @ w_out


def conv_ffn(hn, w_up, conv_w, conv_b, w_down):
    a = hn @ w_up
    ap = jnp.pad(a, ((0, 0), (CONV_WIDTH - 1, 0), (0, 0)))
    a = ap[:, :-2] * conv_w[0] + ap[:, 1:-1] * conv_w[1] + ap[:, 2:] * conv_w[2] + conv_b
    gate, val = jnp.split(a, 2, axis=-1)
    return (jax.nn.silu(gate) * val) @ w_down


def setup_inputs(seed: int = 0) -> dict:
    key = jax.random.key(seed)
    ks = jax.random.split(key, 32)
    f32 = jnp.float32

    def nrm(k, shape, std):
        return std * jax.random.normal(k, shape, f32)

    E, O, L = N_EVEN, N_ODD, DEPTH
    G, P, H = S5_GROUPS, S5_STATE, S5_GROUP
    lam_im = jnp.broadcast_to(jnp.pi * jnp.arange(P, dtype=f32), (E, G, P))
    return {
        "x": nrm(ks[0], (BATCH, SEQ, D_MODEL), 1.0),
        "ln_mix_even": 1.0 + nrm(ks[1], (E, D_MODEL), 0.02),
        "w_in_even": nrm(ks[2], (E, D_MODEL, EVEN_IN), D_MODEL ** -0.5),
        "s5_lambda_re": -0.5 + nrm(ks[3], (E, G, P), 0.01),
        "s5_lambda_im": lam_im + 0.0,
        "s5_log_dt": jax.random.uniform(ks[4], (E, G), f32, math.log(1e-3), math.log(1e-1)),
        "s5_b_re": nrm(ks[5], (E, G, P, H), (1.0 / math.sqrt(H)) / math.sqrt(2.0)),
        "s5_b_im": nrm(ks[6], (E, G, P, H), (1.0 / math.sqrt(H)) / math.sqrt(2.0)),
        "s5_c_re": nrm(ks[7], (E, G, H, P), (1.0 / math.sqrt(P)) / math.sqrt(2.0)),
        "s5_c_im": nrm(ks[8], (E, G, H, P), (1.0 / math.sqrt(P)) / math.sqrt(2.0)),
        "s5_d": nrm(ks[9], (E, S5_WIDTH), 1.0),
        "s5_glu_w": nrm(ks[10], (E, S5_WIDTH, S5_WIDTH), S5_WIDTH ** -0.5),
        "s5_glu_b": nrm(ks[11], (E, S5_WIDTH), 0.01),
        "ret_norm": 1.0 + nrm(ks[12], (E, RET_WIDTH), 0.02),
        "w_out_even": nrm(ks[13], (E, D_MODEL, D_MODEL), D_MODEL ** -0.5),
        "ln_mix_odd": 1.0 + nrm(ks[14], (O, D_MODEL), 0.02),
        "mla_w_down": nrm(ks[15], (O, D_MODEL, MLA_DOWN), D_MODEL ** -0.5),
        "mla_q_norm": 1.0 + nrm(ks[16], (O, MLA_Q_RANK), 0.02),
        "mla_w_uq": nrm(ks[17], (O, MLA_Q_RANK, MLA_HEADS * (MLA_NOPE + MLA_ROPE)), MLA_Q_RANK ** -0.5),
        "mla_kv_norm": 1.0 + nrm(ks[18], (O, MLA_KV_RANK), 0.02),
        "mla_w_ukv": nrm(ks[19], (O, MLA_KV_RANK, MLA_HEADS * (MLA_NOPE + MLA_V)), MLA_KV_RANK ** -0.5),
        "w_out_odd": nrm(ks[20], (O, MLA_HEADS * MLA_V, D_MODEL), (MLA_HEADS * MLA_V) ** -0.5),
        "ln_ffn": 1.0 + nrm(ks[21], (L, D_MODEL), 0.02),
        "ffn_w_up": nrm(ks[22], (L, D_MODEL, 2 * D_FF), D_MODEL ** -0.5),
        "ffn_conv_w": nrm(ks[23], (L, CONV_WIDTH, 2 * D_FF), CONV_WIDTH ** -0.5),
        "ffn_conv_b": nrm(ks[24], (L, 2 * D_FF), 0.01),
        "ffn_w_down": nrm(ks[25], (L, D_FF, D_MODEL), D_FF ** -0.5),
        "ln_final": 1.0 + nrm(ks[26], (D_MODEL,), 0.02),
    }


def reference(x, ln_mix_even, w_in_even, s5_lambda_re, s5_lambda_im, s5_log_dt, s5_b_re, s5_b_im,
              s5_c_re, s5_c_im, s5_d, s5_glu_w, s5_glu_b, ret_norm, w_out_even, ln_mix_odd,
              mla_w_down, mla_q_norm, mla_w_uq, mla_kv_norm, mla_w_ukv, w_out_odd, ln_ffn,
              ffn_w_up, ffn_conv_w, ffn_conv_b, ffn_w_down, ln_final):
    s = x.shape[1]
    ret_cos, ret_sin = rope_tables(s, RET_HEAD_DIM)
    mla_cos, mla_sin = rope_tables(s, MLA_ROPE)
    gamma_log = jnp.log(1.0 - 2.0 ** (-5.0 - jnp.arange(RET_HEADS, dtype=jnp.float32)))
    h = x
    for layer in range(DEPTH):
        i = layer // 2
        if layer % 2 == 0:
            h = h + even_mixer(rms_norm(h, ln_mix_even[i]), w_in_even[i], s5_lambda_re[i], s5_lambda_im[i],
                               s5_log_dt[i], s5_b_re[i], s5_b_im[i], s5_c_re[i], s5_c_im[i], s5_d[i],
                               s5_glu_w[i], s5_glu_b[i], ret_norm[i], w_out_even[i],
                               ret_cos, ret_sin, gamma_log)
        else:
            h = h + mla_mixer(rms_norm(h, ln_mix_odd[i]), mla_w_down[i], mla_q_norm[i], mla_w_uq[i],
                              mla_kv_norm[i], mla_w_ukv[i], w_out_odd[i], mla_cos, mla_sin)
        h = h + conv_ffn(rms_norm(h, ln_ffn[layer]), ffn_w_up[layer], ffn_conv_w[layer],
                         ffn_conv_b[layer], ffn_w_down[layer])
    return rms_norm(h, ln_final)
```

```python
import functools
import math

import numpy as np
import jax
import jax.numpy as jnp
from jax import lax
from jax.experimental import pallas as pl
from jax.experimental.pallas import tpu as pltpu

F32 = jnp.float32
BF16 = jnp.bfloat16

EPS = 1e-6
ROPE_BASE = 10000.0
D_MODEL = 2048
S5_WIDTH = 512
S5_GROUP = 16
S5_GROUPS = 32
S5_STATE = 64
S5_GB = 8
S5_NBLK = S5_GROUPS // S5_GB
S5_NSTATE = S5_GROUPS * S5_STATE
RET_HEADS = 6
RET_HEAD_DIM = 256
RET_WIDTH = RET_HEADS * RET_HEAD_DIM
RET_CHUNK = 128
EVEN_IN = S5_WIDTH + 4 * RET_WIDTH
MLA_HEADS = 16
MLA_NOPE = 128
MLA_ROPE = 64
MLA_V = 128
MLA_Q_RANK = 512
MLA_KV_RANK = 512
D_FF = 5632

LANE = 128
VMEM_LIMIT = 60 * 1024 * 1024


def _params(sem, vmem=VMEM_LIMIT):
    return pltpu.CompilerParams(dimension_semantics=sem, vmem_limit_bytes=vmem)


def _rms(x, g):
    return x * lax.rsqrt(jnp.mean(x * x, axis=-1, keepdims=True) + EPS) * g


def _const_spec(shape):
    nd = len(shape)
    return pl.BlockSpec(shape, lambda *_: (0,) * nd, pipeline_mode=pl.Buffered(1))


IN_TN = 512
IN_NJ = EVEN_IN // IN_TN


def _inproj_kernel(h_ref, ln_ref, w_ref, cos_ref, sin_ref, u_ref, p_ref, hn_ref):
    j = pl.program_id(1)

    @pl.when(j == 0)
    def _():
        hn_ref[...] = _rms(h_ref[...], ln_ref[...]).astype(BF16)

    acc = jnp.dot(hn_ref[...], w_ref[...], preferred_element_type=F32)

    @pl.when(j == 0)
    def _():
        u_ref[...] = acc

    @pl.when(jnp.logical_and(j >= 1, j <= 6))
    def _():
        sc = jnp.where(j >= 4, RET_HEAD_DIM ** -0.5, 1.0).astype(F32)
        c = cos_ref[...] * sc
        s = sin_ref[...] * sc
        for hh in range(IN_TN // RET_HEAD_DIM):
            o = hh * RET_HEAD_DIM
            x1 = acc[:, o:o + LANE]
            x2 = acc[:, o + LANE:o + 2 * LANE]
            p_ref[:, o:o + LANE] = (x1 * c - x2 * s).astype(BF16)
            p_ref[:, o + LANE:o + 2 * LANE] = (x2 * c + x1 * s).astype(BF16)

    @pl.when(jnp.logical_and(j >= 7, j <= 9))
    def _():
        p_ref[...] = acc.astype(BF16)

    @pl.when(j >= 10)
    def _():
        p_ref[...] = (acc * jax.nn.sigmoid(acc)).astype(BF16)


def _inproj(x2d, ln, w, cos, sin, nb, seq, tm):
    nt = seq // tm
    npj = IN_NJ - 1
    return pl.pallas_call(
        _inproj_kernel,
        grid=(nb * nt, IN_NJ),
        in_specs=[
            pl.BlockSpec((tm, D_MODEL), lambda i, j: (i, 0)),
            pl.BlockSpec((1, D_MODEL), lambda i, j: (0, 0)),
            pl.BlockSpec((D_MODEL, IN_TN), lambda i, j: (0, j)),
            pl.BlockSpec((tm, LANE), lambda i, j: (i % nt, 0)),
            pl.BlockSpec((tm, LANE), lambda i, j: (i % nt, 0)),
        ],
        out_specs=[
            pl.BlockSpec((tm, IN_TN), lambda i, j: (i % nt, i // nt)),
            pl.BlockSpec((tm, IN_TN), lambda i, j: (i % nt, (i // nt) * npj + jnp.maximum(j - 1, 0))),
        ],
        out_shape=[
            jax.ShapeDtypeStruct((seq, nb * S5_WIDTH), F32),
            jax.ShapeDtypeStruct((seq, nb * npj * IN_TN), BF16),
        ],
        scratch_shapes=[pltpu.VMEM((tm, D_MODEL), BF16)],
        compiler_params=_params(("parallel", "arbitrary")),
        name="even_inproj",
    )(x2d, ln, w, cos, sin)


def _s5_disc_kernel(lr_ref, li_ref, ldt_ref, br_ref, bi_ref, are_ref, aim_ref, bbr_ref, bbi_ref):
    lr = lr_ref[...]
    li = li_ref[...]
    dt = jnp.exp(ldt_ref[...])
    mag = jnp.exp(lr * dt)
    ab_re = mag * jnp.cos(li * dt)
    ab_im = mag * jnp.sin(li * dt)
    den = lr * lr + li * li
    nr = ab_re - 1.0
    ni = ab_im
    f_re = (nr * lr + ni * li) / den
    f_im = (ni * lr - nr * li) / den
    are_ref[...] = ab_re
    aim_ref[...] = ab_im
    br = br_ref[...]
    bi = bi_ref[...]
    bbr_ref[...] = f_re[:, None, :] * br - f_im[:, None, :] * bi
    bbi_ref[...] = f_re[:, None, :] * bi + f_im[:, None, :] * br


def _s5_discretise(lam_re, lam_im, log_dt, b_re, b_im):
    g, p, hh = S5_GROUPS, S5_STATE, S5_GROUP
    brt = jnp.swapaxes(b_re, 1, 2)
    bit = jnp.swapaxes(b_im, 1, 2)
    return pl.pallas_call(
        _s5_disc_kernel,
        out_shape=[
            jax.ShapeDtypeStruct((g, p), F32),
            jax.ShapeDtypeStruct((g, p), F32),
            jax.ShapeDtypeStruct((g, hh, p), F32),
            jax.ShapeDtypeStruct((g, hh, p), F32),
        ],
        name="s5_discretise",
    )(lam_re, lam_im, log_dt.reshape(g, 1), brt, bit)


def _s5_pack(ab_re, ab_im, bbr, bbi, c_re, c_im):
    eye = jnp.eye(S5_GB, dtype=F32)
    nb, gb, p, hh = S5_NBLK, S5_GB, S5_STATE, S5_GROUP

    def pack_b(bb):
        return jnp.einsum('gy,aghp->aghyp', eye, bb.reshape(nb, gb, hh, p)).reshape(nb, gb * hh, gb * p)

    def pack_c(cc):
        return jnp.einsum('gy,aghp->agpyh', eye, cc.reshape(nb, gb, hh, p)).reshape(nb, gb * p, gb * hh)

    bblk = jnp.concatenate([pack_b(bbr), pack_b(bbi)], axis=-1).astype(BF16)
    cblk = jnp.concatenate([pack_c(c_re), pack_c(-c_im)], axis=1).astype(BF16)
    return ab_re.reshape(1, S5_NSTATE), ab_im.reshape(1, S5_NSTATE), bblk, cblk


S5_SLAB = 256


def _s5_kernel(u_ref, bblk_ref, cblk_ref, are_ref, aim_ref, d_ref, gw_ref, gb_ref, y_ref,
               bur_ref, bui_ref, xr_ref, xi_ref, sr_ref, si_ref, *, nb, tc):
    t = pl.program_id(0)
    half = S5_GB * S5_STATE
    wu = S5_GB * S5_GROUP

    @pl.when(t == 0)
    def _():
        sr_ref[...] = jnp.zeros_like(sr_ref)
        si_ref[...] = jnp.zeros_like(si_ref)

    u = u_ref[...]
    ub = u.astype(BF16)
    for a in range(S5_NBLK):
        bu = jnp.dot(ub[:, a * wu:(a + 1) * wu], bblk_ref[a], preferred_element_type=F32)
        bur_ref[:, a * half:(a + 1) * half] = bu[:, :half]
        bui_ref[:, a * half:(a + 1) * half] = bu[:, half:]

    for sl in range(S5_NSTATE // S5_SLAB):
        cs = slice(sl * S5_SLAB, (sl + 1) * S5_SLAB)
        ar = jnp.broadcast_to(are_ref[:, cs], (nb, S5_SLAB))
        ai = jnp.broadcast_to(aim_ref[:, cs], (nb, S5_SLAB))

        def step(k, carry, cs=cs, ar=ar, ai=ai):
            xr, xi = carry
            r0 = pl.multiple_of(k * nb, nb)
            nr = ar * xr - ai * xi + bur_ref[pl.ds(r0, nb), cs]
            ni = ar * xi + ai * xr + bui_ref[pl.ds(r0, nb), cs]
            xr_ref[pl.ds(r0, nb), cs] = nr.astype(BF16)
            xi_ref[pl.ds(r0, nb), cs] = ni.astype(BF16)
            return nr, ni

        xr, xi = lax.fori_loop(0, tc, step, (sr_ref[:, cs], si_ref[:, cs]), unroll=4)
        sr_ref[:, cs] = xr
        si_ref[:, cs] = xi

    ys = []
    for a in range(S5_NBLK):
        ya = jnp.dot(xr_ref[:, a * half:(a + 1) * half], cblk_ref[a, :half, :], preferred_element_type=F32)
        ya = ya + jnp.dot(xi_ref[:, a * half:(a + 1) * half], cblk_ref[a, half:, :], preferred_element_type=F32)
        ys.append(ya)
    y = jnp.concatenate(ys, axis=-1) + d_ref[...] * u
    y = jax.nn.gelu(y)
    z = jnp.dot(y.astype(BF16), gw_ref[...], preferred_element_type=F32) + gb_ref[...]
    y_ref[...] = (y * jax.nn.sigmoid(z)).astype(BF16)


def _s5(u_tm, bblk, cblk, a_re, a_im, d_skip, glu_w, glu_b, nb, seq, tc):
    rows = nb * tc
    return pl.pallas_call(
        functools.partial(_s5_kernel, nb=nb, tc=tc),
        grid=(seq // tc,),
        in_specs=[
            pl.BlockSpec((rows, S5_WIDTH), lambda t: (t, 0)),
            _const_spec(bblk.shape),
            _const_spec(cblk.shape),
            _const_spec((1, S5_NSTATE)),
            _const_spec((1, S5_NSTATE)),
            _const_spec((1, S5_WIDTH)),
            _const_spec((S5_WIDTH, S5_WIDTH)),
            _const_spec((1, S5_WIDTH)),
        ],
        out_specs=pl.BlockSpec((rows, S5_WIDTH), lambda t: (t, 0)),
        out_shape=jax.ShapeDtypeStruct((seq * nb, S5_WIDTH), BF16),
        scratch_shapes=[
            pltpu.VMEM((rows, S5_NSTATE), F32),
            pltpu.VMEM((rows, S5_NSTATE), F32),
            pltpu.VMEM((rows, S5_NSTATE), BF16),
            pltpu.VMEM((rows, S5_NSTATE), BF16),
            pltpu.VMEM((nb, S5_NSTATE), F32),
            pltpu.VMEM((nb, S5_NSTATE), F32),
        ],
        compiler_params=_params(("arbitrary",)),
        name="s5_scan",
    )(u_tm, bblk, cblk, a_re, a_im, d_skip, glu_w, glu_b)


def _retention_consts():
    hs = np.arange(RET_HEADS, dtype=np.float32)
    gamma_log = np.log((1.0 - np.exp2(-5.0 - hs)).astype(np.float32)).astype(np.float32)
    c = RET_CHUNK
    idx = np.arange(c, dtype=np.float32)
    diff = idx[:, None] - idx[None, :]
    dmask = np.where(diff[None] >= 0, np.exp(np.maximum(diff[None], 0.0) * gamma_log[:, None, None]), 0.0)
    xi = np.exp((idx[None, :] + 1.0) * gamma_log[:, None])
    zeta = np.exp((c - 1.0 - idx[None, :]) * gamma_log[:, None])
    g_chunk = np.exp(c * gamma_log)
    xi_b = np.broadcast_to(xi[:, :, None], (RET_HEADS, c, RET_HEAD_DIM))
    zeta_b = np.broadcast_to(zeta[:, :, None], (RET_HEADS, c, RET_HEAD_DIM))
    return (dmask.astype(np.float32), np.ascontiguousarray(xi_b, np.float32),
            np.ascontiguousarray(zeta_b, np.float32), [float(v) for v in g_chunk.astype(np.float32)])


def _retention_kernel(q_ref, k_ref, v_ref, sg_ref, dm_ref, xi_ref, zeta_ref, rn_ref, y_ref, r_ref, *, g_chunk):
    c = pl.program_id(1)

    @pl.when(c == 0)
    def _():
        r_ref[...] = jnp.zeros_like(r_ref)

    for h in range(RET_HEADS):
        hs = slice(h * RET_HEAD_DIM, (h + 1) * RET_HEAD_DIM)
        q = q_ref[:, hs]
        k = k_ref[:, hs]
        v = v_ref[:, hs]
        r = r_ref[h]
        s = lax.dot_general(q, k, (((1,), (1,)), ((), ())), preferred_element_type=F32)
        inner = (s * dm_ref[h]).astype(BF16)
        o = jnp.dot(inner, v, preferred_element_type=F32)
        o = o + xi_ref[h] * jnp.dot(q, r.astype(BF16), preferred_element_type=F32)
        kz = (k.astype(F32) * zeta_ref[h]).astype(BF16)
        r_ref[h] = r * g_chunk[h] + lax.dot_general(kz, v, (((0,), (0,)), ((), ())), preferred_element_type=F32)
        o = o * lax.rsqrt(jnp.mean(o * o, axis=-1, keepdims=True) + EPS)
        o = o * rn_ref[:, hs]
        y_ref[:, hs] = (sg_ref[:, hs].astype(F32) * o).astype(BF16)


def _retention(proj, ret_norm, nb, seq):
    dmask, xi_b, zeta_b, g_chunk = _retention_consts()
    nc = seq // RET_CHUNK
    blk = (RET_CHUNK, RET_WIDTH)
    return pl.pallas_call(
        functools.partial(_retention_kernel, g_chunk=g_chunk),
        grid=(nb, nc),
        in_specs=[
            pl.BlockSpec(blk, lambda b, c: (c, 4 * b)),
            pl.BlockSpec(blk, lambda b, c: (c, 4 * b + 1)),
            pl.BlockSpec(blk, lambda b, c: (c, 4 * b + 2)),
            pl.BlockSpec(blk, lambda b, c: (c, 4 * b + 3)),
            _const_spec(dmask.shape),
            _const_spec(xi_b.shape),
            _const_spec(zeta_b.shape),
            _const_spec((1, RET_WIDTH)),
        ],
        out_specs=pl.BlockSpec(blk, lambda b, c: (c, b)),
        out_shape=jax.ShapeDtypeStruct((seq, nb * RET_WIDTH), BF16),
        scratch_shapes=[pltpu.VMEM((RET_HEADS, RET_HEAD_DIM, RET_HEAD_DIM), F32)],
        compiler_params=_params(("parallel", "arbitrary")),
        name="retention",
    )(proj, proj, proj, proj, jnp.asarray(dmask), jnp.asarray(xi_b), jnp.asarray(zeta_b), ret_norm)


def _outproj_kernel(*refs, n_in):
    res_ref = refs[0]
    ys = refs[1:1 + n_in]
    ws = refs[1 + n_in:1 + 2 * n_in]
    o_ref = refs[1 + 2 * n_in]
    acc = res_ref[...]
    for y, w in zip(ys, ws):
        acc = acc + jnp.dot(y[...], w[...], preferred_element_type=F32)
    o_ref[...] = acc


def _outproj(res, res_spec, ys, y_specs, ws, grid, out_spec, out_shape, name):
    n_in = len(ys)
    return pl.pallas_call(
        functools.partial(_outproj_kernel, n_in=n_in),
        grid=grid,
        in_specs=[res_spec] + list(y_specs) + [_const_spec(w.shape) for w in ws],
        out_specs=out_spec,
        out_shape=out_shape,
        compiler_params=_params(("parallel",) * len(grid)),
        name=name,
    )(res, *ys, *ws)


def _ffn_kernel(h_ref, halo_ref, ln_ref, wu_ref, cw_ref, cb_ref, wd_ref, o_ref, hn_ref, *, nb, tm, tf):
    i = pl.program_id(0)
    j = pl.program_id(1)
    hl = 2 * nb

    @pl.when(j == 0)
    def _():
        hn_ref[hl:, :] = _rms(h_ref[...], ln_ref[...]).astype(BF16)
        halo = _rms(halo_ref[...], ln_ref[...])
        hn_ref[:hl, :] = jnp.where(i > 0, halo, 0.0).astype(BF16)

    a = jnp.dot(hn_ref[...], wu_ref[...], preferred_element_type=F32)
    cw = cw_ref[...]
    a = a[:tm] * cw[0:1] + a[nb:nb + tm] * cw[1:2] + a[hl:] * cw[2:3] + cb_ref[...]
    gate = a[:, :tf]
    g = (gate * jax.nn.sigmoid(gate) * a[:, tf:]).astype(BF16)
    contrib = jnp.dot(g, wd_ref[...], preferred_element_type=F32)

    @pl.when(j == 0)
    def _():
        o_ref[...] = h_ref[...] + contrib

    @pl.when(j > 0)
    def _():
        o_ref[...] += contrib


def _ffn(h_tm, ln, w_up, conv_w, conv_b, w_down, nb, tm, tf):
    rows = h_tm.shape[0]
    hl = 2 * nb
    nj = D_FF // tf
    return pl.pallas_call(
        functools.partial(_ffn_kernel, nb=nb, tm=tm, tf=tf),
        grid=(rows // tm, nj),
        in_specs=[
            pl.BlockSpec((tm, D_MODEL), lambda i, j: (i, 0), pipeline_mode=pl.Buffered(1)),
            pl.BlockSpec((hl, D_MODEL), lambda i, j: (jnp.maximum(i * (tm // hl) - 1, 0), 0)),
            pl.BlockSpec((1, D_MODEL), lambda i, j: (0, 0)),
            pl.BlockSpec((D_MODEL, 2 * tf), lambda i, j: (0, j)),
            pl.BlockSpec((3, 2 * tf), lambda i, j: (0, j)),
            pl.BlockSpec((1, 2 * tf), lambda i, j: (0, j)),
            pl.BlockSpec((tf, D_MODEL), lambda i, j: (j, 0)),
        ],
        out_specs=pl.BlockSpec((tm, D_MODEL), lambda i, j: (i, 0)),
        out_shape=jax.ShapeDtypeStruct((rows, D_MODEL), F32),
        scratch_shapes=[pltpu.VMEM((hl + tm, D_MODEL), BF16)],
        compiler_params=_params(("parallel", "arbitrary")),
        name="conv_ffn",
    )(h_tm, h_tm, ln, w_up, conv_w, conv_b, w_down)


def _interleave_tiles(a, tf):
    lead = a.shape[:-1]
    nj = D_FF // tf
    a = a.reshape(lead + (2, nj, tf))
    a = jnp.swapaxes(a, -3, -2)
    return a.reshape(lead + (2 * D_FF,))


MLA_DOWN_PAD = MLA_Q_RANK + MLA_KV_RANK + 2 * LANE
MLA_QN = MLA_HEADS * MLA_NOPE
MLA_QR = MLA_HEADS * MLA_ROPE
MLA_KV = MLA_HEADS * (MLA_NOPE + MLA_V)
MLA_CHUNK = 1024


def _mla_proj_kernel(h_ref, ln_ref, wd_ref, qn_ref, kvn_ref, wq_ref, wkv_ref, cos_ref, sin_ref,
                     oqn_ref, oqr_ref, okv_ref, okr_ref):
    hn = _rms(h_ref[...], ln_ref[...]).astype(BF16)
    c = jnp.dot(hn, wd_ref[...], preferred_element_type=F32)
    cos = cos_ref[...]
    sin = sin_ref[...]
    scale = (MLA_NOPE + MLA_ROPE) ** -0.5

    cq = _rms(c[:, :MLA_Q_RANK], qn_ref[...]).astype(BF16)
    for n0 in range(0, MLA_QN, MLA_CHUNK):
        qn = jnp.dot(cq, wq_ref[:, n0:n0 + MLA_CHUNK], preferred_element_type=F32)
        oqn_ref[:, n0:n0 + MLA_CHUNK] = (qn * scale).astype(BF16)
    cs = cos * scale
    ss = sin * scale
    for n0 in range(0, MLA_QR, MLA_CHUNK):
        qr = jnp.dot(cq, wq_ref[:, MLA_QN + n0:MLA_QN + n0 + MLA_CHUNK], preferred_element_type=F32)
        qp = jnp.dot(cq, wq_ref[:, MLA_QN + MLA_QR + n0:MLA_QN + MLA_QR + n0 + MLA_CHUNK],
                     preferred_element_type=F32)
        for m in range(MLA_CHUNK // LANE):
            ls = slice(m * LANE, (m + 1) * LANE)
            oqr_ref[:, n0 + m * LANE:n0 + (m + 1) * LANE] = (qr[:, ls] * cs + qp[:, ls] * ss).astype(BF16)

    ckv = _rms(c[:, MLA_Q_RANK:MLA_Q_RANK + MLA_KV_RANK], kvn_ref[...]).astype(BF16)
    for n0 in range(0, MLA_KV, MLA_CHUNK):
        okv_ref[:, n0:n0 + MLA_CHUNK] = jnp.dot(
            ckv, wkv_ref[:, n0:n0 + MLA_CHUNK], preferred_element_type=F32).astype(BF16)

    o = MLA_Q_RANK + MLA_KV_RANK
    kr2 = c[:, o:o + LANE] * cos + c[:, o + LANE:o + 2 * LANE] * sin
    left = lax.broadcasted_iota(jnp.int32, kr2.shape, 1) < MLA_ROPE
    okr_ref[:, :LANE] = jnp.where(left, kr2, 0.0).astype(BF16)
    okr_ref[:, LANE:] = jnp.where(left, 0.0, kr2).astype(BF16)


def _mla_proj(h_tm, ln, wd, q_norm, kv_norm, wq, wkv, cos4, sin4, nb, tm):
    rows = h_tm.shape[0]
    tper = tm // nb
    row = lambda i: (i, 0)
    return pl.pallas_call(
        _mla_proj_kernel,
        grid=(rows // tm,),
        in_specs=[
            pl.BlockSpec((tm, D_MODEL), row),
            _const_spec((1, D_MODEL)),
            _const_spec(wd.shape),
            _const_spec((1, MLA_Q_RANK)),
            _const_spec((1, MLA_KV_RANK)),
            _const_spec(wq.shape),
            _const_spec(wkv.shape),
            pl.BlockSpec((tm, LANE), row),
            pl.BlockSpec((tm, LANE), row),
        ],
        out_specs=[
            pl.BlockSpec((tm, MLA_QN), row),
            pl.BlockSpec((tm, MLA_QR), row),
            pl.BlockSpec((tm, MLA_KV), row),
            pl.BlockSpec((tm, 2 * LANE), row),
        ],
        out_shape=[
            jax.ShapeDtypeStruct((rows, MLA_QN), BF16),
            jax.ShapeDtypeStruct((rows, MLA_QR), BF16),
            jax.ShapeDtypeStruct((rows, MLA_KV), BF16),
            jax.ShapeDtypeStruct((rows, 2 * LANE), BF16),
        ],
        compiler_params=_params(("parallel",)),
        name="mla_proj",
    )(h_tm, ln, wd, q_norm, kv_norm, wq, wkv, cos4, sin4)


def _attn_kernel(qn_ref, qr_ref, kn_ref, kr_ref, v_ref, o_ref, *, tq):
    qi = pl.program_id(2)
    q = jnp.concatenate([qn_ref[...], qr_ref[...]], axis=-1)

    def scores(kb):
        r0 = pl.multiple_of(kb * tq, tq)
        k = jnp.concatenate([kn_ref[pl.ds(r0, tq), :], kr_ref[pl.ds(r0, tq), :]], axis=-1)
        s = lax.dot_general(q, k, (((1,), (1,)), ((), ())), preferred_element_type=F32)
        return s, v_ref[pl.ds(r0, tq), :]

    def update(s, v, carry):
        m, l, acc = carry
        m_new = jnp.maximum(m, jnp.max(s, axis=-1, keepdims=True))
        alpha = jnp.exp(m - m_new)
        p = jnp.exp(s - m_new)
        l = alpha * l + jnp.sum(p, axis=-1, keepdims=True)
        acc = alpha * acc + jnp.dot(p.astype(BF16), v, preferred_element_type=F32)
        return m_new, l, acc

    def body(kb, carry):
        s, v = scores(kb)
        return update(s, v, carry)

    init = (jnp.full((tq, 1), -1e30, F32), jnp.zeros((tq, 1), F32), jnp.zeros((tq, MLA_V), F32))
    carry = lax.fori_loop(0, qi, body, init)
    s, v = scores(qi)
    row = lax.broadcasted_iota(jnp.int32, s.shape, 0)
    col = lax.broadcasted_iota(jnp.int32, s.shape, 1)
    s = jnp.where(row >= col, s, -1e30)
    _, l, acc = update(s, v, carry)
    o_ref[...] = (acc / l).astype(BF16)


def _attention(qn, qr, kv, krp, nb, seq, tq):
    nh = MLA_HEADS
    return pl.pallas_call(
        functools.partial(_attn_kernel, tq=tq),
        grid=(nb, nh, seq // tq),
        in_specs=[
            pl.BlockSpec((tq, LANE), lambda b, h, i: (i, b * nh + h)),
            pl.BlockSpec((tq, LANE), lambda b, h, i: (i, b * (nh // 2) + h // 2)),
            pl.BlockSpec((seq, LANE), lambda b, h, i: (0, b * 2 * nh + 2 * h)),
            pl.BlockSpec((seq, LANE), lambda b, h, i: (0, b * 2 + h % 2)),
            pl.BlockSpec((seq, LANE), lambda b, h, i: (0, b * 2 * nh + 2 * h + 1)),
        ],
        out_specs=pl.BlockSpec((tq, LANE), lambda b, h, i: (i, b * nh + h)),
        out_shape=jax.ShapeDtypeStruct((seq, nb * nh * MLA_V), BF16),
        compiler_params=_params(("parallel", "parallel", "arbitrary")),
        name="mla_attention",
    )(qn, qr, kv, krp, kv)


def _final_norm_kernel(h_ref, g_ref, o_ref):
    o_ref[...] = _rms(h_ref[...], g_ref[...])


def _final_norm(h_view, g, nb, seq, tm):
    nt = seq // tm
    return pl.pallas_call(
        _final_norm_kernel,
        grid=(nb, nt),
        in_specs=[
            pl.BlockSpec((tm, D_MODEL), lambda b, t: (t, b)),
            pl.BlockSpec((1, D_MODEL), lambda b, t: (0, 0)),
        ],
        out_specs=pl.BlockSpec((tm, D_MODEL), lambda b, t: (b * nt + t, 0)),
        out_shape=jax.ShapeDtypeStruct((nb * seq, D_MODEL), F32),
        compiler_params=_params(("parallel", "parallel")),
        name="final_norm",
    )(h_view, g)


def _rope_tables(n, dim):
    inv = ROPE_BASE ** (-jnp.arange(0, dim, 2, dtype=F32) / dim)
    ang = jnp.arange(n, dtype=F32)[:, None] * inv[None, :]
    return jnp.cos(ang), jnp.sin(ang)


def _rot_cols(w):
    lead = w.shape[:-1]
    w = w.reshape(lead + (-1, 2, MLA_ROPE // 2))
    return jnp.stack([-w[..., 1, :], w[..., 0, :]], axis=-2).reshape(lead + (-1,))


def kernel(x, ln_mix_even, w_in_even, s5_lambda_re, s5_lambda_im, s5_log_dt, s5_b_re, s5_b_im, s5_c_re, s5_c_im, s5_d, s5_glu_w, s5_glu_b, ret_norm, w_out_even, ln_mix_odd, mla_w_down, mla_q_norm, mla_w_uq, mla_kv_norm, mla_w_ukv, w_out_odd, ln_ffn, ffn_w_up, ffn_conv_w, ffn_conv_b, ffn_w_down, ln_final):
    nb, seq, d = x.shape
    assert d == D_MODEL and nb == 16 and seq % 128 == 0
    rows = nb * seq
    tm_a = min(seq, 1024)
    tm = min(rows, 1024)
    tf = 512
    nt = seq // tm_a
    row2 = lambda a: a.reshape(1, -1)

    x2d = x.reshape(rows, d)

    ret_cos, ret_sin = _rope_tables(seq, RET_HEAD_DIM)
    u_v, proj_v = _inproj(x2d, row2(ln_mix_even[0]), w_in_even[0].astype(BF16), ret_cos, ret_sin, nb, seq, tm_a)

    ab_re, ab_im, bbr, bbi = _s5_discretise(s5_lambda_re[0], s5_lambda_im[0], s5_log_dt[0], s5_b_re[0], s5_b_im[0])
    a_re, a_im, bblk, cblk = _s5_pack(ab_re, ab_im, bbr, bbi, s5_c_re[0], s5_c_im[0])
    y_s5 = _s5(u_v.reshape(rows, S5_WIDTH), bblk, cblk, a_re, a_im, row2(s5_d[0]),
               s5_glu_w[0].astype(BF16), row2(s5_glu_b[0]), nb, seq, tc=min(seq, 32))

    y_ret = _retention(proj_v, row2(ret_norm[0]), nb, seq)

    wo = w_out_even[0].astype(BF16)
    tmo = min(seq, 512)
    nto = seq // tmo
    h = _outproj(
        x2d, pl.BlockSpec((tmo, d), lambda b, t: (b * nto + t, 0)),
        [y_s5.reshape(seq, nb * S5_WIDTH), y_ret],
        [pl.BlockSpec((tmo, S5_WIDTH), lambda b, t: (t, b)), pl.BlockSpec((tmo, RET_WIDTH), lambda b, t: (t, b))],
        [wo[:S5_WIDTH], wo[S5_WIDTH:]],
        (nb, nto), pl.BlockSpec((tmo, d), lambda b, t: (t, b)),
        jax.ShapeDtypeStruct((seq, nb * d), F32), "even_outproj")
    h = h.reshape(rows, d)

    def ffn(h, layer):
        return _ffn(h, row2(ln_ffn[layer]),
                    _interleave_tiles(ffn_w_up[layer], tf).astype(BF16),
                    _interleave_tiles(ffn_conv_w[layer], tf),
                    _interleave_tiles(ffn_conv_b[layer], tf).reshape(1, -1),
                    ffn_w_down[layer].astype(BF16), nb, tm, tf)

    h = ffn(h, 0)

    mla_cos, mla_sin = _rope_tables(seq, MLA_ROPE)
    cos4 = jnp.repeat(jnp.tile(mla_cos, (1, 4)), nb, axis=0)
    sin4 = jnp.repeat(jnp.tile(mla_sin, (1, 4)), nb, axis=0)
    wdn = mla_w_down[0]
    w_kr = wdn[:, MLA_Q_RANK + MLA_KV_RANK:]
    w_krp = _rot_cols(w_kr)
    wd = jnp.concatenate([wdn[:, :MLA_Q_RANK + MLA_KV_RANK], w_kr, w_kr, w_krp, w_krp], axis=1).astype(BF16)
    wq3 = mla_w_uq[0].reshape(MLA_Q_RANK, MLA_HEADS, MLA_NOPE + MLA_ROPE)
    wq_n = wq3[:, :, :MLA_NOPE].reshape(MLA_Q_RANK, MLA_QN)
    wq_r = wq3[:, :, MLA_NOPE:].reshape(MLA_Q_RANK, MLA_QR)
    wq = jnp.concatenate([wq_n, wq_r, _rot_cols(wq_r)], axis=1).astype(BF16)
    tmp = min(rows, 512)
    qn, qr, kv, krp = _mla_proj(h, row2(ln_mix_odd[0]), wd, row2(mla_q_norm[0]), row2(mla_kv_norm[0]),
                                wq, mla_w_ukv[0].astype(BF16), cos4, sin4, nb, tmp)
    o = _attention(qn.reshape(seq, nb * MLA_QN), qr.reshape(seq, nb * MLA_QR), kv.reshape(seq, nb * MLA_KV),
                   krp.reshape(seq, nb * 2 * LANE), nb, seq, tq=min(seq, 512))
    o = o.reshape(rows, MLA_HEADS * MLA_V)
    tmr = min(rows, 512)
    rowmap = lambda i: (i, 0)
    h = _outproj(h, pl.BlockSpec((tmr, d), rowmap), [o], [pl.BlockSpec((tmr, MLA_HEADS * MLA_V), rowmap)],
                 [w_out_odd[0].astype(BF16)], (rows // tmr,), pl.BlockSpec((tmr, d), rowmap),
                 jax.ShapeDtypeStruct((rows, d), F32), "odd_outproj")
    h = ffn(h, 1)

    out = _final_norm(h.reshape(seq, nb * d), row2(ln_final), nb, seq, tm_a)
    return out.reshape(nb, seq, d)
```

```python
import functools

import numpy as np
import jax
import jax.numpy as jnp
from jax import lax
from jax.experimental import pallas as pl
from jax.experimental.pallas import tpu as pltpu

F32 = jnp.float32
BF16 = jnp.bfloat16

EPS = 1e-6
ROPE_BASE = 10000.0
D_MODEL = 2048
S5_WIDTH = 512
S5_GROUP = 16
S5_GROUPS = 32
S5_STATE = 64
S5_GB = 8
S5_NBLK = S5_GROUPS // S5_GB
S5_NSTATE = S5_GROUPS * S5_STATE
RET_HEADS = 6
RET_HEAD_DIM = 256
RET_WIDTH = RET_HEADS * RET_HEAD_DIM
RET_CHUNK = 128
EVEN_IN = S5_WIDTH + 4 * RET_WIDTH
MLA_HEADS = 16
MLA_NOPE = 128
MLA_ROPE = 64
MLA_V = 128
MLA_Q_RANK = 512
MLA_KV_RANK = 512
D_FF = 5632

LANE = 128
BF16_ROWS = 16
VMEM_LIMIT = 60 * 1024 * 1024


def _params(sem, vmem=VMEM_LIMIT):
    return pltpu.CompilerParams(dimension_semantics=sem, vmem_limit_bytes=vmem)


def _rms(x, g):
    return x * lax.rsqrt(jnp.mean(x * x, axis=-1, keepdims=True) + EPS) * g


def _const_spec(shape):
    nd = len(shape)
    return pl.BlockSpec(shape, lambda *_: (0,) * nd, pipeline_mode=pl.Buffered(1))


IN_TN = 512
IN_NJ = EVEN_IN // IN_TN


def _inproj_kernel(h_ref, ln_ref, w_ref, cos_ref, sin_ref, u_ref, p_ref, hn_ref):
    j = pl.program_id(1)

    @pl.when(j == 0)
    def _():
        hn_ref[...] = _rms(h_ref[...], ln_ref[...]).astype(BF16)

    acc = jnp.dot(hn_ref[...], w_ref[...], preferred_element_type=F32)

    @pl.when(j == 0)
    def _():
        u_ref[...] = acc

    @pl.when(jnp.logical_and(j >= 1, j <= 6))
    def _():
        sc = jnp.where(j >= 4, RET_HEAD_DIM ** -0.5, 1.0).astype(F32)
        c = cos_ref[...] * sc
        s = sin_ref[...] * sc
        for hh in range(IN_TN // RET_HEAD_DIM):
            o = hh * RET_HEAD_DIM
            x1 = acc[:, o:o + LANE]
            x2 = acc[:, o + LANE:o + 2 * LANE]
            p_ref[:, o:o + LANE] = (x1 * c - x2 * s).astype(BF16)
            p_ref[:, o + LANE:o + 2 * LANE] = (x2 * c + x1 * s).astype(BF16)

    @pl.when(jnp.logical_and(j >= 7, j <= 9))
    def _():
        p_ref[...] = acc.astype(BF16)

    @pl.when(j >= 10)
    def _():
        p_ref[...] = (acc * jax.nn.sigmoid(acc)).astype(BF16)


def _inproj(x2d, ln, w, cos, sin, seq, tm):
    rows = x2d.shape[0]
    nt = seq // tm
    return pl.pallas_call(
        _inproj_kernel,
        grid=(rows // tm, IN_NJ),
        in_specs=[
            pl.BlockSpec((tm, D_MODEL), lambda i, j: (i, 0)),
            pl.BlockSpec((1, D_MODEL), lambda i, j: (0, 0)),
            pl.BlockSpec((D_MODEL, IN_TN), lambda i, j: (0, j)),
            pl.BlockSpec((tm, LANE), lambda i, j: (i % nt, 0)),
            pl.BlockSpec((tm, LANE), lambda i, j: (i % nt, 0)),
        ],
        out_specs=[
            pl.BlockSpec((tm, IN_TN), lambda i, j: (i, 0)),
            pl.BlockSpec((tm, IN_TN), lambda i, j: (i, jnp.maximum(j - 1, 0))),
        ],
        out_shape=[
            jax.ShapeDtypeStruct((rows, S5_WIDTH), F32),
            jax.ShapeDtypeStruct((rows, EVEN_IN - S5_WIDTH), BF16),
        ],
        scratch_shapes=[pltpu.VMEM((tm, D_MODEL), BF16)],
        compiler_params=_params(("parallel", "arbitrary")),
        name="even_inproj",
    )(x2d, ln, w, cos, sin)


def _s5_disc_kernel(lr_ref, li_ref, ldt_ref, br_ref, bi_ref, are_ref, aim_ref, bbr_ref, bbi_ref):
    lr = lr_ref[...]
    li = li_ref[...]
    dt = jnp.exp(ldt_ref[...])
    mag = jnp.exp(lr * dt)
    ab_re = mag * jnp.cos(li * dt)
    ab_im = mag * jnp.sin(li * dt)
    den = lr * lr + li * li
    nr = ab_re - 1.0
    ni = ab_im
    f_re = (nr * lr + ni * li) / den
    f_im = (ni * lr - nr * li) / den
    are_ref[...] = ab_re
    aim_ref[...] = ab_im
    br = br_ref[...]
    bi = bi_ref[...]
    bbr_ref[...] = f_re[:, None, :] * br - f_im[:, None, :] * bi
    bbi_ref[...] = f_re[:, None, :] * bi + f_im[:, None, :] * br


def _s5_discretise(lam_re, lam_im, log_dt, b_re, b_im):
    g, p, hh = S5_GROUPS, S5_STATE, S5_GROUP
    brt = jnp.swapaxes(b_re, 1, 2)
    bit = jnp.swapaxes(b_im, 1, 2)
    return pl.pallas_call(
        _s5_disc_kernel,
        out_shape=[
            jax.ShapeDtypeStruct((g, p), F32),
            jax.ShapeDtypeStruct((g, p), F32),
            jax.ShapeDtypeStruct((g, hh, p), F32),
            jax.ShapeDtypeStruct((g, hh, p), F32),
        ],
        name="s5_discretise",
    )(lam_re, lam_im, log_dt.reshape(g, 1), brt, bit)


def _s5_pack(ab_re, ab_im, bbr, bbi, c_re, c_im):
    eye = jnp.eye(S5_GB, dtype=F32)
    nb, gb, p, hh = S5_NBLK, S5_GB, S5_STATE, S5_GROUP

    def pack_b(bb):
        return jnp.einsum('gy,aghp->aghyp', eye, bb.reshape(nb, gb, hh, p)).reshape(nb, gb * hh, gb * p)

    def pack_c(cc):
        return jnp.einsum('gy,aghp->agpyh', eye, cc.reshape(nb, gb, hh, p)).reshape(nb, gb * p, gb * hh)

    bblk = jnp.concatenate([pack_b(bbr), pack_b(bbi)], axis=-1).astype(BF16)
    cblk = jnp.concatenate([pack_c(c_re), pack_c(-c_im)], axis=1).astype(BF16)
    return ab_re.reshape(1, S5_NSTATE), ab_im.reshape(1, S5_NSTATE), bblk, cblk


S5_SLAB = 256


def _s5_kernel(u_hbm, bblk_ref, cblk_ref, are_ref, aim_ref, d_ref, gw_ref, gb_ref, y_hbm,
               ubuf, ybuf, in_sem, out_sem, bur_ref, bui_ref, xr_ref, xi_ref, sr_ref, si_ref, *, nb, tc):
    t = pl.program_id(0)
    nt = pl.num_programs(0)
    slot = t % 2
    half = S5_GB * S5_STATE
    wu = S5_GB * S5_GROUP

    def in_copy(step, sl, b):
        return pltpu.make_async_copy(u_hbm.at[b, pl.ds(step * tc, tc), :], ubuf.at[sl, :, b, :], in_sem.at[sl, b])

    def out_copy(step, sl, b):
        return pltpu.make_async_copy(ybuf.at[sl, :, b, :], y_hbm.at[b, pl.ds(step * tc, tc), :], out_sem.at[sl, b])

    @pl.when(t == 0)
    def _():
        for b in range(nb):
            in_copy(0, 0, b).start()
        sr_ref[...] = jnp.zeros_like(sr_ref)
        si_ref[...] = jnp.zeros_like(si_ref)

    @pl.when(t + 1 < nt)
    def _():
        for b in range(nb):
            in_copy(t + 1, 1 - slot, b).start()

    for b in range(nb):
        in_copy(t, slot, b).wait()

    u = ubuf[slot].reshape(tc * nb, S5_WIDTH)
    ub = u.astype(BF16)
    for a in range(S5_NBLK):
        bu = jnp.dot(ub[:, a * wu:(a + 1) * wu], bblk_ref[a], preferred_element_type=F32)
        bur_ref[:, a * half:(a + 1) * half] = bu[:, :half]
        bui_ref[:, a * half:(a + 1) * half] = bu[:, half:]

    for sl in range(S5_NSTATE // S5_SLAB):
        cs = slice(sl * S5_SLAB, (sl + 1) * S5_SLAB)
        ar = jnp.broadcast_to(are_ref[:, cs], (nb, S5_SLAB))
        ai = jnp.broadcast_to(aim_ref[:, cs], (nb, S5_SLAB))

        def step(k, carry, cs=cs, ar=ar, ai=ai):
            xr, xi = carry
            r0 = pl.multiple_of(k * nb, nb)
            nr = ar * xr - ai * xi + bur_ref[pl.ds(r0, nb), cs]
            ni = ar * xi + ai * xr + bui_ref[pl.ds(r0, nb), cs]
            xr_ref[pl.ds(r0, nb), cs] = nr.astype(BF16)
            xi_ref[pl.ds(r0, nb), cs] = ni.astype(BF16)
            return nr, ni

        xr, xi = lax.fori_loop(0, tc, step, (sr_ref[:, cs], si_ref[:, cs]), unroll=4)
        sr_ref[:, cs] = xr
        si_ref[:, cs] = xi

    ys = []
    for a in range(S5_NBLK):
        ya = jnp.dot(xr_ref[:, a * half:(a + 1) * half], cblk_ref[a, :half, :], preferred_element_type=F32)
        ya = ya + jnp.dot(xi_ref[:, a * half:(a + 1) * half], cblk_ref[a, half:, :], preferred_element_type=F32)
        ys.append(ya)
    y = jnp.concatenate(ys, axis=-1) + d_ref[...] * u
    y = jax.nn.gelu(y)
    z = jnp.dot(y.astype(BF16), gw_ref[...], preferred_element_type=F32) + gb_ref[...]
    y = y * jax.nn.sigmoid(z)

    @pl.when(t >= 2)
    def _():
        for b in range(nb):
            out_copy(t - 2, slot, b).wait()

    ybuf[slot] = y.reshape(tc, nb, S5_WIDTH)
    for b in range(nb):
        out_copy(t, slot, b).start()

    @pl.when(t == nt - 1)
    def _():
        for b in range(nb):
            out_copy(t, slot, b).wait()

        @pl.when(nt >= 2)
        def _():
            for b in range(nb):
                out_copy(t - 1, 1 - slot, b).wait()


def _s5(u3, bblk, cblk, a_re, a_im, d_skip, glu_w, glu_b, tc):
    nb, seq, _ = u3.shape
    rows = nb * tc
    return pl.pallas_call(
        functools.partial(_s5_kernel, nb=nb, tc=tc),
        grid=(seq // tc,),
        in_specs=[
            pl.BlockSpec(memory_space=pl.ANY),
            _const_spec(bblk.shape),
            _const_spec(cblk.shape),
            _const_spec((1, S5_NSTATE)),
            _const_spec((1, S5_NSTATE)),
            _const_spec((1, S5_WIDTH)),
            _const_spec((S5_WIDTH, S5_WIDTH)),
            _const_spec((1, S5_WIDTH)),
        ],
        out_specs=pl.BlockSpec(memory_space=pl.ANY),
        out_shape=jax.ShapeDtypeStruct((nb, seq, S5_WIDTH), F32),
        scratch_shapes=[
            pltpu.VMEM((2, tc, nb, S5_WIDTH), F32),
            pltpu.VMEM((2, tc, nb, S5_WIDTH), F32),
            pltpu.SemaphoreType.DMA((2, nb)),
            pltpu.SemaphoreType.DMA((2, nb)),
            pltpu.VMEM((rows, S5_NSTATE), F32),
            pltpu.VMEM((rows, S5_NSTATE), F32),
            pltpu.VMEM((rows, S5_NSTATE), BF16),
            pltpu.VMEM((rows, S5_NSTATE), BF16),
            pltpu.VMEM((nb, S5_NSTATE), F32),
            pltpu.VMEM((nb, S5_NSTATE), F32),
        ],
        compiler_params=_params(("arbitrary",)),
        name="s5_scan",
    )(u3, bblk, cblk, a_re, a_im, d_skip, glu_w, glu_b)


def _retention_consts():
    hs = np.arange(RET_HEADS, dtype=np.float32)
    gamma_log = np.log((1.0 - np.exp2(-5.0 - hs)).astype(np.float32)).astype(np.float32)
    c = RET_CHUNK
    idx = np.arange(c, dtype=np.float32)
    diff = idx[:, None] - idx[None, :]
    dmask = np.where(diff[None] >= 0, np.exp(np.maximum(diff[None], 0.0) * gamma_log[:, None, None]), 0.0)
    xi = np.exp((idx[None, :] + 1.0) * gamma_log[:, None])
    zeta = np.exp((c - 1.0 - idx[None, :]) * gamma_log[:, None])
    g_chunk = np.exp(c * gamma_log)
    xi_b = np.broadcast_to(xi[:, :, None], (RET_HEADS, c, RET_HEAD_DIM))
    zeta_b = np.broadcast_to(zeta[:, :, None], (RET_HEADS, c, RET_HEAD_DIM))
    return (dmask.astype(np.float32), np.ascontiguousarray(xi_b, np.float32),
            np.ascontiguousarray(zeta_b, np.float32), [float(v) for v in g_chunk.astype(np.float32)])


def _retention_kernel(q_ref, k_ref, v_ref, sg_ref, dm_ref, xi_ref, zeta_ref, rn_ref, y_ref, r_ref, *, g_chunk):
    c = pl.program_id(1)

    @pl.when(c == 0)
    def _():
        r_ref[...] = jnp.zeros_like(r_ref)

    for h in range(RET_HEADS):
        hs = slice(h * RET_HEAD_DIM, (h + 1) * RET_HEAD_DIM)
        q = q_ref[:, hs]
        k = k_ref[:, hs]
        v = v_ref[:, hs]
        r = r_ref[h]
        s = lax.dot_general(q, k, (((1,), (1,)), ((), ())), preferred_element_type=F32)
        inner = (s * dm_ref[h]).astype(BF16)
        o = jnp.dot(inner, v, preferred_element_type=F32)
        o = o + xi_ref[h] * jnp.dot(q, r.astype(BF16), preferred_element_type=F32)
        kz = (k.astype(F32) * zeta_ref[h]).astype(BF16)
        r_ref[h] = r * g_chunk[h] + lax.dot_general(kz, v, (((0,), (0,)), ((), ())), preferred_element_type=F32)
        o = o * lax.rsqrt(jnp.mean(o * o, axis=-1, keepdims=True) + EPS)
        o = o * rn_ref[:, hs]
        y_ref[:, hs] = (sg_ref[:, hs].astype(F32) * o).astype(BF16)


def _retention(proj, ret_norm, nb, seq):
    dmask, xi_b, zeta_b, g_chunk = _retention_consts()
    nc = seq // RET_CHUNK
    blk = (RET_CHUNK, RET_WIDTH)
    return pl.pallas_call(
        functools.partial(_retention_kernel, g_chunk=g_chunk),
        grid=(nb, nc),
        in_specs=[
            pl.BlockSpec(blk, lambda b, c: (b * nc + c, 0)),
            pl.BlockSpec(blk, lambda b, c: (b * nc + c, 1)),
            pl.BlockSpec(blk, lambda b, c: (b * nc + c, 2)),
            pl.BlockSpec(blk, lambda b, c: (b * nc + c, 3)),
            _const_spec(dmask.shape),
            _const_spec(xi_b.shape),
            _const_spec(zeta_b.shape),
            _const_spec((1, RET_WIDTH)),
        ],
        out_specs=pl.BlockSpec(blk, lambda b, c: (b * nc + c, 0)),
        out_shape=jax.ShapeDtypeStruct((nb * seq, RET_WIDTH), BF16),
        scratch_shapes=[pltpu.VMEM((RET_HEADS, RET_HEAD_DIM, RET_HEAD_DIM), F32)],
        compiler_params=_params(("parallel", "arbitrary")),
        name="retention",
    )(proj, proj, proj, proj, jnp.asarray(dmask), jnp.asarray(xi_b), jnp.asarray(zeta_b), ret_norm)


def _outproj_kernel(*refs, n_in):
    res_ref = refs[0]
    ys = refs[1:1 + n_in]
    ws = refs[1 + n_in:1 + 2 * n_in]
    o_ref = refs[1 + 2 * n_in]
    acc = res_ref[...]
    for y, w in zip(ys, ws):
        acc = acc + jnp.dot(y[...].astype(BF16), w[...], preferred_element_type=F32)
    o_ref[...] = acc


def _outproj(res, ys, ws, tm, name):
    rows, d = res.shape
    n_in = len(ys)
    row = lambda i: (i, 0)
    return pl.pallas_call(
        functools.partial(_outproj_kernel, n_in=n_in),
        grid=(rows // tm,),
        in_specs=([pl.BlockSpec((tm, d), row)] + [pl.BlockSpec((tm, y.shape[1]), row) for y in ys]
                  + [_const_spec(w.shape) for w in ws]),
        out_specs=pl.BlockSpec((tm, d), row),
        out_shape=jax.ShapeDtypeStruct((rows, d), F32),
        compiler_params=_params(("parallel",)),
        name=name,
    )(res, *ys, *ws)


def _ffn_kernel(h_ref, halo_ref, ln_ref, wg_ref, wv_ref, cwg_ref, cwv_ref, cbg_ref, cbv_ref, wd_ref, lnf_ref,
                o_ref, hn_ref, *, nt, tm, final_norm):
    i = pl.program_id(0)
    j = pl.program_id(1)
    hl = BF16_ROWS

    @pl.when(j == 0)
    def _():
        hn_ref[hl:, :] = _rms(h_ref[...], ln_ref[...]).astype(BF16)
        halo = _rms(halo_ref[...], ln_ref[...])
        hn_ref[:hl, :] = jnp.where(i % nt > 0, halo, 0.0).astype(BF16)

    hn = hn_ref[...]

    def conv(w_ref, cw_ref, cb_ref):
        a = jnp.dot(hn, w_ref[...], preferred_element_type=F32)
        cw = cw_ref[...]
        return (a[hl - 2:hl - 2 + tm] * cw[0:1] + a[hl - 1:hl - 1 + tm] * cw[1:2] + a[hl:] * cw[2:3]
                + cb_ref[...])

    gate = conv(wg_ref, cwg_ref, cbg_ref)
    val = conv(wv_ref, cwv_ref, cbv_ref)
    g = (gate * jax.nn.sigmoid(gate) * val).astype(BF16)
    contrib = jnp.dot(g, wd_ref[...], preferred_element_type=F32)

    @pl.when(j == 0)
    def _():
        o_ref[...] = h_ref[...] + contrib

    @pl.when(j > 0)
    def _():
        o_ref[...] += contrib

    if final_norm:
        @pl.when(j == pl.num_programs(1) - 1)
        def _():
            o_ref[...] = _rms(o_ref[...], lnf_ref[...])


def _ffn(h, ln, w_up, conv_w, conv_b, w_down, ln_final, seq, tm, tf, final_norm):
    rows = h.shape[0]
    hl = BF16_ROWS
    nj = D_FF // tf
    nt = seq // tm
    return pl.pallas_call(
        functools.partial(_ffn_kernel, nt=nt, tm=tm, final_norm=final_norm),
        grid=(rows // tm, nj),
        in_specs=[
            pl.BlockSpec((tm, D_MODEL), lambda i, j: (i, 0), pipeline_mode=pl.Buffered(1)),
            pl.BlockSpec((hl, D_MODEL), lambda i, j: (jnp.maximum(i * (tm // hl) - 1, 0), 0)),
            pl.BlockSpec((1, D_MODEL), lambda i, j: (0, 0)),
            pl.BlockSpec((D_MODEL, tf), lambda i, j: (0, j)),
            pl.BlockSpec((D_MODEL, tf), lambda i, j: (0, nj + j)),
            pl.BlockSpec((3, tf), lambda i, j: (0, j)),
            pl.BlockSpec((3, tf), lambda i, j: (0, nj + j)),
            pl.BlockSpec((1, tf), lambda i, j: (0, j)),
            pl.BlockSpec((1, tf), lambda i, j: (0, nj + j)),
            pl.BlockSpec((tf, D_MODEL), lambda i, j: (j, 0)),
            pl.BlockSpec((1, D_MODEL), lambda i, j: (0, 0)),
        ],
        out_specs=pl.BlockSpec((tm, D_MODEL), lambda i, j: (i, 0)),
        out_shape=jax.ShapeDtypeStruct((rows, D_MODEL), F32),
        scratch_shapes=[pltpu.VMEM((hl + tm, D_MODEL), BF16)],
        compiler_params=_params(("parallel", "arbitrary")),
        name="conv_ffn",
    )(h, h, ln, w_up, w_up, conv_w, conv_w, conv_b, conv_b, w_down, ln_final)


MLA_DOWN_PAD = MLA_Q_RANK + MLA_KV_RANK + 2 * LANE
MLA_QN = MLA_HEADS * MLA_NOPE
MLA_QR = MLA_HEADS * MLA_ROPE
MLA_KV = MLA_HEADS * (MLA_NOPE + MLA_V)
MLA_CHUNK = 1024


def _mla_proj_kernel(h_ref, ln_ref, wd_ref, qn_ref, kvn_ref, wq_ref, wkv_ref, cos_ref, sin_ref,
                     oqn_ref, oqr_ref, okv_ref, okr_ref):
    hn = _rms(h_ref[...], ln_ref[...]).astype(BF16)
    c = jnp.dot(hn, wd_ref[...], preferred_element_type=F32)
    cos = cos_ref[...]
    sin = sin_ref[...]
    scale = (MLA_NOPE + MLA_ROPE) ** -0.5

    cq = _rms(c[:, :MLA_Q_RANK], qn_ref[...]).astype(BF16)
    for n0 in range(0, MLA_QN, MLA_CHUNK):
        qn = jnp.dot(cq, wq_ref[:, n0:n0 + MLA_CHUNK], preferred_element_type=F32)
        oqn_ref[:, n0:n0 + MLA_CHUNK] = (qn * scale).astype(BF16)
    cs = cos * scale
    ss = sin * scale
    for n0 in range(0, MLA_QR, MLA_CHUNK):
        qr = jnp.dot(cq, wq_ref[:, MLA_QN + n0:MLA_QN + n0 + MLA_CHUNK], preferred_element_type=F32)
        qp = jnp.dot(cq, wq_ref[:, MLA_QN + MLA_QR + n0:MLA_QN + MLA_QR + n0 + MLA_CHUNK],
                     preferred_element_type=F32)
        for m in range(MLA_CHUNK // LANE):
            ls = slice(m * LANE, (m + 1) * LANE)
            oqr_ref[:, n0 + m * LANE:n0 + (m + 1) * LANE] = (qr[:, ls] * cs + qp[:, ls] * ss).astype(BF16)

    ckv = _rms(c[:, MLA_Q_RANK:MLA_Q_RANK + MLA_KV_RANK], kvn_ref[...]).astype(BF16)
    for n0 in range(0, MLA_KV, MLA_CHUNK):
        okv_ref[:, n0:n0 + MLA_CHUNK] = jnp.dot(
            ckv, wkv_ref[:, n0:n0 + MLA_CHUNK], preferred_element_type=F32).astype(BF16)

    o = MLA_Q_RANK + MLA_KV_RANK
    kr2 = c[:, o:o + LANE] * cos + c[:, o + LANE:o + 2 * LANE] * sin
    left = lax.broadcasted_iota(jnp.int32, kr2.shape, 1) < MLA_ROPE
    okr_ref[:, :LANE] = jnp.where(left, kr2, 0.0).astype(BF16)
    okr_ref[:, LANE:] = jnp.where(left, 0.0, kr2).astype(BF16)


def _mla_proj(h, ln, wd, q_norm, kv_norm, wq, wkv, cos4, sin4, seq, tm):
    rows = h.shape[0]
    nt = seq // tm
    row = lambda i: (i, 0)
    return pl.pallas_call(
        _mla_proj_kernel,
        grid=(rows // tm,),
        in_specs=[
            pl.BlockSpec((tm, D_MODEL), row),
            _const_spec((1, D_MODEL)),
            _const_spec(wd.shape),
            _const_spec((1, MLA_Q_RANK)),
            _const_spec((1, MLA_KV_RANK)),
            _const_spec(wq.shape),
            _const_spec(wkv.shape),
            pl.BlockSpec((tm, LANE), lambda i: (i % nt, 0)),
            pl.BlockSpec((tm, LANE), lambda i: (i % nt, 0)),
        ],
        out_specs=[
            pl.BlockSpec((tm, MLA_QN), row),
            pl.BlockSpec((tm, MLA_QR), row),
            pl.BlockSpec((tm, MLA_KV), row),
            pl.BlockSpec((tm, 2 * LANE), row),
        ],
        out_shape=[
            jax.ShapeDtypeStruct((rows, MLA_QN), BF16),
            jax.ShapeDtypeStruct((rows, MLA_QR), BF16),
            jax.ShapeDtypeStruct((rows, MLA_KV), BF16),
            jax.ShapeDtypeStruct((rows, 2 * LANE), BF16),
        ],
        compiler_params=_params(("parallel",)),
        name="mla_proj",
    )(h, ln, wd, q_norm, kv_norm, wq, wkv, cos4, sin4)


def _attn_kernel(qn_ref, qr_ref, kn_ref, kr_ref, v_ref, o_ref, *, tq):
    qi = pl.program_id(2)
    q = jnp.concatenate([qn_ref[...], qr_ref[...]], axis=-1)

    def scores(kb):
        r0 = pl.multiple_of(kb * tq, tq)
        k = jnp.concatenate([kn_ref[pl.ds(r0, tq), :], kr_ref[pl.ds(r0, tq), :]], axis=-1)
        s = lax.dot_general(q, k, (((1,), (1,)), ((), ())), preferred_element_type=F32)
        return s, v_ref[pl.ds(r0, tq), :]

    def update(s, v, carry):
        m, l, acc = carry
        m_new = jnp.maximum(m, jnp.max(s, axis=-1, keepdims=True))
        alpha = jnp.exp(m - m_new)
        p = jnp.exp(s - m_new)
        l = alpha * l + jnp.sum(p, axis=-1, keepdims=True)
        acc = alpha * acc + jnp.dot(p.astype(BF16), v, preferred_element_type=F32)
        return m_new, l, acc

    def body(kb, carry):
        s, v = scores(kb)
        return update(s, v, carry)

    init = (jnp.full((tq, 1), -1e30, F32), jnp.zeros((tq, 1), F32), jnp.zeros((tq, MLA_V), F32))
    carry = lax.fori_loop(0, qi, body, init)
    s, v = scores(qi)
    row = lax.broadcasted_iota(jnp.int32, s.shape, 0)
    col = lax.broadcasted_iota(jnp.int32, s.shape, 1)
    s = jnp.where(row >= col, s, -1e30)
    _, l, acc = update(s, v, carry)
    o_ref[...] = (acc / l).astype(BF16)


def _attention(qn, qr, kv, krp, nb, seq, tq):
    nh = MLA_HEADS
    nq = seq // tq
    return pl.pallas_call(
        functools.partial(_attn_kernel, tq=tq),
        grid=(nb, nh, nq),
        in_specs=[
            pl.BlockSpec((tq, LANE), lambda b, h, i: (b * nq + i, h)),
            pl.BlockSpec((tq, LANE), lambda b, h, i: (b * nq + i, h // 2)),
            pl.BlockSpec((seq, LANE), lambda b, h, i: (b, 2 * h)),
            pl.BlockSpec((seq, LANE), lambda b, h, i: (b, h % 2)),
            pl.BlockSpec((seq, LANE), lambda b, h, i: (b, 2 * h + 1)),
        ],
        out_specs=pl.BlockSpec((tq, LANE), lambda b, h, i: (b * nq + i, h)),
        out_shape=jax.ShapeDtypeStruct((nb * seq, nh * MLA_V), BF16),
        compiler_params=_params(("parallel", "parallel", "arbitrary")),
        name="mla_attention",
    )(qn, qr, kv, krp, kv)


def _rope_tables(n, dim):
    inv = ROPE_BASE ** (-jnp.arange(0, dim, 2, dtype=F32) / dim)
    ang = jnp.arange(n, dtype=F32)[:, None] * inv[None, :]
    return jnp.cos(ang), jnp.sin(ang)


def _rot_cols(w):
    lead = w.shape[:-1]
    w = w.reshape(lead + (-1, 2, MLA_ROPE // 2))
    return jnp.stack([-w[..., 1, :], w[..., 0, :]], axis=-2).reshape(lead + (-1,))


def kernel(x, ln_mix_even, w_in_even, s5_lambda_re, s5_lambda_im, s5_log_dt, s5_b_re, s5_b_im, s5_c_re, s5_c_im, s5_d, s5_glu_w, s5_glu_b, ret_norm, w_out_even, ln_mix_odd, mla_w_down, mla_q_norm, mla_w_uq, mla_kv_norm, mla_w_ukv, w_out_odd, ln_ffn, ffn_w_up, ffn_conv_w, ffn_conv_b, ffn_w_down, ln_final):
    nb, seq, d = x.shape
    assert d == D_MODEL and nb % 8 == 0 and seq % 128 == 0
    rows = nb * seq
    tm = min(seq, 1024)
    tmo = min(seq, 512)
    tf = 512
    row2 = lambda a: a.reshape(1, -1)

    x2d = x.reshape(rows, d)

    ret_cos, ret_sin = _rope_tables(seq, RET_HEAD_DIM)
    u, proj = _inproj(x2d, row2(ln_mix_even[0]), w_in_even[0].astype(BF16), ret_cos, ret_sin, seq, tm)

    ab_re, ab_im, bbr, bbi = _s5_discretise(s5_lambda_re[0], s5_lambda_im[0], s5_log_dt[0], s5_b_re[0], s5_b_im[0])
    a_re, a_im, bblk, cblk = _s5_pack(ab_re, ab_im, bbr, bbi, s5_c_re[0], s5_c_im[0])
    y_s5 = _s5(u.reshape(nb, seq, S5_WIDTH), bblk, cblk, a_re, a_im, row2(s5_d[0]),
               s5_glu_w[0].astype(BF16), row2(s5_glu_b[0]), tc=min(seq, 32))
    y_ret = _retention(proj, row2(ret_norm[0]), nb, seq)

    wo = w_out_even[0].astype(BF16)
    h = _outproj(x2d, [y_s5.reshape(rows, S5_WIDTH), y_ret], [wo[:S5_WIDTH], wo[S5_WIDTH:]], tmo, "even_outproj")

    def ffn(h, layer, final_norm):
        return _ffn(h, row2(ln_ffn[layer]), ffn_w_up[layer].astype(BF16), ffn_conv_w[layer],
                    row2(ffn_conv_b[layer]), ffn_w_down[layer].astype(BF16), row2(ln_final),
                    seq, tm, tf, final_norm)

    h = ffn(h, 0, False)

    mla_cos, mla_sin = _rope_tables(seq, MLA_ROPE)
    cos4 = jnp.tile(mla_cos, (1, 4))
    sin4 = jnp.tile(mla_sin, (1, 4))
    wdn = mla_w_down[0]
    w_kr = wdn[:, MLA_Q_RANK + MLA_KV_RANK:]
    w_krp = _rot_cols(w_kr)
    wd = jnp.concatenate([wdn[:, :MLA_Q_RANK + MLA_KV_RANK], w_kr, w_kr, w_krp, w_krp], axis=1).astype(BF16)
    wq3 = mla_w_uq[0].reshape(MLA_Q_RANK, MLA_HEADS, MLA_NOPE + MLA_ROPE)
    wq_n = wq3[:, :, :MLA_NOPE].reshape(MLA_Q_RANK, MLA_QN)
    wq_r = wq3[:, :, MLA_NOPE:].reshape(MLA_Q_RANK, MLA_QR)
    wq = jnp.concatenate([wq_n, wq_r, _rot_cols(wq_r)], axis=1).astype(BF16)
    qn, qr, kv, krp = _mla_proj(h, row2(ln_mix_odd[0]), wd, row2(mla_q_norm[0]), row2(mla_kv_norm[0]),
                                wq, mla_w_ukv[0].astype(BF16), cos4, sin4, seq, tmo)
    o = _attention(qn, qr, kv, krp, nb, seq, tq=min(seq, 512))
    h = _outproj(h, [o], [w_out_odd[0].astype(BF16)], tmo, "odd_outproj")
    h = ffn(h, 1, True)
    return h.reshape(nb, seq, d)
```

```python
import functools

import numpy as np
import jax
import jax.numpy as jnp
from jax import lax
from jax.experimental import pallas as pl
from jax.experimental.pallas import tpu as pltpu

F32 = jnp.float32
BF16 = jnp.bfloat16

EPS = 1e-6
ROPE_BASE = 10000.0
D_MODEL = 2048
S5_WIDTH = 512
S5_GROUP = 16
S5_GROUPS = 32
S5_STATE = 64
S5_GB = 8
S5_NBLK = S5_GROUPS // S5_GB
S5_NSTATE = S5_GROUPS * S5_STATE
RET_HEADS = 6
RET_HEAD_DIM = 256
RET_WIDTH = RET_HEADS * RET_HEAD_DIM
RET_CHUNK = 128
EVEN_IN = S5_WIDTH + 4 * RET_WIDTH
MLA_HEADS = 16
MLA_NOPE = 128
MLA_ROPE = 64
MLA_V = 128
MLA_Q_RANK = 512
MLA_KV_RANK = 512
D_FF = 5632
FFN_SUBTILES = 2

LANE = 128
BF16_ROWS = 16
VMEM_LIMIT = 60 * 1024 * 1024


def _params(sem, vmem=VMEM_LIMIT):
    return pltpu.CompilerParams(dimension_semantics=sem, vmem_limit_bytes=vmem)


def _rms(x, g):
    return x * lax.rsqrt(jnp.mean(x * x, axis=-1, keepdims=True) + EPS) * g


def _const_spec(shape):
    nd = len(shape)
    return pl.BlockSpec(shape, lambda *_: (0,) * nd, pipeline_mode=pl.Buffered(1))


IN_TN = 512
IN_NJ = EVEN_IN // IN_TN
IN_SUBTILES = 2


def _inproj_kernel(h_ref, ln_ref, w_ref, cos_ref, sin_ref, u_ref, p_ref, hn_ref):
    j = pl.program_id(1)

    @pl.when(j == 0)
    def _():
        hn_ref[...] = _rms(h_ref[...], ln_ref[...]).astype(BF16)

    ts = hn_ref.shape[0] // IN_SUBTILES

    def tiles():
        for sub in range(IN_SUBTILES):
            rs = slice(sub * ts, (sub + 1) * ts)
            yield rs, jnp.dot(hn_ref[rs, :], w_ref[...], preferred_element_type=F32)

    @pl.when(j == 0)
    def _():
        for rs, acc in tiles():
            u_ref[rs, :] = acc

    @pl.when(jnp.logical_and(j >= 1, j <= 6))
    def _():
        sc = jnp.where(j >= 4, RET_HEAD_DIM ** -0.5, 1.0).astype(F32)
        for rs, acc in tiles():
            c = cos_ref[rs, :] * sc
            s = sin_ref[rs, :] * sc
            for hh in range(IN_TN // RET_HEAD_DIM):
                o = hh * RET_HEAD_DIM
                x1 = acc[:, o:o + LANE]
                x2 = acc[:, o + LANE:o + 2 * LANE]
                p_ref[rs, o:o + LANE] = (x1 * c - x2 * s).astype(BF16)
                p_ref[rs, o + LANE:o + 2 * LANE] = (x2 * c + x1 * s).astype(BF16)

    @pl.when(jnp.logical_and(j >= 7, j <= 9))
    def _():
        for rs, acc in tiles():
            p_ref[rs, :] = acc.astype(BF16)

    @pl.when(j >= 10)
    def _():
        for rs, acc in tiles():
            p_ref[rs, :] = (acc * jax.nn.sigmoid(acc)).astype(BF16)


def _inproj(x2d, ln, w, cos, sin, seq, tm):
    rows = x2d.shape[0]
    nt = seq // tm
    return pl.pallas_call(
        _inproj_kernel,
        grid=(rows // tm, IN_NJ),
        in_specs=[
            pl.BlockSpec((tm, D_MODEL), lambda i, j: (i, 0)),
            pl.BlockSpec((1, D_MODEL), lambda i, j: (0, 0)),
            pl.BlockSpec((D_MODEL, IN_TN), lambda i, j: (0, j)),
            pl.BlockSpec((tm, LANE), lambda i, j: (i % nt, 0)),
            pl.BlockSpec((tm, LANE), lambda i, j: (i % nt, 0)),
        ],
        out_specs=[
            pl.BlockSpec((tm, IN_TN), lambda i, j: (i, 0)),
            pl.BlockSpec((tm, IN_TN), lambda i, j: (i, jnp.maximum(j - 1, 0))),
        ],
        out_shape=[
            jax.ShapeDtypeStruct((rows, S5_WIDTH), F32),
            jax.ShapeDtypeStruct((rows, EVEN_IN - S5_WIDTH), BF16),
        ],
        scratch_shapes=[pltpu.VMEM((tm, D_MODEL), BF16)],
        compiler_params=_params(("parallel", "arbitrary")),
        name="even_inproj",
    )(x2d, ln, w, cos, sin)


def _s5_disc_kernel(lr_ref, li_ref, ldt_ref, br_ref, bi_ref, are_ref, aim_ref, bbr_ref, bbi_ref):
    lr = lr_ref[...]
    li = li_ref[...]
    dt = jnp.exp(ldt_ref[...])
    mag = jnp.exp(lr * dt)
    ab_re = mag * jnp.cos(li * dt)
    ab_im = mag * jnp.sin(li * dt)
    den = lr * lr + li * li
    nr = ab_re - 1.0
    ni = ab_im
    f_re = (nr * lr + ni * li) / den
    f_im = (ni * lr - nr * li) / den
    are_ref[...] = ab_re
    aim_ref[...] = ab_im
    br = br_ref[...]
    bi = bi_ref[...]
    bbr_ref[...] = f_re[:, None, :] * br - f_im[:, None, :] * bi
    bbi_ref[...] = f_re[:, None, :] * bi + f_im[:, None, :] * br


def _s5_discretise(lam_re, lam_im, log_dt, b_re, b_im):
    g, p, hh = S5_GROUPS, S5_STATE, S5_GROUP
    brt = jnp.swapaxes(b_re, 1, 2)
    bit = jnp.swapaxes(b_im, 1, 2)
    return pl.pallas_call(
        _s5_disc_kernel,
        out_shape=[
            jax.ShapeDtypeStruct((g, p), F32),
            jax.ShapeDtypeStruct((g, p), F32),
            jax.ShapeDtypeStruct((g, hh, p), F32),
            jax.ShapeDtypeStruct((g, hh, p), F32),
        ],
        name="s5_discretise",
    )(lam_re, lam_im, log_dt.reshape(g, 1), brt, bit)


def _s5_pack(ab_re, ab_im, bbr, bbi, c_re, c_im):
    eye = jnp.eye(S5_GB, dtype=F32)
    nb, gb, p, hh = S5_NBLK, S5_GB, S5_STATE, S5_GROUP

    def pack_b(bb):
        return jnp.einsum('gy,aghp->aghyp', eye, bb.reshape(nb, gb, hh, p)).reshape(nb, gb * hh, gb * p)

    def pack_c(cc):
        return jnp.einsum('gy,aghp->agpyh', eye, cc.reshape(nb, gb, hh, p)).reshape(nb, gb * p, gb * hh)

    bblk = jnp.concatenate([pack_b(bbr), pack_b(bbi)], axis=-1).astype(BF16)
    cblk = jnp.concatenate([pack_c(c_re), pack_c(-c_im)], axis=1).astype(BF16)
    return ab_re.reshape(1, S5_NSTATE), ab_im.reshape(1, S5_NSTATE), bblk, cblk


S5_SLAB = 256


def _s5_kernel(u_hbm, bblk_ref, cblk_ref, are_ref, aim_ref, d_ref, gw_ref, gb_ref, y_hbm,
               ubuf, ybuf, in_sem, out_sem, bur_ref, bui_ref, xr_ref, xi_ref, sr_ref, si_ref, *, nb, tc):
    t = pl.program_id(0)
    nt = pl.num_programs(0)
    slot = t % 2
    half = S5_GB * S5_STATE
    wu = S5_GB * S5_GROUP

    def in_copy(step, sl, b):
        return pltpu.make_async_copy(u_hbm.at[b, pl.ds(step * tc, tc), :], ubuf.at[sl, :, b, :], in_sem.at[sl, b])

    def out_copy(step, sl, b):
        return pltpu.make_async_copy(ybuf.at[sl, :, b, :], y_hbm.at[b, pl.ds(step * tc, tc), :], out_sem.at[sl, b])

    @pl.when(t == 0)
    def _():
        for b in range(nb):
            in_copy(0, 0, b).start()
        sr_ref[...] = jnp.zeros_like(sr_ref)
        si_ref[...] = jnp.zeros_like(si_ref)

    @pl.when(t + 1 < nt)
    def _():
        for b in range(nb):
            in_copy(t + 1, 1 - slot, b).start()

    for b in range(nb):
        in_copy(t, slot, b).wait()

    u = ubuf[slot].reshape(tc * nb, S5_WIDTH)
    ub = u.astype(BF16)
    for a in range(S5_NBLK):
        bu = jnp.dot(ub[:, a * wu:(a + 1) * wu], bblk_ref[a], preferred_element_type=F32)
        bur_ref[:, a * half:(a + 1) * half] = bu[:, :half]
        bui_ref[:, a * half:(a + 1) * half] = bu[:, half:]

    for sl in range(S5_NSTATE // S5_SLAB):
        cs = slice(sl * S5_SLAB, (sl + 1) * S5_SLAB)
        ar = jnp.broadcast_to(are_ref[:, cs], (nb, S5_SLAB))
        ai = jnp.broadcast_to(aim_ref[:, cs], (nb, S5_SLAB))

        def step(k, carry, cs=cs, ar=ar, ai=ai):
            xr, xi = carry
            r0 = pl.multiple_of(k * nb, nb)
            nr = ar * xr - ai * xi + bur_ref[pl.ds(r0, nb), cs]
            ni = ar * xi + ai * xr + bui_ref[pl.ds(r0, nb), cs]
            xr_ref[pl.ds(r0, nb), cs] = nr.astype(BF16)
            xi_ref[pl.ds(r0, nb), cs] = ni.astype(BF16)
            return nr, ni

        xr, xi = lax.fori_loop(0, tc, step, (sr_ref[:, cs], si_ref[:, cs]), unroll=4)
        sr_ref[:, cs] = xr
        si_ref[:, cs] = xi

    ys = []
    for a in range(S5_NBLK):
        ya = jnp.dot(xr_ref[:, a * half:(a + 1) * half], cblk_ref[a, :half, :], preferred_element_type=F32)
        ya = ya + jnp.dot(xi_ref[:, a * half:(a + 1) * half], cblk_ref[a, half:, :], preferred_element_type=F32)
        ys.append(ya)
    y = jnp.concatenate(ys, axis=-1) + d_ref[...] * u
    y = jax.nn.gelu(y)
    z = jnp.dot(y.astype(BF16), gw_ref[...], preferred_element_type=F32) + gb_ref[...]
    y = y * jax.nn.sigmoid(z)

    @pl.when(t >= 2)
    def _():
        for b in range(nb):
            out_copy(t - 2, slot, b).wait()

    ybuf[slot] = y.reshape(tc, nb, S5_WIDTH)
    for b in range(nb):
        out_copy(t, slot, b).start()

    @pl.when(t == nt - 1)
    def _():
        for b in range(nb):
            out_copy(t, slot, b).wait()

        @pl.when(nt >= 2)
        def _():
            for b in range(nb):
                out_copy(t - 1, 1 - slot, b).wait()


def _s5(u3, bblk, cblk, a_re, a_im, d_skip, glu_w, glu_b, tc):
    nb, seq, _ = u3.shape
    rows = nb * tc
    return pl.pallas_call(
        functools.partial(_s5_kernel, nb=nb, tc=tc),
        grid=(seq // tc,),
        in_specs=[
            pl.BlockSpec(memory_space=pl.ANY),
            _const_spec(bblk.shape),
            _const_spec(cblk.shape),
            _const_spec((1, S5_NSTATE)),
            _const_spec((1, S5_NSTATE)),
            _const_spec((1, S5_WIDTH)),
            _const_spec((S5_WIDTH, S5_WIDTH)),
            _const_spec((1, S5_WIDTH)),
        ],
        out_specs=pl.BlockSpec(memory_space=pl.ANY),
        out_shape=jax.ShapeDtypeStruct((nb, seq, S5_WIDTH), F32),
        scratch_shapes=[
            pltpu.VMEM((2, tc, nb, S5_WIDTH), F32),
            pltpu.VMEM((2, tc, nb, S5_WIDTH), F32),
            pltpu.SemaphoreType.DMA((2, nb)),
            pltpu.SemaphoreType.DMA((2, nb)),
            pltpu.VMEM((rows, S5_NSTATE), F32),
            pltpu.VMEM((rows, S5_NSTATE), F32),
            pltpu.VMEM((rows, S5_NSTATE), BF16),
            pltpu.VMEM((rows, S5_NSTATE), BF16),
            pltpu.VMEM((nb, S5_NSTATE), F32),
            pltpu.VMEM((nb, S5_NSTATE), F32),
        ],
        compiler_params=_params(("arbitrary",)),
        name="s5_scan",
    )(u3, bblk, cblk, a_re, a_im, d_skip, glu_w, glu_b)


def _retention_consts():
    hs = np.arange(RET_HEADS, dtype=np.float32)
    gamma_log = np.log((1.0 - np.exp2(-5.0 - hs)).astype(np.float32)).astype(np.float32)
    c = RET_CHUNK
    idx = np.arange(c, dtype=np.float32)
    diff = idx[:, None] - idx[None, :]
    dmask = np.where(diff[None] >= 0, np.exp(np.maximum(diff[None], 0.0) * gamma_log[:, None, None]), 0.0)
    xi = np.exp((idx[None, :] + 1.0) * gamma_log[:, None])
    zeta = np.exp((c - 1.0 - idx[None, :]) * gamma_log[:, None])
    g_chunk = np.exp(c * gamma_log)
    xi_b = np.broadcast_to(xi[:, :, None], (RET_HEADS, c, RET_HEAD_DIM))
    zeta_b = np.broadcast_to(zeta[:, :, None], (RET_HEADS, c, RET_HEAD_DIM))
    return (dmask.astype(np.float32), np.ascontiguousarray(xi_b, np.float32),
            np.ascontiguousarray(zeta_b, np.float32), [float(v) for v in g_chunk.astype(np.float32)])


def _retention_kernel(q_ref, k_ref, v_ref, sg_ref, dm_ref, xi_ref, zeta_ref, rn_ref, y_ref, r_ref, *, g_chunk):
    c = pl.program_id(1)

    @pl.when(c == 0)
    def _():
        r_ref[...] = jnp.zeros_like(r_ref)

    for h in range(RET_HEADS):
        hs = slice(h * RET_HEAD_DIM, (h + 1) * RET_HEAD_DIM)
        q = q_ref[:, hs]
        k = k_ref[:, hs]
        v = v_ref[:, hs]
        r = r_ref[h]
        s = lax.dot_general(q, k, (((1,), (1,)), ((), ())), preferred_element_type=F32)
        inner = (s * dm_ref[h]).astype(BF16)
        o = jnp.dot(inner, v, preferred_element_type=F32)
        o = o + xi_ref[h] * jnp.dot(q, r.astype(BF16), preferred_element_type=F32)
        kz = (k.astype(F32) * zeta_ref[h]).astype(BF16)
        r_ref[h] = r * g_chunk[h] + lax.dot_general(kz, v, (((0,), (0,)), ((), ())), preferred_element_type=F32)
        o = o * lax.rsqrt(jnp.mean(o * o, axis=-1, keepdims=True) + EPS)
        o = o * rn_ref[:, hs]
        y_ref[:, hs] = (sg_ref[:, hs].astype(F32) * o).astype(BF16)


def _retention(proj, ret_norm, nb, seq):
    dmask, xi_b, zeta_b, g_chunk = _retention_consts()
    nc = seq // RET_CHUNK
    blk = (RET_CHUNK, RET_WIDTH)
    return pl.pallas_call(
        functools.partial(_retention_kernel, g_chunk=g_chunk),
        grid=(nb, nc),
        in_specs=[
            pl.BlockSpec(blk, lambda b, c: (b * nc + c, 0)),
            pl.BlockSpec(blk, lambda b, c: (b * nc + c, 1)),
            pl.BlockSpec(blk, lambda b, c: (b * nc + c, 2)),
            pl.BlockSpec(blk, lambda b, c: (b * nc + c, 3)),
            _const_spec(dmask.shape),
            _const_spec(xi_b.shape),
            _const_spec(zeta_b.shape),
            _const_spec((1, RET_WIDTH)),
        ],
        out_specs=pl.BlockSpec(blk, lambda b, c: (b * nc + c, 0)),
        out_shape=jax.ShapeDtypeStruct((nb * seq, RET_WIDTH), BF16),
        scratch_shapes=[pltpu.VMEM((RET_HEADS, RET_HEAD_DIM, RET_HEAD_DIM), F32)],
        compiler_params=_params(("parallel", "arbitrary")),
        name="retention",
    )(proj, proj, proj, proj, jnp.asarray(dmask), jnp.asarray(xi_b), jnp.asarray(zeta_b), ret_norm)


def _outproj_kernel(*refs, n_in):
    res_ref = refs[0]
    ys = refs[1:1 + n_in]
    ws = refs[1 + n_in:1 + 2 * n_in]
    o_ref = refs[1 + 2 * n_in]
    acc = res_ref[...]
    for y, w in zip(ys, ws):
        acc = acc + jnp.dot(y[...].astype(BF16), w[...], preferred_element_type=F32)
    o_ref[...] = acc


def _outproj(res, ys, ws, tm, name):
    rows, d = res.shape
    n_in = len(ys)
    row = lambda i: (i, 0)
    return pl.pallas_call(
        functools.partial(_outproj_kernel, n_in=n_in),
        grid=(rows // tm,),
        in_specs=([pl.BlockSpec((tm, d), row)] + [pl.BlockSpec((tm, y.shape[1]), row) for y in ys]
                  + [_const_spec(w.shape) for w in ws]),
        out_specs=pl.BlockSpec((tm, d), row),
        out_shape=jax.ShapeDtypeStruct((rows, d), F32),
        compiler_params=_params(("parallel",)),
        name=name,
    )(res, *ys, *ws)


def _ffn_kernel(h_ref, halo_ref, ln_ref, wg_ref, wv_ref, cwg_ref, cwv_ref, cbg_ref, cbv_ref, wd_ref, lnf_ref,
                o_ref, hn_ref, *, nt, tm, final_norm):
    i = pl.program_id(0)
    j = pl.program_id(1)
    hl = BF16_ROWS

    @pl.when(j == 0)
    def _():
        hn_ref[hl:, :] = _rms(h_ref[...], ln_ref[...]).astype(BF16)
        halo = _rms(halo_ref[...], ln_ref[...])
        hn_ref[:hl, :] = jnp.where(i % nt > 0, halo, 0.0).astype(BF16)
        o_ref[...] = h_ref[...]

    ts = tm // FFN_SUBTILES
    for sub in range(FFN_SUBTILES):
        r0 = sub * ts
        hn = hn_ref[r0:r0 + hl + ts, :]

        def conv(w_ref, cw_ref, cb_ref, hn=hn):
            a = jnp.dot(hn, w_ref[...], preferred_element_type=F32)
            cw = cw_ref[...]
            return (a[hl - 2:hl - 2 + ts] * cw[0:1] + a[hl - 1:hl - 1 + ts] * cw[1:2] + a[hl:] * cw[2:3]
                    + cb_ref[...])

        gate = conv(wg_ref, cwg_ref, cbg_ref)
        val = conv(wv_ref, cwv_ref, cbv_ref)
        g = (gate * jax.nn.sigmoid(gate) * val).astype(BF16)
        o_ref[r0:r0 + ts, :] += jnp.dot(g, wd_ref[...], preferred_element_type=F32)

    if final_norm:
        @pl.when(j == pl.num_programs(1) - 1)
        def _():
            o_ref[...] = _rms(o_ref[...], lnf_ref[...])


def _ffn(h, ln, w_up, conv_w, conv_b, w_down, ln_final, seq, tm, tf, final_norm):
    rows = h.shape[0]
    hl = BF16_ROWS
    nj = D_FF // tf
    nt = seq // tm
    return pl.pallas_call(
        functools.partial(_ffn_kernel, nt=nt, tm=tm, final_norm=final_norm),
        grid=(rows // tm, nj),
        in_specs=[
            pl.BlockSpec((tm, D_MODEL), lambda i, j: (i, 0), pipeline_mode=pl.Buffered(1)),
            pl.BlockSpec((hl, D_MODEL), lambda i, j: (jnp.maximum(i * (tm // hl) - 1, 0), 0)),
            pl.BlockSpec((1, D_MODEL), lambda i, j: (0, 0)),
            pl.BlockSpec((D_MODEL, tf), lambda i, j: (0, j)),
            pl.BlockSpec((D_MODEL, tf), lambda i, j: (0, nj + j)),
            pl.BlockSpec((3, tf), lambda i, j: (0, j)),
            pl.BlockSpec((3, tf), lambda i, j: (0, nj + j)),
            pl.BlockSpec((1, tf), lambda i, j: (0, j)),
            pl.BlockSpec((1, tf), lambda i, j: (0, nj + j)),
            pl.BlockSpec((tf, D_MODEL), lambda i, j: (j, 0)),
            pl.BlockSpec((1, D_MODEL), lambda i, j: (0, 0)),
        ],
        out_specs=pl.BlockSpec((tm, D_MODEL), lambda i, j: (i, 0)),
        out_shape=jax.ShapeDtypeStruct((rows, D_MODEL), F32),
        scratch_shapes=[pltpu.VMEM((hl + tm, D_MODEL), BF16)],
        compiler_params=_params(("parallel", "arbitrary")),
        name="conv_ffn",
    )(h, h, ln, w_up, w_up, conv_w, conv_w, conv_b, conv_b, w_down, ln_final)


MLA_DOWN_PAD = MLA_Q_RANK + MLA_KV_RANK + 2 * LANE
MLA_QN = MLA_HEADS * MLA_NOPE
MLA_QR = MLA_HEADS * MLA_ROPE
MLA_KV = MLA_HEADS * (MLA_NOPE + MLA_V)
MLA_CHUNK = 1024
LOG2E = 1.4426950408889634


def _mla_proj_kernel(h_ref, ln_ref, wd_ref, qn_ref, kvn_ref, wq_ref, wkv_ref, cos_ref, sin_ref,
                     oqn_ref, oqr_ref, okv_ref, okr_ref):
    hn = _rms(h_ref[...], ln_ref[...]).astype(BF16)
    c = jnp.dot(hn, wd_ref[...], preferred_element_type=F32)
    cos = cos_ref[...]
    sin = sin_ref[...]
    scale = (MLA_NOPE + MLA_ROPE) ** -0.5 * LOG2E

    cq = _rms(c[:, :MLA_Q_RANK], qn_ref[...]).astype(BF16)
    for n0 in range(0, MLA_QN, MLA_CHUNK):
        qn = jnp.dot(cq, wq_ref[:, n0:n0 + MLA_CHUNK], preferred_element_type=F32)
        oqn_ref[:, n0:n0 + MLA_CHUNK] = (qn * scale).astype(BF16)
    cs = cos * scale
    ss = sin * scale
    for n0 in range(0, MLA_QR, MLA_CHUNK):
        qr = jnp.dot(cq, wq_ref[:, MLA_QN + n0:MLA_QN + n0 + MLA_CHUNK], preferred_element_type=F32)
        qp = jnp.dot(cq, wq_ref[:, MLA_QN + MLA_QR + n0:MLA_QN + MLA_QR + n0 + MLA_CHUNK],
                     preferred_element_type=F32)
        for m in range(MLA_CHUNK // LANE):
            ls = slice(m * LANE, (m + 1) * LANE)
            oqr_ref[:, n0 + m * LANE:n0 + (m + 1) * LANE] = (qr[:, ls] * cs + qp[:, ls] * ss).astype(BF16)

    ckv = _rms(c[:, MLA_Q_RANK:MLA_Q_RANK + MLA_KV_RANK], kvn_ref[...]).astype(BF16)
    for n0 in range(0, MLA_KV, MLA_CHUNK):
        okv_ref[:, n0:n0 + MLA_CHUNK] = jnp.dot(
            ckv, wkv_ref[:, n0:n0 + MLA_CHUNK], preferred_element_type=F32).astype(BF16)

    o = MLA_Q_RANK + MLA_KV_RANK
    kr2 = c[:, o:o + LANE] * cos + c[:, o + LANE:o + 2 * LANE] * sin
    left = lax.broadcasted_iota(jnp.int32, kr2.shape, 1) < MLA_ROPE
    okr_ref[:, :LANE] = jnp.where(left, kr2, 0.0).astype(BF16)
    okr_ref[:, LANE:] = jnp.where(left, 0.0, kr2).astype(BF16)


def _mla_proj(h, ln, wd, q_norm, kv_norm, wq, wkv, cos4, sin4, seq, tm):
    rows = h.shape[0]
    nt = seq // tm
    row = lambda i: (i, 0)
    return pl.pallas_call(
        _mla_proj_kernel,
        grid=(rows // tm,),
        in_specs=[
            pl.BlockSpec((tm, D_MODEL), row),
            _const_spec((1, D_MODEL)),
            _const_spec(wd.shape),
            _const_spec((1, MLA_Q_RANK)),
            _const_spec((1, MLA_KV_RANK)),
            _const_spec(wq.shape),
            _const_spec(wkv.shape),
            pl.BlockSpec((tm, LANE), lambda i: (i % nt, 0)),
            pl.BlockSpec((tm, LANE), lambda i: (i % nt, 0)),
        ],
        out_specs=[
            pl.BlockSpec((tm, MLA_QN), row),
            pl.BlockSpec((tm, MLA_QR), row),
            pl.BlockSpec((tm, MLA_KV), row),
            pl.BlockSpec((tm, 2 * LANE), row),
        ],
        out_shape=[
            jax.ShapeDtypeStruct((rows, MLA_QN), BF16),
            jax.ShapeDtypeStruct((rows, MLA_QR), BF16),
            jax.ShapeDtypeStruct((rows, MLA_KV), BF16),
            jax.ShapeDtypeStruct((rows, 2 * LANE), BF16),
        ],
        compiler_params=_params(("parallel",)),
        name="mla_proj",
    )(h, ln, wd, q_norm, kv_norm, wq, wkv, cos4, sin4)


def _attn_kernel(qn_ref, qr_ref, kv_ref, kr_ref, o_ref, *, tq, nq):
    qi = pl.program_id(2)
    qr = qr_ref[...]
    qs = [jnp.concatenate([qn_ref[:, s * LANE:(s + 1) * LANE], qr], axis=-1) for s in range(2)]

    def keys(s, r0, n):
        k = jnp.concatenate([kv_ref[pl.ds(r0, n), 2 * s * LANE:(2 * s + 1) * LANE],
                             kr_ref[pl.ds(r0, n), s * LANE:(s + 1) * LANE]], axis=-1)
        return k, kv_ref[pl.ds(r0, n), (2 * s + 1) * LANE:(2 * s + 2) * LANE]

    def update(q, kv, carry, mask=None):
        k, v = kv
        m, l, acc = carry
        s = lax.dot_general(q, k, (((1,), (1,)), ((), ())), preferred_element_type=F32)
        if mask is not None:
            s = jnp.where(mask, s, -1e30)
        m_new = jnp.maximum(m, jnp.max(s, axis=-1, keepdims=True))
        alpha = jnp.exp2(m - m_new)
        p = jnp.exp2(s - m_new)
        l = alpha * l + jnp.sum(p, axis=-1, keepdims=True)
        acc = alpha * acc + jnp.dot(p.astype(BF16), v, preferred_element_type=F32)
        return m_new, l, acc

    row = lax.broadcasted_iota(jnp.int32, (tq, tq), 0)
    col = lax.broadcasted_iota(jnp.int32, (tq, tq), 1)

    for c in range(nq):
        @pl.when(qi == c)
        def _(c=c):
            carries = [(jnp.full((tq, 1), -1e30, F32), jnp.zeros((tq, 1), F32), jnp.zeros((tq, MLA_V), F32))
                       for _ in range(2)]
            for kb in range(c):
                carries = [update(qs[s], keys(s, kb * tq, tq), carries[s]) for s in range(2)]
            for s in range(2):
                _, l, acc = update(qs[s], keys(s, c * tq, tq), carries[s], row >= col)
                o_ref[:, s * LANE:(s + 1) * LANE] = (acc / l).astype(BF16)


def _attention(qn, qr, kv, krp, nb, seq, tq):
    npair = MLA_HEADS // 2
    nq = seq // tq
    return pl.pallas_call(
        functools.partial(_attn_kernel, tq=tq, nq=nq),
        grid=(nb, npair, nq),
        in_specs=[
            pl.BlockSpec((tq, 2 * LANE), lambda b, h, i: (b * nq + i, h)),
            pl.BlockSpec((tq, LANE), lambda b, h, i: (b * nq + i, h)),
            pl.BlockSpec((seq, 4 * LANE), lambda b, h, i: (b, h)),
            pl.BlockSpec((seq, 2 * LANE), lambda b, h, i: (b, 0)),
        ],
        out_specs=pl.BlockSpec((tq, 2 * LANE), lambda b, h, i: (b * nq + i, h)),
        out_shape=jax.ShapeDtypeStruct((nb * seq, MLA_HEADS * MLA_V), BF16),
        compiler_params=_params(("parallel", "parallel", "arbitrary")),
        name="mla_attention",
    )(qn, qr, kv, krp)


def _rope_tables(n, dim):
    inv = ROPE_BASE ** (-jnp.arange(0, dim, 2, dtype=F32) / dim)
    ang = jnp.arange(n, dtype=F32)[:, None] * inv[None, :]
    return jnp.cos(ang), jnp.sin(ang)


def _rot_cols(w):
    lead = w.shape[:-1]
    w = w.reshape(lead + (-1, 2, MLA_ROPE // 2))
    return jnp.stack([-w[..., 1, :], w[..., 0, :]], axis=-2).reshape(lead + (-1,))


def kernel(x, ln_mix_even, w_in_even, s5_lambda_re, s5_lambda_im, s5_log_dt, s5_b_re, s5_b_im, s5_c_re, s5_c_im, s5_d, s5_glu_w, s5_glu_b, ret_norm, w_out_even, ln_mix_odd, mla_w_down, mla_q_norm, mla_w_uq, mla_kv_norm, mla_w_ukv, w_out_odd, ln_ffn, ffn_w_up, ffn_conv_w, ffn_conv_b, ffn_w_down, ln_final):
    nb, seq, d = x.shape
    assert d == D_MODEL and nb % 8 == 0 and seq % 128 == 0
    rows = nb * seq
    tm = min(seq, 1024)
    tmo = min(seq, 512)
    tf = 512
    row2 = lambda a: a.reshape(1, -1)

    x2d = x.reshape(rows, d)

    ret_cos, ret_sin = _rope_tables(seq, RET_HEAD_DIM)
    u, proj = _inproj(x2d, row2(ln_mix_even[0]), w_in_even[0].astype(BF16), ret_cos, ret_sin, seq, tm)

    ab_re, ab_im, bbr, bbi = _s5_discretise(s5_lambda_re[0], s5_lambda_im[0], s5_log_dt[0], s5_b_re[0], s5_b_im[0])
    a_re, a_im, bblk, cblk = _s5_pack(ab_re, ab_im, bbr, bbi, s5_c_re[0], s5_c_im[0])
    y_s5 = _s5(u.reshape(nb, seq, S5_WIDTH), bblk, cblk, a_re, a_im, row2(s5_d[0]),
               s5_glu_w[0].astype(BF16), row2(s5_glu_b[0]), tc=min(seq, 32))
    y_ret = _retention(proj, row2(ret_norm[0]), nb, seq)

    wo = w_out_even[0].astype(BF16)
    h = _outproj(x2d, [y_s5.reshape(rows, S5_WIDTH), y_ret], [wo[:S5_WIDTH], wo[S5_WIDTH:]], tmo, "even_outproj")

    def ffn(h, layer, final_norm):
        return _ffn(h, row2(ln_ffn[layer]), ffn_w_up[layer].astype(BF16), ffn_conv_w[layer],
                    row2(ffn_conv_b[layer]), ffn_w_down[layer].astype(BF16), row2(ln_final),
                    seq, tm, tf, final_norm)

    h = ffn(h, 0, False)

    mla_cos, mla_sin = _rope_tables(seq, MLA_ROPE)
    cos4 = jnp.tile(mla_cos, (1, 4))
    sin4 = jnp.tile(mla_sin, (1, 4))
    wdn = mla_w_down[0]
    w_kr = wdn[:, MLA_Q_RANK + MLA_KV_RANK:]
    w_krp = _rot_cols(w_kr)
    wd = jnp.concatenate([wdn[:, :MLA_Q_RANK + MLA_KV_RANK], w_kr, w_kr, w_krp, w_krp], axis=1).astype(BF16)
    wq3 = mla_w_uq[0].reshape(MLA_Q_RANK, MLA_HEADS, MLA_NOPE + MLA_ROPE)
    wq_n = wq3[:, :, :MLA_NOPE].reshape(MLA_Q_RANK, MLA_QN)
    wq_r = wq3[:, :, MLA_NOPE:].reshape(MLA_Q_RANK, MLA_QR)
    wq = jnp.concatenate([wq_n, wq_r, _rot_cols(wq_r)], axis=1).astype(BF16)
    qn, qr, kv, krp = _mla_proj(h, row2(ln_mix_odd[0]), wd, row2(mla_q_norm[0]), row2(mla_kv_norm[0]),
                                wq, mla_w_ukv[0].astype(BF16), cos4, sin4, seq, tmo)
    o = _attention(qn, qr, kv, krp, nb, seq, tq=min(seq, 512))
    h = _outproj(h, [o], [w_out_odd[0].astype(BF16)], tmo, "odd_outproj")
    h = ffn(h, 1, True)
    return h.reshape(nb, seq, d)
```

```python
import functools

import numpy as np
import jax
import jax.numpy as jnp
from jax import lax
from jax.experimental import pallas as pl
from jax.experimental.pallas import tpu as pltpu

F32 = jnp.float32
BF16 = jnp.bfloat16

EPS = 1e-6
ROPE_BASE = 10000.0
D_MODEL = 2048
S5_WIDTH = 512
S5_GROUP = 16
S5_GROUPS = 32
S5_STATE = 64
S5_GB = 8
S5_NBLK = S5_GROUPS // S5_GB
S5_NSTATE = S5_GROUPS * S5_STATE
RET_HEADS = 6
RET_HEAD_DIM = 256
RET_WIDTH = RET_HEADS * RET_HEAD_DIM
RET_CHUNK = 128
EVEN_IN = S5_WIDTH + 4 * RET_WIDTH
MLA_HEADS = 16
MLA_NOPE = 128
MLA_ROPE = 64
MLA_V = 128
MLA_Q_RANK = 512
MLA_KV_RANK = 512
D_FF = 5632
FFN_SUBTILES = 2

LANE = 128
BF16_ROWS = 16
VMEM_LIMIT = 60 * 1024 * 1024


def _params(sem, vmem=VMEM_LIMIT):
    return pltpu.CompilerParams(dimension_semantics=sem, vmem_limit_bytes=vmem)


def _rms(x, g):
    return x * lax.rsqrt(jnp.mean(x * x, axis=-1, keepdims=True) + EPS) * g


def _const_spec(shape):
    nd = len(shape)
    return pl.BlockSpec(shape, lambda *_: (0,) * nd, pipeline_mode=pl.Buffered(1))


IN_TN = 512
IN_NJ = EVEN_IN // IN_TN
IN_SUBTILES = 2


def _inproj_kernel(h_ref, ln_ref, w_ref, cos_ref, sin_ref, u_ref, p_ref, hn_ref):
    j = pl.program_id(1)

    @pl.when(j == 0)
    def _():
        hn_ref[...] = _rms(h_ref[...], ln_ref[...]).astype(BF16)

    ts = hn_ref.shape[0] // IN_SUBTILES

    def tiles():
        for sub in range(IN_SUBTILES):
            rs = slice(sub * ts, (sub + 1) * ts)
            yield rs, jnp.dot(hn_ref[rs, :], w_ref[...], preferred_element_type=F32)

    @pl.when(j == 0)
    def _():
        for rs, acc in tiles():
            u_ref[rs, :] = acc

    @pl.when(jnp.logical_and(j >= 1, j <= 6))
    def _():
        sc = jnp.where(j >= 4, RET_HEAD_DIM ** -0.5, 1.0).astype(F32)
        for rs, acc in tiles():
            c = cos_ref[rs, :] * sc
            s = sin_ref[rs, :] * sc
            for hh in range(IN_TN // RET_HEAD_DIM):
                o = hh * RET_HEAD_DIM
                x1 = acc[:, o:o + LANE]
                x2 = acc[:, o + LANE:o + 2 * LANE]
                p_ref[rs, o:o + LANE] = (x1 * c - x2 * s).astype(BF16)
                p_ref[rs, o + LANE:o + 2 * LANE] = (x2 * c + x1 * s).astype(BF16)

    @pl.when(jnp.logical_and(j >= 7, j <= 9))
    def _():
        for rs, acc in tiles():
            p_ref[rs, :] = acc.astype(BF16)

    @pl.when(j >= 10)
    def _():
        for rs, acc in tiles():
            p_ref[rs, :] = (acc * jax.nn.sigmoid(acc)).astype(BF16)


def _inproj(x2d, ln, w, cos, sin, seq, tm):
    rows = x2d.shape[0]
    nt = seq // tm
    return pl.pallas_call(
        _inproj_kernel,
        grid=(rows // tm, IN_NJ),
        in_specs=[
            pl.BlockSpec((tm, D_MODEL), lambda i, j: (i, 0)),
            pl.BlockSpec((1, D_MODEL), lambda i, j: (0, 0)),
            pl.BlockSpec((None, D_MODEL, IN_TN), lambda i, j: (j, 0, 0)),
            pl.BlockSpec((tm, LANE), lambda i, j: (i % nt, 0)),
            pl.BlockSpec((tm, LANE), lambda i, j: (i % nt, 0)),
        ],
        out_specs=[
            pl.BlockSpec((tm, IN_TN), lambda i, j: (i, 0)),
            pl.BlockSpec((tm, IN_TN), lambda i, j: (i, jnp.maximum(j - 1, 0))),
        ],
        out_shape=[
            jax.ShapeDtypeStruct((rows, S5_WIDTH), F32),
            jax.ShapeDtypeStruct((rows, EVEN_IN - S5_WIDTH), BF16),
        ],
        scratch_shapes=[pltpu.VMEM((tm, D_MODEL), BF16)],
        compiler_params=_params(("parallel", "arbitrary")),
        name="even_inproj",
    )(x2d, ln, w, cos, sin)


def _s5_disc_kernel(lr_ref, li_ref, ldt_ref, br_ref, bi_ref, are_ref, aim_ref, bbr_ref, bbi_ref):
    lr = lr_ref[...]
    li = li_ref[...]
    dt = jnp.exp(ldt_ref[...])
    mag = jnp.exp(lr * dt)
    ab_re = mag * jnp.cos(li * dt)
    ab_im = mag * jnp.sin(li * dt)
    den = lr * lr + li * li
    nr = ab_re - 1.0
    ni = ab_im
    f_re = (nr * lr + ni * li) / den
    f_im = (ni * lr - nr * li) / den
    are_ref[...] = ab_re
    aim_ref[...] = ab_im
    br = br_ref[...]
    bi = bi_ref[...]
    bbr_ref[...] = f_re[:, None, :] * br - f_im[:, None, :] * bi
    bbi_ref[...] = f_re[:, None, :] * bi + f_im[:, None, :] * br


def _s5_discretise(lam_re, lam_im, log_dt, b_re, b_im):
    g, p, hh = S5_GROUPS, S5_STATE, S5_GROUP
    brt = jnp.swapaxes(b_re, 1, 2)
    bit = jnp.swapaxes(b_im, 1, 2)
    return pl.pallas_call(
        _s5_disc_kernel,
        out_shape=[
            jax.ShapeDtypeStruct((g, p), F32),
            jax.ShapeDtypeStruct((g, p), F32),
            jax.ShapeDtypeStruct((g, hh, p), F32),
            jax.ShapeDtypeStruct((g, hh, p), F32),
        ],
        name="s5_discretise",
    )(lam_re, lam_im, log_dt.reshape(g, 1), brt, bit)


def _s5_pack(ab_re, ab_im, bbr, bbi, c_re, c_im):
    eye = jnp.eye(S5_GB, dtype=F32)
    nb, gb, p, hh = S5_NBLK, S5_GB, S5_STATE, S5_GROUP

    def pack_b(bb):
        return jnp.einsum('gy,aghp->aghyp', eye, bb.reshape(nb, gb, hh, p)).reshape(nb, gb * hh, gb * p)

    def pack_c(cc):
        return jnp.einsum('gy,aghp->agpyh', eye, cc.reshape(nb, gb, hh, p)).reshape(nb, gb * p, gb * hh)

    bblk = jnp.concatenate([pack_b(bbr), pack_b(bbi)], axis=-1).astype(BF16)
    cblk = jnp.concatenate([pack_c(c_re), pack_c(-c_im)], axis=1).astype(BF16)
    return ab_re.reshape(1, S5_NSTATE), ab_im.reshape(1, S5_NSTATE), bblk, cblk


S5_SLAB = 256


def _s5_kernel(u_hbm, bblk_ref, cblk_ref, are_ref, aim_ref, d_ref, gw_ref, gb_ref, y_hbm,
               ubuf, ybuf, in_sem, out_sem, bur_ref, bui_ref, xr_ref, xi_ref, sr_ref, si_ref, *, nb, tc):
    t = pl.program_id(0)
    nt = pl.num_programs(0)
    slot = t % 2
    half = S5_GB * S5_STATE
    wu = S5_GB * S5_GROUP

    def in_copy(step, sl, b):
        return pltpu.make_async_copy(u_hbm.at[b, pl.ds(step * tc, tc), :], ubuf.at[sl, :, b, :], in_sem.at[sl, b])

    def out_copy(step, sl, b):
        return pltpu.make_async_copy(ybuf.at[sl, :, b, :], y_hbm.at[b, pl.ds(step * tc, tc), :], out_sem.at[sl, b])

    @pl.when(t == 0)
    def _():
        for b in range(nb):
            in_copy(0, 0, b).start()
        sr_ref[...] = jnp.zeros_like(sr_ref)
        si_ref[...] = jnp.zeros_like(si_ref)

    @pl.when(t + 1 < nt)
    def _():
        for b in range(nb):
            in_copy(t + 1, 1 - slot, b).start()

    for b in range(nb):
        in_copy(t, slot, b).wait()

    u = ubuf[slot].reshape(tc * nb, S5_WIDTH)
    ub = u.astype(BF16)
    for a in range(S5_NBLK):
        bu = jnp.dot(ub[:, a * wu:(a + 1) * wu], bblk_ref[a], preferred_element_type=F32)
        bur_ref[:, a * half:(a + 1) * half] = bu[:, :half]
        bui_ref[:, a * half:(a + 1) * half] = bu[:, half:]

    for sl in range(S5_NSTATE // S5_SLAB):
        cs = slice(sl * S5_SLAB, (sl + 1) * S5_SLAB)
        ar = jnp.broadcast_to(are_ref[:, cs], (nb, S5_SLAB))
        ai = jnp.broadcast_to(aim_ref[:, cs], (nb, S5_SLAB))

        def step(k, carry, cs=cs, ar=ar, ai=ai):
            xr, xi = carry
            r0 = pl.multiple_of(k * nb, nb)
            nr = ar * xr - ai * xi + bur_ref[pl.ds(r0, nb), cs]
            ni = ar * xi + ai * xr + bui_ref[pl.ds(r0, nb), cs]
            xr_ref[pl.ds(r0, nb), cs] = nr.astype(BF16)
            xi_ref[pl.ds(r0, nb), cs] = ni.astype(BF16)
            return nr, ni

        xr, xi = lax.fori_loop(0, tc, step, (sr_ref[:, cs], si_ref[:, cs]), unroll=4)
        sr_ref[:, cs] = xr
        si_ref[:, cs] = xi

    ys = []
    for a in range(S5_NBLK):
        ya = jnp.dot(xr_ref[:, a * half:(a + 1) * half], cblk_ref[a, :half, :], preferred_element_type=F32)
        ya = ya + jnp.dot(xi_ref[:, a * half:(a + 1) * half], cblk_ref[a, half:, :], preferred_element_type=F32)
        ys.append(ya)
    y = jnp.concatenate(ys, axis=-1) + d_ref[...] * u
    y = jax.nn.gelu(y)
    z = jnp.dot(y.astype(BF16), gw_ref[...], preferred_element_type=F32) + gb_ref[...]
    y = y * jax.nn.sigmoid(z)

    @pl.when(t >= 2)
    def _():
        for b in range(nb):
            out_copy(t - 2, slot, b).wait()

    ybuf[slot] = y.reshape(tc, nb, S5_WIDTH)
    for b in range(nb):
        out_copy(t, slot, b).start()

    @pl.when(t == nt - 1)
    def _():
        for b in range(nb):
            out_copy(t, slot, b).wait()

        @pl.when(nt >= 2)
        def _():
            for b in range(nb):
                out_copy(t - 1, 1 - slot, b).wait()


def _s5(u3, bblk, cblk, a_re, a_im, d_skip, glu_w, glu_b, tc):
    nb, seq, _ = u3.shape
    rows = nb * tc
    return pl.pallas_call(
        functools.partial(_s5_kernel, nb=nb, tc=tc),
        grid=(seq // tc,),
        in_specs=[
            pl.BlockSpec(memory_space=pl.ANY),
            _const_spec(bblk.shape),
            _const_spec(cblk.shape),
            _const_spec((1, S5_NSTATE)),
            _const_spec((1, S5_NSTATE)),
            _const_spec((1, S5_WIDTH)),
            _const_spec((S5_WIDTH, S5_WIDTH)),
            _const_spec((1, S5_WIDTH)),
        ],
        out_specs=pl.BlockSpec(memory_space=pl.ANY),
        out_shape=jax.ShapeDtypeStruct((nb, seq, S5_WIDTH), F32),
        scratch_shapes=[
            pltpu.VMEM((2, tc, nb, S5_WIDTH), F32),
            pltpu.VMEM((2, tc, nb, S5_WIDTH), F32),
            pltpu.SemaphoreType.DMA((2, nb)),
            pltpu.SemaphoreType.DMA((2, nb)),
            pltpu.VMEM((rows, S5_NSTATE), F32),
            pltpu.VMEM((rows, S5_NSTATE), F32),
            pltpu.VMEM((rows, S5_NSTATE), BF16),
            pltpu.VMEM((rows, S5_NSTATE), BF16),
            pltpu.VMEM((nb, S5_NSTATE), F32),
            pltpu.VMEM((nb, S5_NSTATE), F32),
        ],
        compiler_params=_params(("arbitrary",)),
        name="s5_scan",
    )(u3, bblk, cblk, a_re, a_im, d_skip, glu_w, glu_b)


def _retention_consts():
    hs = np.arange(RET_HEADS, dtype=np.float32)
    gamma_log = np.log((1.0 - np.exp2(-5.0 - hs)).astype(np.float32)).astype(np.float32)
    c = RET_CHUNK
    idx = np.arange(c, dtype=np.float32)
    diff = idx[:, None] - idx[None, :]
    dmask = np.where(diff[None] >= 0, np.exp(np.maximum(diff[None], 0.0) * gamma_log[:, None, None]), 0.0)
    xi = np.exp((idx[None, :] + 1.0) * gamma_log[:, None])
    zeta = np.exp((c - 1.0 - idx[None, :]) * gamma_log[:, None])
    g_chunk = np.exp(c * gamma_log)
    xi_b = np.broadcast_to(xi[:, :, None], (RET_HEADS, c, RET_HEAD_DIM))
    zeta_b = np.broadcast_to(zeta[:, :, None], (RET_HEADS, c, RET_HEAD_DIM))
    return (dmask.astype(np.float32), np.ascontiguousarray(xi_b, np.float32),
            np.ascontiguousarray(zeta_b, np.float32), [float(v) for v in g_chunk.astype(np.float32)])


def _retention_kernel(q_ref, k_ref, v_ref, sg_ref, dm_ref, xi_ref, zeta_ref, rn_ref, y_ref, r_ref, *, g_chunk):
    c = pl.program_id(1)

    @pl.when(c == 0)
    def _():
        r_ref[...] = jnp.zeros_like(r_ref)

    for h in range(RET_HEADS):
        hs = slice(h * RET_HEAD_DIM, (h + 1) * RET_HEAD_DIM)
        q = q_ref[:, hs]
        k = k_ref[:, hs]
        v = v_ref[:, hs]
        r = r_ref[h]
        s = lax.dot_general(q, k, (((1,), (1,)), ((), ())), preferred_element_type=F32)
        inner = (s * dm_ref[h]).astype(BF16)
        o = jnp.dot(inner, v, preferred_element_type=F32)
        o = o + xi_ref[h] * jnp.dot(q, r.astype(BF16), preferred_element_type=F32)
        kz = (k.astype(F32) * zeta_ref[h]).astype(BF16)
        r_ref[h] = r * g_chunk[h] + lax.dot_general(kz, v, (((0,), (0,)), ((), ())), preferred_element_type=F32)
        o = o * lax.rsqrt(jnp.mean(o * o, axis=-1, keepdims=True) + EPS)
        o = o * rn_ref[:, hs]
        y_ref[:, hs] = (sg_ref[:, hs].astype(F32) * o).astype(BF16)


def _retention(proj, ret_norm, nb, seq):
    dmask, xi_b, zeta_b, g_chunk = _retention_consts()
    nc = seq // RET_CHUNK
    blk = (RET_CHUNK, RET_WIDTH)
    return pl.pallas_call(
        functools.partial(_retention_kernel, g_chunk=g_chunk),
        grid=(nb, nc),
        in_specs=[
            pl.BlockSpec(blk, lambda b, c: (b * nc + c, 0)),
            pl.BlockSpec(blk, lambda b, c: (b * nc + c, 1)),
            pl.BlockSpec(blk, lambda b, c: (b * nc + c, 2)),
            pl.BlockSpec(blk, lambda b, c: (b * nc + c, 3)),
            _const_spec(dmask.shape),
            _const_spec(xi_b.shape),
            _const_spec(zeta_b.shape),
            _const_spec((1, RET_WIDTH)),
        ],
        out_specs=pl.BlockSpec(blk, lambda b, c: (b * nc + c, 0)),
        out_shape=jax.ShapeDtypeStruct((nb * seq, RET_WIDTH), BF16),
        scratch_shapes=[pltpu.VMEM((RET_HEADS, RET_HEAD_DIM, RET_HEAD_DIM), F32)],
        compiler_params=_params(("parallel", "arbitrary")),
        name="retention",
    )(proj, proj, proj, proj, jnp.asarray(dmask), jnp.asarray(xi_b), jnp.asarray(zeta_b), ret_norm)


def _outproj_kernel(*refs, n_in):
    res_ref = refs[0]
    ys = refs[1:1 + n_in]
    ws = refs[1 + n_in:1 + 2 * n_in]
    o_ref = refs[1 + 2 * n_in]
    acc = res_ref[...]
    for y, w in zip(ys, ws):
        acc = acc + jnp.dot(y[...].astype(BF16), w[...], preferred_element_type=F32)
    o_ref[...] = acc


def _outproj(res, ys, ws, tm, name):
    rows, d = res.shape
    n_in = len(ys)
    row = lambda i: (i, 0)
    return pl.pallas_call(
        functools.partial(_outproj_kernel, n_in=n_in),
        grid=(rows // tm,),
        in_specs=([pl.BlockSpec((tm, d), row)] + [pl.BlockSpec((tm, y.shape[1]), row) for y in ys]
                  + [_const_spec(w.shape) for w in ws]),
        out_specs=pl.BlockSpec((tm, d), row),
        out_shape=jax.ShapeDtypeStruct((rows, d), F32),
        compiler_params=_params(("parallel",)),
        name=name,
    )(res, *ys, *ws)


def _ffn_kernel(h_ref, halo_ref, ln_ref, wg_ref, wv_ref, cwg_ref, cwv_ref, cbg_ref, cbv_ref, wd_ref, lnf_ref,
                o_ref, hn_ref, *, nt, tm, final_norm):
    i = pl.program_id(0)
    j = pl.program_id(1)
    hl = BF16_ROWS

    @pl.when(j == 0)
    def _():
        hn_ref[hl:, :] = _rms(h_ref[...], ln_ref[...]).astype(BF16)
        halo = _rms(halo_ref[...], ln_ref[...])
        hn_ref[:hl, :] = jnp.where(i % nt > 0, halo, 0.0).astype(BF16)
        o_ref[...] = h_ref[...]

    ts = tm // FFN_SUBTILES
    for sub in range(FFN_SUBTILES):
        r0 = sub * ts
        hn = hn_ref[r0:r0 + hl + ts, :]

        def conv(w_ref, cw_ref, cb_ref, hn=hn):
            a = jnp.dot(hn, w_ref[...], preferred_element_type=F32)
            cw = cw_ref[...]
            return (a[hl - 2:hl - 2 + ts] * cw[0:1] + a[hl - 1:hl - 1 + ts] * cw[1:2] + a[hl:] * cw[2:3]
                    + cb_ref[...])

        gate = conv(wg_ref, cwg_ref, cbg_ref)
        val = conv(wv_ref, cwv_ref, cbv_ref)
        g = (gate * jax.nn.sigmoid(gate) * val).astype(BF16)
        o_ref[r0:r0 + ts, :] += jnp.dot(g, wd_ref[...], preferred_element_type=F32)

    if final_norm:
        @pl.when(j == pl.num_programs(1) - 1)
        def _():
            o_ref[...] = _rms(o_ref[...], lnf_ref[...])


def _ffn(h, ln, w_up, conv_w, conv_b, w_down, ln_final, seq, tm, tf, final_norm):
    rows = h.shape[0]
    hl = BF16_ROWS
    nj = D_FF // tf
    nt = seq // tm
    return pl.pallas_call(
        functools.partial(_ffn_kernel, nt=nt, tm=tm, final_norm=final_norm),
        grid=(rows // tm, nj),
        in_specs=[
            pl.BlockSpec((tm, D_MODEL), lambda i, j: (i, 0), pipeline_mode=pl.Buffered(1)),
            pl.BlockSpec((hl, D_MODEL), lambda i, j: (jnp.maximum(i * (tm // hl) - 1, 0), 0)),
            pl.BlockSpec((1, D_MODEL), lambda i, j: (0, 0)),
            pl.BlockSpec((None, D_MODEL, tf), lambda i, j: (j, 0, 0)),
            pl.BlockSpec((None, D_MODEL, tf), lambda i, j: (nj + j, 0, 0)),
            pl.BlockSpec((3, tf), lambda i, j: (0, j)),
            pl.BlockSpec((3, tf), lambda i, j: (0, nj + j)),
            pl.BlockSpec((1, tf), lambda i, j: (0, j)),
            pl.BlockSpec((1, tf), lambda i, j: (0, nj + j)),
            pl.BlockSpec((tf, D_MODEL), lambda i, j: (j, 0)),
            pl.BlockSpec((1, D_MODEL), lambda i, j: (0, 0)),
        ],
        out_specs=pl.BlockSpec((tm, D_MODEL), lambda i, j: (i, 0)),
        out_shape=jax.ShapeDtypeStruct((rows, D_MODEL), F32),
        scratch_shapes=[pltpu.VMEM((hl + tm, D_MODEL), BF16)],
        compiler_params=_params(("parallel", "arbitrary")),
        name="conv_ffn",
    )(h, h, ln, w_up, w_up, conv_w, conv_w, conv_b, conv_b, w_down, ln_final)


MLA_DOWN_PAD = MLA_Q_RANK + MLA_KV_RANK + 2 * LANE
MLA_QN = MLA_HEADS * MLA_NOPE
MLA_QR = MLA_HEADS * MLA_ROPE
MLA_KV = MLA_HEADS * (MLA_NOPE + MLA_V)
MLA_CHUNK = 1024
LOG2E = 1.4426950408889634


def _mla_proj_kernel(h_ref, ln_ref, wd_ref, qn_ref, kvn_ref, wq_ref, wkv_ref, cos_ref, sin_ref,
                     oqn_ref, oqr_ref, okv_ref, okr_ref):
    hn = _rms(h_ref[...], ln_ref[...]).astype(BF16)
    c = jnp.dot(hn, wd_ref[...], preferred_element_type=F32)
    cos = cos_ref[...]
    sin = sin_ref[...]
    scale = (MLA_NOPE + MLA_ROPE) ** -0.5 * LOG2E

    cq = _rms(c[:, :MLA_Q_RANK], qn_ref[...]).astype(BF16)
    for n0 in range(0, MLA_QN, MLA_CHUNK):
        qn = jnp.dot(cq, wq_ref[:, n0:n0 + MLA_CHUNK], preferred_element_type=F32)
        oqn_ref[:, n0:n0 + MLA_CHUNK] = (qn * scale).astype(BF16)
    cs = cos * scale
    ss = sin * scale
    for n0 in range(0, MLA_QR, MLA_CHUNK):
        qr = jnp.dot(cq, wq_ref[:, MLA_QN + n0:MLA_QN + n0 + MLA_CHUNK], preferred_element_type=F32)
        qp = jnp.dot(cq, wq_ref[:, MLA_QN + MLA_QR + n0:MLA_QN + MLA_QR + n0 + MLA_CHUNK],
                     preferred_element_type=F32)
        for m in range(MLA_CHUNK // LANE):
            ls = slice(m * LANE, (m + 1) * LANE)
            oqr_ref[:, n0 + m * LANE:n0 + (m + 1) * LANE] = (qr[:, ls] * cs + qp[:, ls] * ss).astype(BF16)

    ckv = _rms(c[:, MLA_Q_RANK:MLA_Q_RANK + MLA_KV_RANK], kvn_ref[...]).astype(BF16)
    for n0 in range(0, MLA_KV, MLA_CHUNK):
        okv_ref[:, n0:n0 + MLA_CHUNK] = jnp.dot(
            ckv, wkv_ref[:, n0:n0 + MLA_CHUNK], preferred_element_type=F32).astype(BF16)

    o = MLA_Q_RANK + MLA_KV_RANK
    kr2 = c[:, o:o + LANE] * cos + c[:, o + LANE:o + 2 * LANE] * sin
    left = lax.broadcasted_iota(jnp.int32, kr2.shape, 1) < MLA_ROPE
    okr_ref[:, :LANE] = jnp.where(left, kr2, 0.0).astype(BF16)
    okr_ref[:, LANE:] = jnp.where(left, 0.0, kr2).astype(BF16)


def _mla_proj(h, ln, wd, q_norm, kv_norm, wq, wkv, cos4, sin4, seq, tm):
    rows = h.shape[0]
    nt = seq // tm
    row = lambda i: (i, 0)
    return pl.pallas_call(
        _mla_proj_kernel,
        grid=(rows // tm,),
        in_specs=[
            pl.BlockSpec((tm, D_MODEL), row),
            _const_spec((1, D_MODEL)),
            _const_spec(wd.shape),
            _const_spec((1, MLA_Q_RANK)),
            _const_spec((1, MLA_KV_RANK)),
            _const_spec(wq.shape),
            _const_spec(wkv.shape),
            pl.BlockSpec((tm, LANE), lambda i: (i % nt, 0)),
            pl.BlockSpec((tm, LANE), lambda i: (i % nt, 0)),
        ],
        out_specs=[
            pl.BlockSpec((tm, MLA_QN), row),
            pl.BlockSpec((tm, MLA_QR), row),
            pl.BlockSpec((tm, MLA_KV), row),
            pl.BlockSpec((tm, 2 * LANE), row),
        ],
        out_shape=[
            jax.ShapeDtypeStruct((rows, MLA_QN), BF16),
            jax.ShapeDtypeStruct((rows, MLA_QR), BF16),
            jax.ShapeDtypeStruct((rows, MLA_KV), BF16),
            jax.ShapeDtypeStruct((rows, 2 * LANE), BF16),
        ],
        compiler_params=_params(("parallel",)),
        name="mla_proj",
    )(h, ln, wd, q_norm, kv_norm, wq, wkv, cos4, sin4)


def _attn_kernel(qn_ref, qr_ref, kv_ref, kr_ref, o_ref, *, tq, nq):
    qi = pl.program_id(2)
    qr = qr_ref[...]
    qs = [jnp.concatenate([qn_ref[:, s * LANE:(s + 1) * LANE], qr], axis=-1) for s in range(2)]

    def keys(s, r0, n):
        k = jnp.concatenate([kv_ref[pl.ds(r0, n), 2 * s * LANE:(2 * s + 1) * LANE],
                             kr_ref[pl.ds(r0, n), s * LANE:(s + 1) * LANE]], axis=-1)
        return k, kv_ref[pl.ds(r0, n), (2 * s + 1) * LANE:(2 * s + 2) * LANE]

    def update(q, kv, carry, mask=None):
        k, v = kv
        m, l, acc = carry
        s = lax.dot_general(q, k, (((1,), (1,)), ((), ())), preferred_element_type=F32)
        if mask is not None:
            s = jnp.where(mask, s, -1e30)
        m_new = jnp.maximum(m, jnp.max(s, axis=-1, keepdims=True))
        alpha = jnp.exp2(m - m_new)
        p = jnp.exp2(s - m_new)
        l = alpha * l + jnp.sum(p, axis=-1, keepdims=True)
        acc = alpha * acc + jnp.dot(p.astype(BF16), v, preferred_element_type=F32)
        return m_new, l, acc

    row = lax.broadcasted_iota(jnp.int32, (tq, tq), 0)
    col = lax.broadcasted_iota(jnp.int32, (tq, tq), 1)

    for c in range(nq):
        @pl.when(qi == c)
        def _(c=c):
            carries = [(jnp.full((tq, 1), -1e30, F32), jnp.zeros((tq, 1), F32), jnp.zeros((tq, MLA_V), F32))
                       for _ in range(2)]
            for kb in range(c):
                carries = [update(qs[s], keys(s, kb * tq, tq), carries[s]) for s in range(2)]
            for s in range(2):
                _, l, acc = update(qs[s], keys(s, c * tq, tq), carries[s], row >= col)
                o_ref[:, s * LANE:(s + 1) * LANE] = (acc / l).astype(BF16)


def _attention(qn, qr, kv, krp, nb, seq, tq):
    npair = MLA_HEADS // 2
    nq = seq // tq
    return pl.pallas_call(
        functools.partial(_attn_kernel, tq=tq, nq=nq),
        grid=(nb, npair, nq),
        in_specs=[
            pl.BlockSpec((tq, 2 * LANE), lambda b, h, i: (b * nq + i, h)),
            pl.BlockSpec((tq, LANE), lambda b, h, i: (b * nq + i, h)),
            pl.BlockSpec((seq, 4 * LANE), lambda b, h, i: (b, h)),
            pl.BlockSpec((seq, 2 * LANE), lambda b, h, i: (b, 0)),
        ],
        out_specs=pl.BlockSpec((tq, 2 * LANE), lambda b, h, i: (b * nq + i, h)),
        out_shape=jax.ShapeDtypeStruct((nb * seq, MLA_HEADS * MLA_V), BF16),
        compiler_params=_params(("parallel", "parallel", "arbitrary")),
        name="mla_attention",
    )(qn, qr, kv, krp)


def _rope_tables(n, dim):
    inv = ROPE_BASE ** (-jnp.arange(0, dim, 2, dtype=F32) / dim)
    ang = jnp.arange(n, dtype=F32)[:, None] * inv[None, :]
    return jnp.cos(ang), jnp.sin(ang)


def _col_tiles(w, tn):
    k, n = w.shape
    return jnp.swapaxes(w.reshape(k, n // tn, tn), 0, 1).astype(BF16)


def _rot_cols(w):
    lead = w.shape[:-1]
    w = w.reshape(lead + (-1, 2, MLA_ROPE // 2))
    return jnp.stack([-w[..., 1, :], w[..., 0, :]], axis=-2).reshape(lead + (-1,))


def kernel(x, ln_mix_even, w_in_even, s5_lambda_re, s5_lambda_im, s5_log_dt, s5_b_re, s5_b_im, s5_c_re, s5_c_im, s5_d, s5_glu_w, s5_glu_b, ret_norm, w_out_even, ln_mix_odd, mla_w_down, mla_q_norm, mla_w_uq, mla_kv_norm, mla_w_ukv, w_out_odd, ln_ffn, ffn_w_up, ffn_conv_w, ffn_conv_b, ffn_w_down, ln_final):
    nb, seq, d = x.shape
    assert d == D_MODEL and nb % 8 == 0 and seq % 128 == 0
    rows = nb * seq
    tm = min(seq, 1024)
    tmo = min(seq, 512)
    tf = 512
    row2 = lambda a: a.reshape(1, -1)

    x2d = x.reshape(rows, d)

    ret_cos, ret_sin = _rope_tables(seq, RET_HEAD_DIM)
    u, proj = _inproj(x2d, row2(ln_mix_even[0]), _col_tiles(w_in_even[0], IN_TN), ret_cos, ret_sin, seq, tm)

    ab_re, ab_im, bbr, bbi = _s5_discretise(s5_lambda_re[0], s5_lambda_im[0], s5_log_dt[0], s5_b_re[0], s5_b_im[0])
    a_re, a_im, bblk, cblk = _s5_pack(ab_re, ab_im, bbr, bbi, s5_c_re[0], s5_c_im[0])
    y_s5 = _s5(u.reshape(nb, seq, S5_WIDTH), bblk, cblk, a_re, a_im, row2(s5_d[0]),
               s5_glu_w[0].astype(BF16), row2(s5_glu_b[0]), tc=min(seq, 32))
    y_ret = _retention(proj, row2(ret_norm[0]), nb, seq)

    wo = w_out_even[0].astype(BF16)
    h = _outproj(x2d, [y_s5.reshape(rows, S5_WIDTH), y_ret], [wo[:S5_WIDTH], wo[S5_WIDTH:]], tmo, "even_outproj")

    def ffn(h, layer, final_norm):
        return _ffn(h, row2(ln_ffn[layer]), _col_tiles(ffn_w_up[layer], tf), ffn_conv_w[layer],
                    row2(ffn_conv_b[layer]), ffn_w_down[layer].astype(BF16), row2(ln_final),
                    seq, tm, tf, final_norm)

    h = ffn(h, 0, False)

    mla_cos, mla_sin = _rope_tables(seq, MLA_ROPE)
    cos4 = jnp.tile(mla_cos, (1, 4))
    sin4 = jnp.tile(mla_sin, (1, 4))
    wdn = mla_w_down[0]
    w_kr = wdn[:, MLA_Q_RANK + MLA_KV_RANK:]
    w_krp = _rot_cols(w_kr)
    wd = jnp.concatenate([wdn[:, :MLA_Q_RANK + MLA_KV_RANK], w_kr, w_kr, w_krp, w_krp], axis=1).astype(BF16)
    wq3 = mla_w_uq[0].reshape(MLA_Q_RANK, MLA_HEADS, MLA_NOPE + MLA_ROPE)
    wq_n = wq3[:, :, :MLA_NOPE].reshape(MLA_Q_RANK, MLA_QN)
    wq_r = wq3[:, :, MLA_NOPE:].reshape(MLA_Q_RANK, MLA_QR)
    wq = jnp.concatenate([wq_n, wq_r, _rot_cols(wq_r)], axis=1).astype(BF16)
    qn, qr, kv, krp = _mla_proj(h, row2(ln_mix_odd[0]), wd, row2(mla_q_norm[0]), row2(mla_kv_norm[0]),
                                wq, mla_w_ukv[0].astype(BF16), cos4, sin4, seq, tmo)
    o = _attention(qn, qr, kv, krp, nb, seq, tq=min(seq, 512))
    h = _outproj(h, [o], [w_out_odd[0].astype(BF16)], tmo, "odd_outproj")
    h = ffn(h, 1, True)
    return h.reshape(nb, seq, d)
```

```python
import functools

import numpy as np
import jax
import jax.numpy as jnp
from jax import lax
from jax.experimental import pallas as pl
from jax.experimental.pallas import tpu as pltpu

F32 = jnp.float32
BF16 = jnp.bfloat16

EPS = 1e-6
ROPE_BASE = 10000.0
D_MODEL = 2048
S5_WIDTH = 512
S5_GROUP = 16
S5_GROUPS = 32
S5_STATE = 64
S5_GB = 8
S5_NBLK = S5_GROUPS // S5_GB
S5_NSTATE = S5_GROUPS * S5_STATE
RET_HEADS = 6
RET_HEAD_DIM = 256
RET_WIDTH = RET_HEADS * RET_HEAD_DIM
RET_CHUNK = 128
EVEN_IN = S5_WIDTH + 4 * RET_WIDTH
MLA_HEADS = 16
MLA_NOPE = 128
MLA_ROPE = 64
MLA_V = 128
MLA_Q_RANK = 512
MLA_KV_RANK = 512
D_FF = 5632

LANE = 128
BF16_ROWS = 16
VMEM_LIMIT = 60 * 1024 * 1024


def _params(sem, vmem=VMEM_LIMIT, flags=None):
    return pltpu.CompilerParams(dimension_semantics=sem, vmem_limit_bytes=vmem, flags=flags)


def _rms(x, g):
    return x * lax.rsqrt(jnp.mean(x * x, axis=-1, keepdims=True) + EPS) * g


def _const_spec(shape):
    nd = len(shape)
    return pl.BlockSpec(shape, lambda *_: (0,) * nd, pipeline_mode=pl.Buffered(1))


IN_TN = RET_WIDTH
IN_NJ = 4
IN_CHUNK = 512
IN_SUBTILES = 2


def _inproj_kernel(h_ref, ln_ref, wu_ref, w_ref, cos_ref, sin_ref, u_ref, p_ref, hn_ref):
    j = pl.program_id(1)
    ts = hn_ref.shape[0] // IN_SUBTILES

    def pieces(wref):
        for c0 in range(0, wref.shape[1], IN_CHUNK):
            for sub in range(IN_SUBTILES):
                rs = slice(sub * ts, (sub + 1) * ts)
                cs = slice(c0, c0 + IN_CHUNK)
                yield rs, cs, jnp.dot(hn_ref[rs, :], wref[:, cs], preferred_element_type=F32)

    @pl.when(j == 0)
    def _():
        hn_ref[...] = _rms(h_ref[...], ln_ref[...]).astype(BF16)
        for rs, cs, acc in pieces(wu_ref):
            u_ref[rs, cs] = acc

    @pl.when(j <= 1)
    def _():
        sc = jnp.where(j == 1, RET_HEAD_DIM ** -0.5, 1.0).astype(F32)
        for rs, cs, acc in pieces(w_ref):
            c = cos_ref[rs, :] * sc
            s = sin_ref[rs, :] * sc
            for hh in range(IN_CHUNK // RET_HEAD_DIM):
                o = hh * RET_HEAD_DIM
                x1 = acc[:, o:o + LANE]
                x2 = acc[:, o + LANE:o + 2 * LANE]
                p_ref[rs, cs.start + o:cs.start + o + LANE] = (x1 * c - x2 * s).astype(BF16)
                p_ref[rs, cs.start + o + LANE:cs.start + o + 2 * LANE] = (x2 * c + x1 * s).astype(BF16)

    @pl.when(j == 2)
    def _():
        for rs, cs, acc in pieces(w_ref):
            p_ref[rs, cs] = acc.astype(BF16)

    @pl.when(j == 3)
    def _():
        for rs, cs, acc in pieces(w_ref):
            p_ref[rs, cs] = (acc * jax.nn.sigmoid(acc)).astype(BF16)


def _inproj(x2d, ln, wu, w, cos, sin, seq, tm):
    rows = x2d.shape[0]
    nt = seq // tm
    return pl.pallas_call(
        _inproj_kernel,
        grid=(rows // tm, IN_NJ),
        in_specs=[
            pl.BlockSpec((tm, D_MODEL), lambda i, j: (i, 0)),
            pl.BlockSpec((1, D_MODEL), lambda i, j: (0, 0)),
            _const_spec((D_MODEL, S5_WIDTH)),
            pl.BlockSpec((D_MODEL, IN_TN), lambda i, j: (0, j)),
            pl.BlockSpec((tm, LANE), lambda i, j: (i % nt, 0)),
            pl.BlockSpec((tm, LANE), lambda i, j: (i % nt, 0)),
        ],
        out_specs=[
            pl.BlockSpec((tm, S5_WIDTH), lambda i, j: (i, 0)),
            pl.BlockSpec((tm, IN_TN), lambda i, j: (i, j)),
        ],
        out_shape=[
            jax.ShapeDtypeStruct((rows, S5_WIDTH), F32),
            jax.ShapeDtypeStruct((rows, EVEN_IN - S5_WIDTH), BF16),
        ],
        scratch_shapes=[pltpu.VMEM((tm, D_MODEL), BF16)],
        compiler_params=_params(("parallel", "arbitrary")),
        name="even_inproj",
    )(x2d, ln, wu, w, cos, sin)


def _s5_disc_kernel(lr_ref, li_ref, ldt_ref, br_ref, bi_ref, are_ref, aim_ref, bbr_ref, bbi_ref):
    lr = lr_ref[...]
    li = li_ref[...]
    dt = jnp.exp(ldt_ref[...])
    mag = jnp.exp(lr * dt)
    ab_re = mag * jnp.cos(li * dt)
    ab_im = mag * jnp.sin(li * dt)
    den = lr * lr + li * li
    nr = ab_re - 1.0
    ni = ab_im
    f_re = (nr * lr + ni * li) / den
    f_im = (ni * lr - nr * li) / den
    are_ref[...] = ab_re
    aim_ref[...] = ab_im
    br = br_ref[...]
    bi = bi_ref[...]
    bbr_ref[...] = f_re[:, None, :] * br - f_im[:, None, :] * bi
    bbi_ref[...] = f_re[:, None, :] * bi + f_im[:, None, :] * br


def _s5_discretise(lam_re, lam_im, log_dt, b_re, b_im):
    g, p, hh = S5_GROUPS, S5_STATE, S5_GROUP
    brt = jnp.swapaxes(b_re, 1, 2)
    bit = jnp.swapaxes(b_im, 1, 2)
    return pl.pallas_call(
        _s5_disc_kernel,
        out_shape=[
            jax.ShapeDtypeStruct((g, p), F32),
            jax.ShapeDtypeStruct((g, p), F32),
            jax.ShapeDtypeStruct((g, hh, p), F32),
            jax.ShapeDtypeStruct((g, hh, p), F32),
        ],
        name="s5_discretise",
    )(lam_re, lam_im, log_dt.reshape(g, 1), brt, bit)


def _s5_pack(ab_re, ab_im, bbr, bbi, c_re, c_im):
    eye = jnp.eye(S5_GB, dtype=F32)
    nb, gb, p, hh = S5_NBLK, S5_GB, S5_STATE, S5_GROUP

    def pack_b(bb):
        return jnp.einsum('gy,aghp->aghyp', eye, bb.reshape(nb, gb, hh, p)).reshape(nb, gb * hh, gb * p)

    def pack_c(cc):
        return jnp.einsum('gy,aghp->agpyh', eye, cc.reshape(nb, gb, hh, p)).reshape(nb, gb * p, gb * hh)

    bblk = jnp.concatenate([pack_b(bbr), pack_b(bbi)], axis=-1).astype(BF16)
    cblk = jnp.concatenate([pack_c(c_re), pack_c(-c_im)], axis=1).astype(BF16)
    return ab_re.reshape(1, S5_NSTATE), ab_im.reshape(1, S5_NSTATE), bblk, cblk


S5_SLAB = 256


def _s5_kernel(u_hbm, bblk_ref, cblk_ref, are_ref, aim_ref, d_ref, gw_ref, gb_ref, y_hbm,
               ubuf, ybuf, in_sem, out_sem, bur_ref, bui_ref, xr_ref, xi_ref, sr_ref, si_ref, *, nb, tc):
    t = pl.program_id(0)
    nt = pl.num_programs(0)
    slot = t % 2
    half = S5_GB * S5_STATE
    wu = S5_GB * S5_GROUP

    def in_copy(step, sl, b):
        return pltpu.make_async_copy(u_hbm.at[b, pl.ds(step * tc, tc), :], ubuf.at[sl, :, b, :], in_sem.at[sl, b])

    def out_copy(step, sl, b):
        return pltpu.make_async_copy(ybuf.at[sl, :, b, :], y_hbm.at[b, pl.ds(step * tc, tc), :], out_sem.at[sl, b])

    @pl.when(t == 0)
    def _():
        for b in range(nb):
            in_copy(0, 0, b).start()
        sr_ref[...] = jnp.zeros_like(sr_ref)
        si_ref[...] = jnp.zeros_like(si_ref)

    @pl.when(t + 1 < nt)
    def _():
        for b in range(nb):
            in_copy(t + 1, 1 - slot, b).start()

    for b in range(nb):
        in_copy(t, slot, b).wait()

    u = ubuf[slot].reshape(tc * nb, S5_WIDTH)
    ub = u.astype(BF16)
    for a in range(S5_NBLK):
        bu = jnp.dot(ub[:, a * wu:(a + 1) * wu], bblk_ref[a], preferred_element_type=F32)
        bur_ref[:, a * half:(a + 1) * half] = bu[:, :half]
        bui_ref[:, a * half:(a + 1) * half] = bu[:, half:]

    for sl in range(S5_NSTATE // S5_SLAB):
        cs = slice(sl * S5_SLAB, (sl + 1) * S5_SLAB)
        ar = jnp.broadcast_to(are_ref[:, cs], (nb, S5_SLAB))
        ai = jnp.broadcast_to(aim_ref[:, cs], (nb, S5_SLAB))

        def step(k, carry, cs=cs, ar=ar, ai=ai):
            xr, xi = carry
            r0 = pl.multiple_of(k * nb, nb)
            nr = ar * xr - ai * xi + bur_ref[pl.ds(r0, nb), cs]
            ni = ar * xi + ai * xr + bui_ref[pl.ds(r0, nb), cs]
            xr_ref[pl.ds(r0, nb), cs] = nr.astype(BF16)
            xi_ref[pl.ds(r0, nb), cs] = ni.astype(BF16)
            return nr, ni

        xr, xi = lax.fori_loop(0, tc, step, (sr_ref[:, cs], si_ref[:, cs]), unroll=4)
        sr_ref[:, cs] = xr
        si_ref[:, cs] = xi

    ys = []
    for a in range(S5_NBLK):
        ya = jnp.dot(xr_ref[:, a * half:(a + 1) * half], cblk_ref[a, :half, :], preferred_element_type=F32)
        ya = ya + jnp.dot(xi_ref[:, a * half:(a + 1) * half], cblk_ref[a, half:, :], preferred_element_type=F32)
        ys.append(ya)
    y = jnp.concatenate(ys, axis=-1) + d_ref[...] * u
    y = jax.nn.gelu(y)
    z = jnp.dot(y.astype(BF16), gw_ref[...], preferred_element_type=F32) + gb_ref[...]
    y = y * jax.nn.sigmoid(z)

    @pl.when(t >= 2)
    def _():
        for b in range(nb):
            out_copy(t - 2, slot, b).wait()

    ybuf[slot] = y.reshape(tc, nb, S5_WIDTH)
    for b in range(nb):
        out_copy(t, slot, b).start()

    @pl.when(t == nt - 1)
    def _():
        for b in range(nb):
            out_copy(t, slot, b).wait()

        @pl.when(nt >= 2)
        def _():
            for b in range(nb):
                out_copy(t - 1, 1 - slot, b).wait()


def _s5(u3, bblk, cblk, a_re, a_im, d_skip, glu_w, glu_b, tc):
    nb, seq, _ = u3.shape
    rows = nb * tc
    return pl.pallas_call(
        functools.partial(_s5_kernel, nb=nb, tc=tc),
        grid=(seq // tc,),
        in_specs=[
            pl.BlockSpec(memory_space=pl.ANY),
            _const_spec(bblk.shape),
            _const_spec(cblk.shape),
            _const_spec((1, S5_NSTATE)),
            _const_spec((1, S5_NSTATE)),
            _const_spec((1, S5_WIDTH)),
            _const_spec((S5_WIDTH, S5_WIDTH)),
            _const_spec((1, S5_WIDTH)),
        ],
        out_specs=pl.BlockSpec(memory_space=pl.ANY),
        out_shape=jax.ShapeDtypeStruct((nb, seq, S5_WIDTH), F32),
        scratch_shapes=[
            pltpu.VMEM((2, tc, nb, S5_WIDTH), F32),
            pltpu.VMEM((2, tc, nb, S5_WIDTH), F32),
            pltpu.SemaphoreType.DMA((2, nb)),
            pltpu.SemaphoreType.DMA((2, nb)),
            pltpu.VMEM((rows, S5_NSTATE), F32),
            pltpu.VMEM((rows, S5_NSTATE), F32),
            pltpu.VMEM((rows, S5_NSTATE), BF16),
            pltpu.VMEM((rows, S5_NSTATE), BF16),
            pltpu.VMEM((nb, S5_NSTATE), F32),
            pltpu.VMEM((nb, S5_NSTATE), F32),
        ],
        compiler_params=_params(("arbitrary",)),
        name="s5_scan",
    )(u3, bblk, cblk, a_re, a_im, d_skip, glu_w, glu_b)


def _retention_consts():
    hs = np.arange(RET_HEADS, dtype=np.float32)
    gamma_log = np.log((1.0 - np.exp2(-5.0 - hs)).astype(np.float32)).astype(np.float32)
    c = RET_CHUNK
    idx = np.arange(c, dtype=np.float32)
    diff = idx[:, None] - idx[None, :]
    dmask = np.where(diff[None] >= 0, np.exp(np.maximum(diff[None], 0.0) * gamma_log[:, None, None]), 0.0)
    xi = np.exp((idx[None, :] + 1.0) * gamma_log[:, None])
    zeta = np.exp((c - 1.0 - idx[None, :]) * gamma_log[:, None])
    g_chunk = np.exp(c * gamma_log)
    xi_b = np.broadcast_to(xi[:, :, None], (RET_HEADS, c, RET_HEAD_DIM))
    zeta_b = np.broadcast_to(zeta[:, :, None], (RET_HEADS, c, RET_HEAD_DIM))
    return (dmask.astype(np.float32), np.ascontiguousarray(xi_b, np.float32),
            np.ascontiguousarray(zeta_b, np.float32), [float(v) for v in g_chunk.astype(np.float32)])


def _retention_kernel(q_ref, k_ref, v_ref, sg_ref, dm_ref, xi_ref, zeta_ref, rn_ref, y_ref, r_ref, *, g_chunk):
    c = pl.program_id(1)

    @pl.when(c == 0)
    def _():
        r_ref[...] = jnp.zeros_like(r_ref)

    for h in range(RET_HEADS):
        hs = slice(h * RET_HEAD_DIM, (h + 1) * RET_HEAD_DIM)
        q = q_ref[:, hs]
        k = k_ref[:, hs]
        v = v_ref[:, hs]
        r = r_ref[h]
        s = lax.dot_general(q, k, (((1,), (1,)), ((), ())), preferred_element_type=F32)
        inner = (s * dm_ref[h]).astype(BF16)
        o = jnp.dot(inner, v, preferred_element_type=F32)
        o = o + xi_ref[h] * jnp.dot(q, r.astype(BF16), preferred_element_type=F32)
        kz = (k.astype(F32) * zeta_ref[h]).astype(BF16)
        r_ref[h] = r * g_chunk[h] + lax.dot_general(kz, v, (((0,), (0,)), ((), ())), preferred_element_type=F32)
        o = o * lax.rsqrt(jnp.mean(o * o, axis=-1, keepdims=True) + EPS)
        o = o * rn_ref[:, hs]
        y_ref[:, hs] = (sg_ref[:, hs].astype(F32) * o).astype(BF16)


def _retention(proj, ret_norm, nb, seq):
    dmask, xi_b, zeta_b, g_chunk = _retention_consts()
    nc = seq // RET_CHUNK
    blk = (RET_CHUNK, RET_WIDTH)
    return pl.pallas_call(
        functools.partial(_retention_kernel, g_chunk=g_chunk),
        grid=(nb, nc),
        in_specs=[
            pl.BlockSpec(blk, lambda b, c: (b * nc + c, 0)),
            pl.BlockSpec(blk, lambda b, c: (b * nc + c, 1)),
            pl.BlockSpec(blk, lambda b, c: (b * nc + c, 2)),
            pl.BlockSpec(blk, lambda b, c: (b * nc + c, 3)),
            _const_spec(dmask.shape),
            _const_spec(xi_b.shape),
            _const_spec(zeta_b.shape),
            _const_spec((1, RET_WIDTH)),
        ],
        out_specs=pl.BlockSpec(blk, lambda b, c: (b * nc + c, 0)),
        out_shape=jax.ShapeDtypeStruct((nb * seq, RET_WIDTH), BF16),
        scratch_shapes=[pltpu.VMEM((RET_HEADS, RET_HEAD_DIM, RET_HEAD_DIM), F32)],
        compiler_params=_params(("parallel", "arbitrary")),
        name="retention",
    )(proj, proj, proj, proj, jnp.asarray(dmask), jnp.asarray(xi_b), jnp.asarray(zeta_b), ret_norm)


def _outproj_kernel(*refs, n_in):
    res_ref = refs[0]
    ys = refs[1:1 + n_in]
    w_ref = refs[1 + n_in]
    o_ref = refs[2 + n_in]
    acc = res_ref[...]
    k0 = 0
    for y in ys:
        kk = y.shape[1]
        acc = acc + jnp.dot(y[...].astype(BF16), w_ref[k0:k0 + kk, :], preferred_element_type=F32)
        k0 += kk
    o_ref[...] = acc


def _outproj(res, ys, w, tm, name):
    rows, d = res.shape
    row = lambda i: (i, 0)
    return pl.pallas_call(
        functools.partial(_outproj_kernel, n_in=len(ys)),
        grid=(rows // tm,),
        in_specs=([pl.BlockSpec((tm, d), row)] + [pl.BlockSpec((tm, y.shape[1]), row) for y in ys]
                  + [_const_spec(w.shape)]),
        out_specs=pl.BlockSpec((tm, d), row),
        out_shape=jax.ShapeDtypeStruct((rows, d), F32),
        compiler_params=_params(("parallel",)),
        name=name,
    )(res, *ys, w)


def _ffn_kernel(h_ref, halo_ref, ln_ref, wg_ref, wv_ref, cwg_ref, cwv_ref, cbg_ref, cbv_ref, wd_ref, lnf_ref,
                o_ref, hn_ref, ag_ref, av_ref, *, nt, tm, nj, final_norm):
    i = pl.program_id(0)
    s = pl.program_id(1)
    hl = BF16_ROWS
    slot = s % 2

    def up(sl):
        hn = hn_ref[...]
        ag_ref[sl] = jnp.dot(hn, wg_ref[...], preferred_element_type=F32)
        av_ref[sl] = jnp.dot(hn, wv_ref[...], preferred_element_type=F32)

    def down(sl):
        def conv(a_ref, cw_ref, cb_ref):
            cw = cw_ref[...]
            return (a_ref[sl, pl.ds(hl - 2, tm), :] * cw[0:1] + a_ref[sl, pl.ds(hl - 1, tm), :] * cw[1:2]
                    + a_ref[sl, pl.ds(hl, tm), :] * cw[2:3] + cb_ref[...])

        gate = conv(ag_ref, cwg_ref, cbg_ref)
        val = conv(av_ref, cwv_ref, cbv_ref)
        g = (gate * jax.nn.sigmoid(gate) * val).astype(BF16)
        o_ref[...] += jnp.dot(g, wd_ref[...], preferred_element_type=F32)

    @pl.when(s == 0)
    def _():
        hn_ref[hl:, :] = _rms(h_ref[...], ln_ref[...]).astype(BF16)
        halo = _rms(halo_ref[...], ln_ref[...])
        hn_ref[:hl, :] = jnp.where(i % nt > 0, halo, 0.0).astype(BF16)
        o_ref[...] = h_ref[...]
        up(0)

    @pl.when(jnp.logical_and(s > 0, s < nj))
    def _():
        up(slot)
        down(1 - slot)

    @pl.when(s == nj)
    def _():
        down(1 - slot)
        if final_norm:
            o_ref[...] = _rms(o_ref[...], lnf_ref[...])


def _ffn(h, ln, w_up, conv_w, conv_b, w_down, ln_final, seq, tm, tf, final_norm):
    rows = h.shape[0]
    hl = BF16_ROWS
    nj = D_FF // tf
    nt = seq // tm
    up_j = lambda s: jnp.minimum(s, nj - 1)
    dn_j = lambda s: jnp.maximum(s - 1, 0)
    return pl.pallas_call(
        functools.partial(_ffn_kernel, nt=nt, tm=tm, nj=nj, final_norm=final_norm),
        grid=(rows // tm, nj + 1),
        in_specs=[
            pl.BlockSpec((tm, D_MODEL), lambda i, s: (i, 0), pipeline_mode=pl.Buffered(1)),
            pl.BlockSpec((hl, D_MODEL), lambda i, s: (jnp.maximum(i * (tm // hl) - 1, 0), 0)),
            pl.BlockSpec((1, D_MODEL), lambda i, s: (0, 0)),
            pl.BlockSpec((D_MODEL, tf), lambda i, s: (0, up_j(s))),
            pl.BlockSpec((D_MODEL, tf), lambda i, s: (0, nj + up_j(s))),
            pl.BlockSpec((3, tf), lambda i, s: (0, dn_j(s))),
            pl.BlockSpec((3, tf), lambda i, s: (0, nj + dn_j(s))),
            pl.BlockSpec((1, tf), lambda i, s: (0, dn_j(s))),
            pl.BlockSpec((1, tf), lambda i, s: (0, nj + dn_j(s))),
            pl.BlockSpec((tf, D_MODEL), lambda i, s: (dn_j(s), 0)),
            pl.BlockSpec((1, D_MODEL), lambda i, s: (0, 0)),
        ],
        out_specs=pl.BlockSpec((tm, D_MODEL), lambda i, s: (i, 0)),
        out_shape=jax.ShapeDtypeStruct((rows, D_MODEL), F32),
        scratch_shapes=[
            pltpu.VMEM((hl + tm, D_MODEL), BF16),
            pltpu.VMEM((2, hl + tm, tf), F32),
            pltpu.VMEM((2, hl + tm, tf), F32),
        ],
        compiler_params=_params(("parallel", "arbitrary")),
        name="conv_ffn",
    )(h, h, ln, w_up, w_up, conv_w, conv_w, conv_b, conv_b, w_down, ln_final)


MLA_DOWN_PAD = MLA_Q_RANK + MLA_KV_RANK + 2 * LANE
MLA_QN = MLA_HEADS * MLA_NOPE
MLA_QR = MLA_HEADS * MLA_ROPE
MLA_KV = MLA_HEADS * (MLA_NOPE + MLA_V)
MLA_CHUNK = 1024
LOG2E = 1.4426950408889634


def _mla_proj_kernel(h_ref, ln_ref, wd_ref, qn_ref, kvn_ref, wq_ref, wkv_ref, cos_ref, sin_ref,
                     oqn_ref, oqr_ref, okv_ref, okr_ref):
    hn = _rms(h_ref[...], ln_ref[...]).astype(BF16)
    c = jnp.dot(hn, wd_ref[...], preferred_element_type=F32)
    cos = cos_ref[...]
    sin = sin_ref[...]
    scale = (MLA_NOPE + MLA_ROPE) ** -0.5 * LOG2E

    cq = _rms(c[:, :MLA_Q_RANK], qn_ref[...]).astype(BF16)
    for n0 in range(0, MLA_QN, MLA_CHUNK):
        qn = jnp.dot(cq, wq_ref[:, n0:n0 + MLA_CHUNK], preferred_element_type=F32)
        oqn_ref[:, n0:n0 + MLA_CHUNK] = (qn * scale).astype(BF16)
    cs = cos * scale
    ss = sin * scale
    for n0 in range(0, MLA_QR, MLA_CHUNK):
        qr = jnp.dot(cq, wq_ref[:, MLA_QN + n0:MLA_QN + n0 + MLA_CHUNK], preferred_element_type=F32)
        qp = jnp.dot(cq, wq_ref[:, MLA_QN + MLA_QR + n0:MLA_QN + MLA_QR + n0 + MLA_CHUNK],
                     preferred_element_type=F32)
        for m in range(MLA_CHUNK // LANE):
            ls = slice(m * LANE, (m + 1) * LANE)
            oqr_ref[:, n0 + m * LANE:n0 + (m + 1) * LANE] = (qr[:, ls] * cs + qp[:, ls] * ss).astype(BF16)

    ckv = _rms(c[:, MLA_Q_RANK:MLA_Q_RANK + MLA_KV_RANK], kvn_ref[...]).astype(BF16)
    for n0 in range(0, MLA_KV, MLA_CHUNK):
        okv_ref[:, n0:n0 + MLA_CHUNK] = jnp.dot(
            ckv, wkv_ref[:, n0:n0 + MLA_CHUNK], preferred_element_type=F32).astype(BF16)

    o = MLA_Q_RANK + MLA_KV_RANK
    kr2 = c[:, o:o + LANE] * cos + c[:, o + LANE:o + 2 * LANE] * sin
    left = lax.broadcasted_iota(jnp.int32, kr2.shape, 1) < MLA_ROPE
    okr_ref[:, :LANE] = jnp.where(left, kr2, 0.0).astype(BF16)
    okr_ref[:, LANE:] = jnp.where(left, 0.0, kr2).astype(BF16)


def _mla_proj(h, ln, wd, q_norm, kv_norm, wq, wkv, cos4, sin4, seq, tm):
    rows = h.shape[0]
    nt = seq // tm
    row = lambda i: (i, 0)
    return pl.pallas_call(
        _mla_proj_kernel,
        grid=(rows // tm,),
        in_specs=[
            pl.BlockSpec((tm, D_MODEL), row),
            _const_spec((1, D_MODEL)),
            _const_spec(wd.shape),
            _const_spec((1, MLA_Q_RANK)),
            _const_spec((1, MLA_KV_RANK)),
            _const_spec(wq.shape),
            _const_spec(wkv.shape),
            pl.BlockSpec((tm, LANE), lambda i: (i % nt, 0)),
            pl.BlockSpec((tm, LANE), lambda i: (i % nt, 0)),
        ],
        out_specs=[
            pl.BlockSpec((tm, MLA_QN), row),
            pl.BlockSpec((tm, MLA_QR), row),
            pl.BlockSpec((tm, MLA_KV), row),
            pl.BlockSpec((tm, 2 * LANE), row),
        ],
        out_shape=[
            jax.ShapeDtypeStruct((rows, MLA_QN), BF16),
            jax.ShapeDtypeStruct((rows, MLA_QR), BF16),
            jax.ShapeDtypeStruct((rows, MLA_KV), BF16),
            jax.ShapeDtypeStruct((rows, 2 * LANE), BF16),
        ],
        compiler_params=_params(("parallel",)),
        name="mla_proj",
    )(h, ln, wd, q_norm, kv_norm, wq, wkv, cos4, sin4)


def _attn_kernel(qn_ref, qr_ref, kv_ref, kr_ref, o_ref, *, tq, nq):
    qi = pl.program_id(2)
    qr = qr_ref[...]
    qs = [jnp.concatenate([qn_ref[:, s * LANE:(s + 1) * LANE], qr], axis=-1) for s in range(2)]

    def keys(s, r0, n):
        k = jnp.concatenate([kv_ref[pl.ds(r0, n), 2 * s * LANE:(2 * s + 1) * LANE],
                             kr_ref[pl.ds(r0, n), s * LANE:(s + 1) * LANE]], axis=-1)
        return k, kv_ref[pl.ds(r0, n), (2 * s + 1) * LANE:(2 * s + 2) * LANE]

    def update(q, kv, carry, mask=None):
        k, v = kv
        m, l, acc = carry
        s = lax.dot_general(q, k, (((1,), (1,)), ((), ())), preferred_element_type=F32)
        if mask is not None:
            s = jnp.where(mask, s, -1e30)
        m_new = jnp.maximum(m, jnp.max(s, axis=-1, keepdims=True))
        alpha = jnp.exp2(m - m_new)
        p = jnp.exp2(s - m_new)
        l = alpha * l + jnp.sum(p, axis=-1, keepdims=True)
        acc = alpha * acc + jnp.dot(p.astype(BF16), v, preferred_element_type=F32)
        return m_new, l, acc

    row = lax.broadcasted_iota(jnp.int32, (tq, tq), 0)
    col = lax.broadcasted_iota(jnp.int32, (tq, tq), 1)

    for c in range(nq):
        @pl.when(qi == c)
        def _(c=c):
            carries = [(jnp.full((tq, 1), -1e30, F32), jnp.zeros((tq, 1), F32), jnp.zeros((tq, MLA_V), F32))
                       for _ in range(2)]
            for kb in range(c):
                carries = [update(qs[s], keys(s, kb * tq, tq), carries[s]) for s in range(2)]
            for s in range(2):
                _, l, acc = update(qs[s], keys(s, c * tq, tq), carries[s], row >= col)
                o_ref[:, s * LANE:(s + 1) * LANE] = (acc / l).astype(BF16)


def _attention(qn, qr, kv, krp, nb, seq, tq):
    npair = MLA_HEADS // 2
    nq = seq // tq
    return pl.pallas_call(
        functools.partial(_attn_kernel, tq=tq, nq=nq),
        grid=(nb, npair, nq),
        in_specs=[
            pl.BlockSpec((tq, 2 * LANE), lambda b, h, i: (b * nq + i, h)),
            pl.BlockSpec((tq, LANE), lambda b, h, i: (b * nq + i, h)),
            pl.BlockSpec((seq, 4 * LANE), lambda b, h, i: (b, h)),
            pl.BlockSpec((seq, 2 * LANE), lambda b, h, i: (b, 0)),
        ],
        out_specs=pl.BlockSpec((tq, 2 * LANE), lambda b, h, i: (b * nq + i, h)),
        out_shape=jax.ShapeDtypeStruct((nb * seq, MLA_HEADS * MLA_V), BF16),
        compiler_params=_params(("parallel", "parallel", "arbitrary")),
        name="mla_attention",
    )(qn, qr, kv, krp)


def _rope_tables(n, dim):
    inv = ROPE_BASE ** (-jnp.arange(0, dim, 2, dtype=F32) / dim)
    ang = jnp.arange(n, dtype=F32)[:, None] * inv[None, :]
    return jnp.cos(ang), jnp.sin(ang)


def _rot_cols(w):
    lead = w.shape[:-1]
    w = w.reshape(lead + (-1, 2, MLA_ROPE // 2))
    return jnp.stack([-w[..., 1, :], w[..., 0, :]], axis=-2).reshape(lead + (-1,))


def kernel(x, ln_mix_even, w_in_even, s5_lambda_re, s5_lambda_im, s5_log_dt, s5_b_re, s5_b_im, s5_c_re, s5_c_im, s5_d, s5_glu_w, s5_glu_b, ret_norm, w_out_even, ln_mix_odd, mla_w_down, mla_q_norm, mla_w_uq, mla_kv_norm, mla_w_ukv, w_out_odd, ln_ffn, ffn_w_up, ffn_conv_w, ffn_conv_b, ffn_w_down, ln_final):
    nb, seq, d = x.shape
    assert d == D_MODEL and nb % 8 == 0 and seq % 128 == 0
    rows = nb * seq
    tm = min(seq, 1024)
    tmo = min(seq, 512)
    tf = 512
    row2 = lambda a: a.reshape(1, -1)

    x2d = x.reshape(rows, d)

    ret_cos, ret_sin = _rope_tables(seq, RET_HEAD_DIM)
    w_in = w_in_even[0].astype(BF16)
    u, proj = _inproj(x2d, row2(ln_mix_even[0]), w_in[:, :S5_WIDTH], w_in[:, S5_WIDTH:], ret_cos, ret_sin, seq, tm)

    ab_re, ab_im, bbr, bbi = _s5_discretise(s5_lambda_re[0], s5_lambda_im[0], s5_log_dt[0], s5_b_re[0], s5_b_im[0])
    a_re, a_im, bblk, cblk = _s5_pack(ab_re, ab_im, bbr, bbi, s5_c_re[0], s5_c_im[0])
    y_s5 = _s5(u.reshape(nb, seq, S5_WIDTH), bblk, cblk, a_re, a_im, row2(s5_d[0]),
               s5_glu_w[0].astype(BF16), row2(s5_glu_b[0]), tc=min(seq, 32))
    y_ret = _retention(proj, row2(ret_norm[0]), nb, seq)

    h = _outproj(x2d, [y_s5.reshape(rows, S5_WIDTH), y_ret], w_out_even[0].astype(BF16), tmo, "even_outproj")

    def ffn(h, layer, final_norm):
        return _ffn(h, row2(ln_ffn[layer]), ffn_w_up[layer].astype(BF16), ffn_conv_w[layer],
                    row2(ffn_conv_b[layer]), ffn_w_down[layer].astype(BF16), row2(ln_final),
                    seq, tm, tf, final_norm)

    h = ffn(h, 0, False)

    mla_cos, mla_sin = _rope_tables(seq, MLA_ROPE)
    cos4 = jnp.tile(mla_cos, (1, 4))
    sin4 = jnp.tile(mla_sin, (1, 4))
    wdn = mla_w_down[0]
    w_kr = wdn[:, MLA_Q_RANK + MLA_KV_RANK:]
    w_krp = _rot_cols(w_kr)
    wd = jnp.concatenate([wdn[:, :MLA_Q_RANK + MLA_KV_RANK], w_kr, w_kr, w_krp, w_krp], axis=1).astype(BF16)
    wq3 = mla_w_uq[0].reshape(MLA_Q_RANK, MLA_HEADS, MLA_NOPE + MLA_ROPE)
    wq_n = wq3[:, :, :MLA_NOPE].reshape(MLA_Q_RANK, MLA_QN)
    wq_r = wq3[:, :, MLA_NOPE:].reshape(MLA_Q_RANK, MLA_QR)
    wq = jnp.concatenate([wq_n, wq_r, _rot_cols(wq_r)], axis=1).astype(BF16)
    qn, qr, kv, krp = _mla_proj(h, row2(ln_mix_odd[0]), wd, row2(mla_q_norm[0]), row2(mla_kv_norm[0]),
                                wq, mla_w_ukv[0].astype(BF16), cos4, sin4, seq, tmo)
    o = _attention(qn, qr, kv, krp, nb, seq, tq=min(seq, 512))
    h = _outproj(h, [o], w_out_odd[0].astype(BF16), tmo, "odd_outproj")
    h = ffn(h, 1, True)
    return h.reshape(nb, seq, d)
```

```python
import functools

import numpy as np
import jax
import jax.numpy as jnp
from jax import lax
from jax.experimental import pallas as pl
from jax.experimental.pallas import tpu as pltpu

F32 = jnp.float32
BF16 = jnp.bfloat16

EPS = 1e-6
ROPE_BASE = 10000.0
D_MODEL = 2048
S5_WIDTH = 512
S5_GROUP = 16
S5_GROUPS = 32
S5_STATE = 64
S5_GB = 8
S5_NBLK = S5_GROUPS // S5_GB
S5_NSTATE = S5_GROUPS * S5_STATE
RET_HEADS = 6
RET_HEAD_DIM = 256
RET_WIDTH = RET_HEADS * RET_HEAD_DIM
RET_CHUNK = 128
EVEN_IN = S5_WIDTH + 4 * RET_WIDTH
MLA_HEADS = 16
MLA_NOPE = 128
MLA_ROPE = 64
MLA_V = 128
MLA_Q_RANK = 512
MLA_KV_RANK = 512
D_FF = 5632
FFN_DOWN_SUBTILES = 4

LANE = 128
BF16_ROWS = 16
VMEM_LIMIT = 60 * 1024 * 1024


def _params(sem, vmem=VMEM_LIMIT, flags=None):
    return pltpu.CompilerParams(dimension_semantics=sem, vmem_limit_bytes=vmem, flags=flags)


def _rms(x, g):
    return x * lax.rsqrt(jnp.mean(x * x, axis=-1, keepdims=True) + EPS) * g


def _const_spec(shape):
    nd = len(shape)
    return pl.BlockSpec(shape, lambda *_: (0,) * nd, pipeline_mode=pl.Buffered(1))


IN_TN = RET_WIDTH
IN_NJ = 4
IN_CHUNK = 512
IN_SUBTILES = 2


def _inproj_kernel(h_ref, ln_ref, wu_ref, w0_ref, w1_ref, w2_ref, cos_ref, sin_ref, u_ref, p_ref, hn_ref):
    j = pl.program_id(1)
    ts = hn_ref.shape[0] // IN_SUBTILES
    w_ref = (w0_ref, w1_ref, w2_ref)

    def pieces(wrefs):
        for c, wref in enumerate(wrefs):
            for sub in range(IN_SUBTILES):
                rs = slice(sub * ts, (sub + 1) * ts)
                cs = slice(c * IN_CHUNK, (c + 1) * IN_CHUNK)
                yield rs, cs, jnp.dot(hn_ref[rs, :], wref[...], preferred_element_type=F32)

    @pl.when(j == 0)
    def _():
        hn_ref[...] = _rms(h_ref[...], ln_ref[...]).astype(BF16)
        for rs, cs, acc in pieces((wu_ref,)):
            u_ref[rs, cs] = acc

    @pl.when(j <= 1)
    def _():
        sc = jnp.where(j == 1, RET_HEAD_DIM ** -0.5, 1.0).astype(F32)
        for rs, cs, acc in pieces(w_ref):
            c = cos_ref[rs, :] * sc
            s = sin_ref[rs, :] * sc
            for hh in range(IN_CHUNK // RET_HEAD_DIM):
                o = hh * RET_HEAD_DIM
                x1 = acc[:, o:o + LANE]
                x2 = acc[:, o + LANE:o + 2 * LANE]
                p_ref[rs, cs.start + o:cs.start + o + LANE] = (x1 * c - x2 * s).astype(BF16)
                p_ref[rs, cs.start + o + LANE:cs.start + o + 2 * LANE] = (x2 * c + x1 * s).astype(BF16)

    @pl.when(j == 2)
    def _():
        for rs, cs, acc in pieces(w_ref):
            p_ref[rs, cs] = acc.astype(BF16)

    @pl.when(j == 3)
    def _():
        for rs, cs, acc in pieces(w_ref):
            p_ref[rs, cs] = (acc * jax.nn.sigmoid(acc)).astype(BF16)


def _inproj(x2d, ln, w, cos, sin, seq, tm):
    rows = x2d.shape[0]
    nt = seq // tm
    nc = IN_TN // IN_CHUNK
    wblk = lambda c: pl.BlockSpec((D_MODEL, IN_CHUNK), lambda i, j: (0, 1 + nc * j + c))
    return pl.pallas_call(
        _inproj_kernel,
        grid=(rows // tm, IN_NJ),
        in_specs=[
            pl.BlockSpec((tm, D_MODEL), lambda i, j: (i, 0)),
            pl.BlockSpec((1, D_MODEL), lambda i, j: (0, 0)),
            pl.BlockSpec((D_MODEL, S5_WIDTH), lambda i, j: (0, 0), pipeline_mode=pl.Buffered(1)),
            wblk(0), wblk(1), wblk(2),
            pl.BlockSpec((tm, LANE), lambda i, j: (i % nt, 0)),
            pl.BlockSpec((tm, LANE), lambda i, j: (i % nt, 0)),
        ],
        out_specs=[
            pl.BlockSpec((tm, S5_WIDTH), lambda i, j: (i, 0)),
            pl.BlockSpec((tm, IN_TN), lambda i, j: (i, j)),
        ],
        out_shape=[
            jax.ShapeDtypeStruct((rows, S5_WIDTH), F32),
            jax.ShapeDtypeStruct((rows, EVEN_IN - S5_WIDTH), BF16),
        ],
        scratch_shapes=[pltpu.VMEM((tm, D_MODEL), BF16)],
        compiler_params=_params(("parallel", "arbitrary")),
        name="even_inproj",
    )(x2d, ln, w, w, w, w, cos, sin)


def _s5_disc_kernel(lr_ref, li_ref, ldt_ref, br_ref, bi_ref, are_ref, aim_ref, bbr_ref, bbi_ref):
    lr = lr_ref[...]
    li = li_ref[...]
    dt = jnp.exp(ldt_ref[...])
    mag = jnp.exp(lr * dt)
    ab_re = mag * jnp.cos(li * dt)
    ab_im = mag * jnp.sin(li * dt)
    den = lr * lr + li * li
    nr = ab_re - 1.0
    ni = ab_im
    f_re = (nr * lr + ni * li) / den
    f_im = (ni * lr - nr * li) / den
    are_ref[...] = ab_re
    aim_ref[...] = ab_im
    br = br_ref[...]
    bi = bi_ref[...]
    bbr_ref[...] = f_re[:, None, :] * br - f_im[:, None, :] * bi
    bbi_ref[...] = f_re[:, None, :] * bi + f_im[:, None, :] * br


def _s5_discretise(lam_re, lam_im, log_dt, b_re, b_im):
    g, p, hh = S5_GROUPS, S5_STATE, S5_GROUP
    brt = jnp.swapaxes(b_re, 1, 2)
    bit = jnp.swapaxes(b_im, 1, 2)
    return pl.pallas_call(
        _s5_disc_kernel,
        out_shape=[
            jax.ShapeDtypeStruct((g, p), F32),
            jax.ShapeDtypeStruct((g, p), F32),
            jax.ShapeDtypeStruct((g, hh, p), F32),
            jax.ShapeDtypeStruct((g, hh, p), F32),
        ],
        name="s5_discretise",
    )(lam_re, lam_im, log_dt.reshape(g, 1), brt, bit)


def _s5_pack(ab_re, ab_im, bbr, bbi, c_re, c_im):
    eye = jnp.eye(S5_GB, dtype=F32)
    nb, gb, p, hh = S5_NBLK, S5_GB, S5_STATE, S5_GROUP

    def pack_b(bb):
        return jnp.einsum('gy,aghp->aghyp', eye, bb.reshape(nb, gb, hh, p)).reshape(nb, gb * hh, gb * p)

    def pack_c(cc):
        return jnp.einsum('gy,aghp->agpyh', eye, cc.reshape(nb, gb, hh, p)).reshape(nb, gb * p, gb * hh)

    bblk = jnp.concatenate([pack_b(bbr), pack_b(bbi)], axis=-1).astype(BF16)
    cblk = jnp.concatenate([pack_c(c_re), pack_c(-c_im)], axis=1).astype(BF16)
    return ab_re.reshape(1, S5_NSTATE), ab_im.reshape(1, S5_NSTATE), bblk, cblk


S5_SLAB = 256


def _s5_kernel(u_hbm, bblk_ref, cblk_ref, are_ref, aim_ref, d_ref, gw_ref, gb_ref, y_hbm,
               ubuf, ybuf, in_sem, out_sem, bur_ref, bui_ref, xr_ref, xi_ref, sr_ref, si_ref, *, nb, tc):
    t = pl.program_id(0)
    nt = pl.num_programs(0)
    slot = t % 2
    half = S5_GB * S5_STATE
    wu = S5_GB * S5_GROUP

    def in_copy(step, sl, b):
        return pltpu.make_async_copy(u_hbm.at[b, pl.ds(step * tc, tc), :], ubuf.at[sl, :, b, :], in_sem.at[sl, b])

    def out_copy(step, sl, b):
        return pltpu.make_async_copy(ybuf.at[sl, :, b, :], y_hbm.at[b, pl.ds(step * tc, tc), :], out_sem.at[sl, b])

    @pl.when(t == 0)
    def _():
        for b in range(nb):
            in_copy(0, 0, b).start()
        sr_ref[...] = jnp.zeros_like(sr_ref)
        si_ref[...] = jnp.zeros_like(si_ref)

    @pl.when(t + 1 < nt)
    def _():
        for b in range(nb):
            in_copy(t + 1, 1 - slot, b).start()

    for b in range(nb):
        in_copy(t, slot, b).wait()

    u = ubuf[slot].reshape(tc * nb, S5_WIDTH)
    ub = u.astype(BF16)
    for a in range(S5_NBLK):
        bu = jnp.dot(ub[:, a * wu:(a + 1) * wu], bblk_ref[a], preferred_element_type=F32)
        bur_ref[:, a * half:(a + 1) * half] = bu[:, :half]
        bui_ref[:, a * half:(a + 1) * half] = bu[:, half:]

    for sl in range(S5_NSTATE // S5_SLAB):
        cs = slice(sl * S5_SLAB, (sl + 1) * S5_SLAB)
        ar = jnp.broadcast_to(are_ref[:, cs], (nb, S5_SLAB))
        ai = jnp.broadcast_to(aim_ref[:, cs], (nb, S5_SLAB))

        def step(k, carry, cs=cs, ar=ar, ai=ai):
            xr, xi = carry
            r0 = pl.multiple_of(k * nb, nb)
            nr = ar * xr - ai * xi + bur_ref[pl.ds(r0, nb), cs]
            ni = ar * xi + ai * xr + bui_ref[pl.ds(r0, nb), cs]
            xr_ref[pl.ds(r0, nb), cs] = nr.astype(BF16)
            xi_ref[pl.ds(r0, nb), cs] = ni.astype(BF16)
            return nr, ni

        xr, xi = lax.fori_loop(0, tc, step, (sr_ref[:, cs], si_ref[:, cs]), unroll=4)
        sr_ref[:, cs] = xr
        si_ref[:, cs] = xi

    ys = []
    for a in range(S5_NBLK):
        ya = jnp.dot(xr_ref[:, a * half:(a + 1) * half], cblk_ref[a, :half, :], preferred_element_type=F32)
        ya = ya + jnp.dot(xi_ref[:, a * half:(a + 1) * half], cblk_ref[a, half:, :], preferred_element_type=F32)
        ys.append(ya)
    y = jnp.concatenate(ys, axis=-1) + d_ref[...] * u
    y = jax.nn.gelu(y)
    z = jnp.dot(y.astype(BF16), gw_ref[...], preferred_element_type=F32) + gb_ref[...]
    y = y * jax.nn.sigmoid(z)

    @pl.when(t >= 2)
    def _():
        for b in range(nb):
            out_copy(t - 2, slot, b).wait()

    ybuf[slot] = y.reshape(tc, nb, S5_WIDTH)
    for b in range(nb):
        out_copy(t, slot, b).start()

    @pl.when(t == nt - 1)
    def _():
        for b in range(nb):
            out_copy(t, slot, b).wait()

        @pl.when(nt >= 2)
        def _():
            for b in range(nb):
                out_copy(t - 1, 1 - slot, b).wait()


def _s5(u3, bblk, cblk, a_re, a_im, d_skip, glu_w, glu_b, tc):
    nb, seq, _ = u3.shape
    rows = nb * tc
    return pl.pallas_call(
        functools.partial(_s5_kernel, nb=nb, tc=tc),
        grid=(seq // tc,),
        in_specs=[
            pl.BlockSpec(memory_space=pl.ANY),
            _const_spec(bblk.shape),
            _const_spec(cblk.shape),
            _const_spec((1, S5_NSTATE)),
            _const_spec((1, S5_NSTATE)),
            _const_spec((1, S5_WIDTH)),
            _const_spec((S5_WIDTH, S5_WIDTH)),
            _const_spec((1, S5_WIDTH)),
        ],
        out_specs=pl.BlockSpec(memory_space=pl.ANY),
        out_shape=jax.ShapeDtypeStruct((nb, seq, S5_WIDTH), F32),
        scratch_shapes=[
            pltpu.VMEM((2, tc, nb, S5_WIDTH), F32),
            pltpu.VMEM((2, tc, nb, S5_WIDTH), F32),
            pltpu.SemaphoreType.DMA((2, nb)),
            pltpu.SemaphoreType.DMA((2, nb)),
            pltpu.VMEM((rows, S5_NSTATE), F32),
            pltpu.VMEM((rows, S5_NSTATE), F32),
            pltpu.VMEM((rows, S5_NSTATE), BF16),
            pltpu.VMEM((rows, S5_NSTATE), BF16),
            pltpu.VMEM((nb, S5_NSTATE), F32),
            pltpu.VMEM((nb, S5_NSTATE), F32),
        ],
        compiler_params=_params(("arbitrary",)),
        name="s5_scan",
    )(u3, bblk, cblk, a_re, a_im, d_skip, glu_w, glu_b)


def _retention_consts():
    hs = np.arange(RET_HEADS, dtype=np.float32)
    gamma_log = np.log((1.0 - np.exp2(-5.0 - hs)).astype(np.float32)).astype(np.float32)
    c = RET_CHUNK
    idx = np.arange(c, dtype=np.float32)
    diff = idx[:, None] - idx[None, :]
    dmask = np.where(diff[None] >= 0, np.exp(np.maximum(diff[None], 0.0) * gamma_log[:, None, None]), 0.0)
    xi = np.exp((idx[None, :] + 1.0) * gamma_log[:, None])
    zeta = np.exp((c - 1.0 - idx[None, :]) * gamma_log[:, None])
    g_chunk = np.exp(c * gamma_log)
    xi_b = np.broadcast_to(xi[:, :, None], (RET_HEADS, c, RET_HEAD_DIM))
    zeta_b = np.broadcast_to(zeta[:, :, None], (RET_HEADS, c, RET_HEAD_DIM))
    return (dmask.astype(np.float32), np.ascontiguousarray(xi_b, np.float32),
            np.ascontiguousarray(zeta_b, np.float32), [float(v) for v in g_chunk.astype(np.float32)])


def _retention_kernel(q_ref, k_ref, v_ref, sg_ref, dm_ref, xi_ref, zeta_ref, rn_ref, y_ref, r_ref, *, g_chunk):
    c = pl.program_id(1)

    @pl.when(c == 0)
    def _():
        r_ref[...] = jnp.zeros_like(r_ref)

    for h in range(RET_HEADS):
        hs = slice(h * RET_HEAD_DIM, (h + 1) * RET_HEAD_DIM)
        q = q_ref[:, hs]
        k = k_ref[:, hs]
        v = v_ref[:, hs]
        r = r_ref[h]
        s = lax.dot_general(q, k, (((1,), (1,)), ((), ())), preferred_element_type=F32)
        inner = (s * dm_ref[h]).astype(BF16)
        o = jnp.dot(inner, v, preferred_element_type=F32)
        o = o + xi_ref[h] * jnp.dot(q, r.astype(BF16), preferred_element_type=F32)
        kz = (k.astype(F32) * zeta_ref[h]).astype(BF16)
        r_ref[h] = r * g_chunk[h] + lax.dot_general(kz, v, (((0,), (0,)), ((), ())), preferred_element_type=F32)
        o = o * lax.rsqrt(jnp.mean(o * o, axis=-1, keepdims=True) + EPS)
        o = o * rn_ref[:, hs]
        y_ref[:, hs] = (sg_ref[:, hs].astype(F32) * o).astype(BF16)


def _retention(proj, ret_norm, nb, seq):
    dmask, xi_b, zeta_b, g_chunk = _retention_consts()
    nc = seq // RET_CHUNK
    blk = (RET_CHUNK, RET_WIDTH)
    return pl.pallas_call(
        functools.partial(_retention_kernel, g_chunk=g_chunk),
        grid=(nb, nc),
        in_specs=[
            pl.BlockSpec(blk, lambda b, c: (b * nc + c, 0)),
            pl.BlockSpec(blk, lambda b, c: (b * nc + c, 1)),
            pl.BlockSpec(blk, lambda b, c: (b * nc + c, 2)),
            pl.BlockSpec(blk, lambda b, c: (b * nc + c, 3)),
            _const_spec(dmask.shape),
            _const_spec(xi_b.shape),
            _const_spec(zeta_b.shape),
            _const_spec((1, RET_WIDTH)),
        ],
        out_specs=pl.BlockSpec(blk, lambda b, c: (b * nc + c, 0)),
        out_shape=jax.ShapeDtypeStruct((nb * seq, RET_WIDTH), BF16),
        scratch_shapes=[pltpu.VMEM((RET_HEADS, RET_HEAD_DIM, RET_HEAD_DIM), F32)],
        compiler_params=_params(("parallel", "arbitrary")),
        name="retention",
    )(proj, proj, proj, proj, jnp.asarray(dmask), jnp.asarray(xi_b), jnp.asarray(zeta_b), ret_norm)


def _outproj_kernel(*refs, n_in):
    res_ref = refs[0]
    ys = refs[1:1 + n_in]
    w_ref = refs[1 + n_in]
    o_ref = refs[2 + n_in]
    acc = res_ref[...]
    k0 = 0
    for y in ys:
        kk = y.shape[1]
        acc = acc + jnp.dot(y[...].astype(BF16), w_ref[k0:k0 + kk, :], preferred_element_type=F32)
        k0 += kk
    o_ref[...] = acc


def _outproj(res, ys, w, tm, name):
    rows, d = res.shape
    row = lambda i: (i, 0)
    return pl.pallas_call(
        functools.partial(_outproj_kernel, n_in=len(ys)),
        grid=(rows // tm,),
        in_specs=([pl.BlockSpec((tm, d), row)] + [pl.BlockSpec((tm, y.shape[1]), row) for y in ys]
                  + [_const_spec(w.shape)]),
        out_specs=pl.BlockSpec((tm, d), row),
        out_shape=jax.ShapeDtypeStruct((rows, d), F32),
        compiler_params=_params(("parallel",)),
        name=name,
    )(res, *ys, w)


def _ffn_kernel(h_ref, halo_ref, ln_ref, wg_ref, wv_ref, cwg_ref, cwv_ref, cbg_ref, cbv_ref, wd_ref, lnf_ref,
                o_ref, hn_ref, ag_ref, av_ref, *, nt, tm, nj, final_norm):
    i = pl.program_id(0)
    s = pl.program_id(1)
    hl = BF16_ROWS
    slot = s % 2

    def up(sl):
        hn = hn_ref[...]
        ag_ref[sl] = jnp.dot(hn, wg_ref[...], preferred_element_type=F32)
        av_ref[sl] = jnp.dot(hn, wv_ref[...], preferred_element_type=F32)

    def down(sl):
        ts = tm // FFN_DOWN_SUBTILES
        for sub in range(FFN_DOWN_SUBTILES):
            r0 = sub * ts

            def conv(a_ref, cw_ref, cb_ref, r0=r0):
                cw = cw_ref[...]
                return (a_ref[sl, pl.ds(r0 + hl - 2, ts), :] * cw[0:1] + a_ref[sl, pl.ds(r0 + hl - 1, ts), :] * cw[1:2]
                        + a_ref[sl, pl.ds(r0 + hl, ts), :] * cw[2:3] + cb_ref[...])

            gate = conv(ag_ref, cwg_ref, cbg_ref)
            val = conv(av_ref, cwv_ref, cbv_ref)
            g = (gate * jax.nn.sigmoid(gate) * val).astype(BF16)
            o_ref[r0:r0 + ts, :] += jnp.dot(g, wd_ref[...], preferred_element_type=F32)

    @pl.when(s == 0)
    def _():
        hn_ref[hl:, :] = _rms(h_ref[...], ln_ref[...]).astype(BF16)
        halo = _rms(halo_ref[...], ln_ref[...])
        hn_ref[:hl, :] = jnp.where(i % nt > 0, halo, 0.0).astype(BF16)
        o_ref[...] = h_ref[...]
        up(0)

    @pl.when(jnp.logical_and(s > 0, s < nj))
    def _():
        up(slot)
        down(1 - slot)

    @pl.when(s == nj)
    def _():
        down(1 - slot)
        if final_norm:
            o_ref[...] = _rms(o_ref[...], lnf_ref[...])


def _ffn(h, ln, w_up, conv_w, conv_b, w_down, ln_final, seq, tm, tf, final_norm):
    rows = h.shape[0]
    hl = BF16_ROWS
    nj = D_FF // tf
    nt = seq // tm
    up_j = lambda s: jnp.minimum(s, nj - 1)
    dn_j = lambda s: jnp.maximum(s - 1, 0)
    return pl.pallas_call(
        functools.partial(_ffn_kernel, nt=nt, tm=tm, nj=nj, final_norm=final_norm),
        grid=(rows // tm, nj + 1),
        in_specs=[
            pl.BlockSpec((tm, D_MODEL), lambda i, s: (i, 0), pipeline_mode=pl.Buffered(1)),
            pl.BlockSpec((hl, D_MODEL), lambda i, s: (jnp.maximum(i * (tm // hl) - 1, 0), 0)),
            pl.BlockSpec((1, D_MODEL), lambda i, s: (0, 0)),
            pl.BlockSpec((D_MODEL, tf), lambda i, s: (0, up_j(s))),
            pl.BlockSpec((D_MODEL, tf), lambda i, s: (0, nj + up_j(s))),
            pl.BlockSpec((3, tf), lambda i, s: (0, dn_j(s))),
            pl.BlockSpec((3, tf), lambda i, s: (0, nj + dn_j(s))),
            pl.BlockSpec((1, tf), lambda i, s: (0, dn_j(s))),
            pl.BlockSpec((1, tf), lambda i, s: (0, nj + dn_j(s))),
            pl.BlockSpec((tf, D_MODEL), lambda i, s: (dn_j(s), 0)),
            pl.BlockSpec((1, D_MODEL), lambda i, s: (0, 0)),
        ],
        out_specs=pl.BlockSpec((tm, D_MODEL), lambda i, s: (i, 0)),
        out_shape=jax.ShapeDtypeStruct((rows, D_MODEL), F32),
        scratch_shapes=[
            pltpu.VMEM((hl + tm, D_MODEL), BF16),
            pltpu.VMEM((2, hl + tm, tf), F32),
            pltpu.VMEM((2, hl + tm, tf), F32),
        ],
        compiler_params=_params(("parallel", "arbitrary")),
        name="conv_ffn",
    )(h, h, ln, w_up, w_up, conv_w, conv_w, conv_b, conv_b, w_down, ln_final)


MLA_DOWN_PAD = MLA_Q_RANK + MLA_KV_RANK + 2 * LANE
MLA_QN = MLA_HEADS * MLA_NOPE
MLA_QR = MLA_HEADS * MLA_ROPE
MLA_KV = MLA_HEADS * (MLA_NOPE + MLA_V)
MLA_CHUNK = 1024
LOG2E = 1.4426950408889634


def _mla_proj_kernel(h_ref, ln_ref, wd_ref, qn_ref, kvn_ref, wq_ref, wkv_ref, cos_ref, sin_ref,
                     oqn_ref, oqr_ref, okv_ref, okr_ref):
    hn = _rms(h_ref[...], ln_ref[...]).astype(BF16)
    c = jnp.dot(hn, wd_ref[...], preferred_element_type=F32)
    cos = cos_ref[...]
    sin = sin_ref[...]
    scale = (MLA_NOPE + MLA_ROPE) ** -0.5 * LOG2E

    cq = _rms(c[:, :MLA_Q_RANK], qn_ref[...]).astype(BF16)
    for n0 in range(0, MLA_QN, MLA_CHUNK):
        qn = jnp.dot(cq, wq_ref[:, n0:n0 + MLA_CHUNK], preferred_element_type=F32)
        oqn_ref[:, n0:n0 + MLA_CHUNK] = (qn * scale).astype(BF16)
    cs = cos * scale
    ss = sin * scale
    for n0 in range(0, MLA_QR, MLA_CHUNK):
        qr = jnp.dot(cq, wq_ref[:, MLA_QN + n0:MLA_QN + n0 + MLA_CHUNK], preferred_element_type=F32)
        qp = jnp.dot(cq, wq_ref[:, MLA_QN + MLA_QR + n0:MLA_QN + MLA_QR + n0 + MLA_CHUNK],
                     preferred_element_type=F32)
        for m in range(MLA_CHUNK // LANE):
            ls = slice(m * LANE, (m + 1) * LANE)
            oqr_ref[:, n0 + m * LANE:n0 + (m + 1) * LANE] = (qr[:, ls] * cs + qp[:, ls] * ss).astype(BF16)

    ckv = _rms(c[:, MLA_Q_RANK:MLA_Q_RANK + MLA_KV_RANK], kvn_ref[...]).astype(BF16)
    for n0 in range(0, MLA_KV, MLA_CHUNK):
        okv_ref[:, n0:n0 + MLA_CHUNK] = jnp.dot(
            ckv, wkv_ref[:, n0:n0 + MLA_CHUNK], preferred_element_type=F32).astype(BF16)

    o = MLA_Q_RANK + MLA_KV_RANK
    kr2 = c[:, o:o + LANE] * cos + c[:, o + LANE:o + 2 * LANE] * sin
    left = lax.broadcasted_iota(jnp.int32, kr2.shape, 1) < MLA_ROPE
    okr_ref[:, :LANE] = jnp.where(left, kr2, 0.0).astype(BF16)
    okr_ref[:, LANE:] = jnp.where(left, 0.0, kr2).astype(BF16)


def _mla_proj(h, ln, wd, q_norm, kv_norm, wq, wkv, cos4, sin4, seq, tm):
    rows = h.shape[0]
    nt = seq // tm
    row = lambda i: (i, 0)
    return pl.pallas_call(
        _mla_proj_kernel,
        grid=(rows // tm,),
        in_specs=[
            pl.BlockSpec((tm, D_MODEL), row),
            _const_spec((1, D_MODEL)),
            _const_spec(wd.shape),
            _const_spec((1, MLA_Q_RANK)),
            _const_spec((1, MLA_KV_RANK)),
            _const_spec(wq.shape),
            _const_spec(wkv.shape),
            pl.BlockSpec((tm, LANE), lambda i: (i % nt, 0)),
            pl.BlockSpec((tm, LANE), lambda i: (i % nt, 0)),
        ],
        out_specs=[
            pl.BlockSpec((tm, MLA_QN), row),
            pl.BlockSpec((tm, MLA_QR), row),
            pl.BlockSpec((tm, MLA_KV), row),
            pl.BlockSpec((tm, 2 * LANE), row),
        ],
        out_shape=[
            jax.ShapeDtypeStruct((rows, MLA_QN), BF16),
            jax.ShapeDtypeStruct((rows, MLA_QR), BF16),
            jax.ShapeDtypeStruct((rows, MLA_KV), BF16),
            jax.ShapeDtypeStruct((rows, 2 * LANE), BF16),
        ],
        compiler_params=_params(("parallel",)),
        name="mla_proj",
    )(h, ln, wd, q_norm, kv_norm, wq, wkv, cos4, sin4)


def _attn_kernel(qn_ref, qr_ref, kv_ref, kr_ref, o_ref, *, tq, nq):
    qi = pl.program_id(2)
    qr = qr_ref[...]
    qs = [jnp.concatenate([qn_ref[:, s * LANE:(s + 1) * LANE], qr], axis=-1) for s in range(2)]

    def keys(s, r0, n):
        k = jnp.concatenate([kv_ref[pl.ds(r0, n), 2 * s * LANE:(2 * s + 1) * LANE],
                             kr_ref[pl.ds(r0, n), s * LANE:(s + 1) * LANE]], axis=-1)
        return k, kv_ref[pl.ds(r0, n), (2 * s + 1) * LANE:(2 * s + 2) * LANE]

    def update(q, kv, carry, mask=None):
        k, v = kv
        m, l, acc = carry
        s = lax.dot_general(q, k, (((1,), (1,)), ((), ())), preferred_element_type=F32)
        if mask is not None:
            s = jnp.where(mask, s, -1e30)
        m_new = jnp.maximum(m, jnp.max(s, axis=-1, keepdims=True))
        alpha = jnp.exp2(m - m_new)
        p = jnp.exp2(s - m_new)
        l = alpha * l + jnp.sum(p, axis=-1, keepdims=True)
        acc = alpha * acc + jnp.dot(p.astype(BF16), v, preferred_element_type=F32)
        return m_new, l, acc

    row = lax.broadcasted_iota(jnp.int32, (tq, tq), 0)
    col = lax.broadcasted_iota(jnp.int32, (tq, tq), 1)

    for c in range(nq):
        @pl.when(qi == c)
        def _(c=c):
            carries = [(jnp.full((tq, 1), -1e30, F32), jnp.zeros((tq, 1), F32), jnp.zeros((tq, MLA_V), F32))
                       for _ in range(2)]
            for kb in range(c):
                carries = [update(qs[s], keys(s, kb * tq, tq), carries[s]) for s in range(2)]
            for s in range(2):
                _, l, acc = update(qs[s], keys(s, c * tq, tq), carries[s], row >= col)
                o_ref[:, s * LANE:(s + 1) * LANE] = (acc / l).astype(BF16)


def _attention(qn, qr, kv, krp, nb, seq, tq):
    npair = MLA_HEADS // 2
    nq = seq // tq
    return pl.pallas_call(
        functools.partial(_attn_kernel, tq=tq, nq=nq),
        grid=(nb, npair, nq),
        in_specs=[
            pl.BlockSpec((tq, 2 * LANE), lambda b, h, i: (b * nq + i, h)),
            pl.BlockSpec((tq, LANE), lambda b, h, i: (b * nq + i, h)),
            pl.BlockSpec((seq, 4 * LANE), lambda b, h, i: (b, h)),
            pl.BlockSpec((seq, 2 * LANE), lambda b, h, i: (b, 0)),
        ],
        out_specs=pl.BlockSpec((tq, 2 * LANE), lambda b, h, i: (b * nq + i, h)),
        out_shape=jax.ShapeDtypeStruct((nb * seq, MLA_HEADS * MLA_V), BF16),
        compiler_params=_params(("parallel", "parallel", "arbitrary")),
        name="mla_attention",
    )(qn, qr, kv, krp)


def _rope_tables(n, dim):
    inv = ROPE_BASE ** (-jnp.arange(0, dim, 2, dtype=F32) / dim)
    ang = jnp.arange(n, dtype=F32)[:, None] * inv[None, :]
    return jnp.cos(ang), jnp.sin(ang)


def _rot_cols(w):
    lead = w.shape[:-1]
    w = w.reshape(lead + (-1, 2, MLA_ROPE // 2))
    return jnp.stack([-w[..., 1, :], w[..., 0, :]], axis=-2).reshape(lead + (-1,))


def kernel(x, ln_mix_even, w_in_even, s5_lambda_re, s5_lambda_im, s5_log_dt, s5_b_re, s5_b_im, s5_c_re, s5_c_im, s5_d, s5_glu_w, s5_glu_b, ret_norm, w_out_even, ln_mix_odd, mla_w_down, mla_q_norm, mla_w_uq, mla_kv_norm, mla_w_ukv, w_out_odd, ln_ffn, ffn_w_up, ffn_conv_w, ffn_conv_b, ffn_w_down, ln_final):
    nb, seq, d = x.shape
    assert d == D_MODEL and nb % 8 == 0 and seq % 128 == 0
    rows = nb * seq
    tm = min(seq, 1024)
    tmo = min(seq, 512)
    tf = 512
    row2 = lambda a: a.reshape(1, -1)

    x2d = x.reshape(rows, d)

    ret_cos, ret_sin = _rope_tables(seq, RET_HEAD_DIM)
    u, proj = _inproj(x2d, row2(ln_mix_even[0]), w_in_even[0].astype(BF16), ret_cos, ret_sin, seq, tm)

    ab_re, ab_im, bbr, bbi = _s5_discretise(s5_lambda_re[0], s5_lambda_im[0], s5_log_dt[0], s5_b_re[0], s5_b_im[0])
    a_re, a_im, bblk, cblk = _s5_pack(ab_re, ab_im, bbr, bbi, s5_c_re[0], s5_c_im[0])
    y_s5 = _s5(u.reshape(nb, seq, S5_WIDTH), bblk, cblk, a_re, a_im, row2(s5_d[0]),
               s5_glu_w[0].astype(BF16), row2(s5_glu_b[0]), tc=min(seq, 32))
    y_ret = _retention(proj, row2(ret_norm[0]), nb, seq)

    h = _outproj(x2d, [y_s5.reshape(rows, S5_WIDTH), y_ret], w_out_even[0].astype(BF16), tmo, "even_outproj")

    def ffn(h, layer, final_norm):
        return _ffn(h, row2(ln_ffn[layer]), ffn_w_up[layer].astype(BF16), ffn_conv_w[layer],
                    row2(ffn_conv_b[layer]), ffn_w_down[layer].astype(BF16), row2(ln_final),
                    seq, tm, tf, final_norm)

    h = ffn(h, 0, False)

    mla_cos, mla_sin = _rope_tables(seq, MLA_ROPE)
    cos4 = jnp.tile(mla_cos, (1, 4))
    sin4 = jnp.tile(mla_sin, (1, 4))
    wdn = mla_w_down[0]
    w_kr = wdn[:, MLA_Q_RANK + MLA_KV_RANK:]
    w_krp = _rot_cols(w_kr)
    wd = jnp.concatenate([wdn[:, :MLA_Q_RANK + MLA_KV_RANK], w_kr, w_kr, w_krp, w_krp], axis=1).astype(BF16)
    wq3 = mla_w_uq[0].reshape(MLA_Q_RANK, MLA_HEADS, MLA_NOPE + MLA_ROPE)
    wq_n = wq3[:, :, :MLA_NOPE].reshape(MLA_Q_RANK, MLA_QN)
    wq_r = wq3[:, :, MLA_NOPE:].reshape(MLA_Q_RANK, MLA_QR)
    wq = jnp.concatenate([wq_n, wq_r, _rot_cols(wq_r)], axis=1).astype(BF16)
    qn, qr, kv, krp = _mla_proj(h, row2(ln_mix_odd[0]), wd, row2(mla_q_norm[0]), row2(mla_kv_norm[0]),
                                wq, mla_w_ukv[0].astype(BF16), cos4, sin4, seq, tmo)
    o = _attention(qn, qr, kv, krp, nb, seq, tq=min(seq, 512))
    h = _outproj(h, [o], w_out_odd[0].astype(BF16), tmo, "odd_outproj")
    h = ffn(h, 1, True)
    return h.reshape(nb, seq, d)
```

```python
import functools

import numpy as np
import jax
import jax.numpy as jnp
from jax import lax
from jax.experimental import pallas as pl
from jax.experimental.pallas import tpu as pltpu

F32 = jnp.float32
BF16 = jnp.bfloat16

EPS = 1e-6
ROPE_BASE = 10000.0
D_MODEL = 2048
S5_WIDTH = 512
S5_GROUP = 16
S5_GROUPS = 32
S5_STATE = 64
S5_GB = 8
S5_NBLK = S5_GROUPS // S5_GB
S5_NSTATE = S5_GROUPS * S5_STATE
RET_HEADS = 6
RET_HEAD_DIM = 256
RET_WIDTH = RET_HEADS * RET_HEAD_DIM
RET_CHUNK = 128
EVEN_IN = S5_WIDTH + 4 * RET_WIDTH
MLA_HEADS = 16
MLA_NOPE = 128
MLA_ROPE = 64
MLA_V = 128
MLA_Q_RANK = 512
MLA_KV_RANK = 512
D_FF = 5632
FFN_DOWN_SUBTILES = 4

LANE = 128
BF16_ROWS = 16
VMEM_LIMIT = 60 * 1024 * 1024


def _params(sem, vmem=VMEM_LIMIT, flags=None):
    return pltpu.CompilerParams(dimension_semantics=sem, vmem_limit_bytes=vmem, flags=flags)


def _rms(x, g):
    return x * lax.rsqrt(jnp.mean(x * x, axis=-1, keepdims=True) + EPS) * g


def _const_spec(shape):
    nd = len(shape)
    return pl.BlockSpec(shape, lambda *_: (0,) * nd, pipeline_mode=pl.Buffered(1))


IN_TN = RET_WIDTH
IN_NJ = 4
IN_CHUNK = 512
IN_SUBTILES = 2


def _inproj_kernel(h_ref, ln_ref, wu_ref, w0_ref, w1_ref, w2_ref, cos_ref, sin_ref, u_ref, p_ref, hn_ref):
    j = pl.program_id(1)
    ts = hn_ref.shape[0] // IN_SUBTILES
    w_ref = (w0_ref, w1_ref, w2_ref)

    def pieces(wrefs):
        for c, wref in enumerate(wrefs):
            for sub in range(IN_SUBTILES):
                rs = slice(sub * ts, (sub + 1) * ts)
                cs = slice(c * IN_CHUNK, (c + 1) * IN_CHUNK)
                yield rs, cs, jnp.dot(hn_ref[rs, :], wref[...], preferred_element_type=F32)

    @pl.when(j == 0)
    def _():
        hn_ref[...] = _rms(h_ref[...], ln_ref[...]).astype(BF16)
        for rs, cs, acc in pieces((wu_ref,)):
            u_ref[rs, cs] = acc

    @pl.when(j <= 1)
    def _():
        sc = jnp.where(j == 1, RET_HEAD_DIM ** -0.5, 1.0).astype(F32)
        for rs, cs, acc in pieces(w_ref):
            c = cos_ref[rs, :] * sc
            s = sin_ref[rs, :] * sc
            for hh in range(IN_CHUNK // RET_HEAD_DIM):
                o = hh * RET_HEAD_DIM
                x1 = acc[:, o:o + LANE]
                x2 = acc[:, o + LANE:o + 2 * LANE]
                p_ref[rs, cs.start + o:cs.start + o + LANE] = (x1 * c - x2 * s).astype(BF16)
                p_ref[rs, cs.start + o + LANE:cs.start + o + 2 * LANE] = (x2 * c + x1 * s).astype(BF16)

    @pl.when(j == 2)
    def _():
        for rs, cs, acc in pieces(w_ref):
            p_ref[rs, cs] = acc.astype(BF16)

    @pl.when(j == 3)
    def _():
        for rs, cs, acc in pieces(w_ref):
            p_ref[rs, cs] = (acc * jax.nn.sigmoid(acc)).astype(BF16)


def _inproj(x2d, ln, w, cos, sin, seq, tm):
    rows = x2d.shape[0]
    nt = seq // tm
    nc = IN_TN // IN_CHUNK
    wblk = lambda c: pl.BlockSpec((D_MODEL, IN_CHUNK), lambda i, j: (0, 1 + nc * j + c))
    return pl.pallas_call(
        _inproj_kernel,
        grid=(rows // tm, IN_NJ),
        in_specs=[
            pl.BlockSpec((tm, D_MODEL), lambda i, j: (i, 0)),
            pl.BlockSpec((1, D_MODEL), lambda i, j: (0, 0)),
            pl.BlockSpec((D_MODEL, S5_WIDTH), lambda i, j: (0, 0), pipeline_mode=pl.Buffered(1)),
            wblk(0), wblk(1), wblk(2),
            pl.BlockSpec((tm, LANE), lambda i, j: (i % nt, 0)),
            pl.BlockSpec((tm, LANE), lambda i, j: (i % nt, 0)),
        ],
        out_specs=[
            pl.BlockSpec((tm, S5_WIDTH), lambda i, j: (i, 0)),
            pl.BlockSpec((tm, IN_TN), lambda i, j: (i, j)),
        ],
        out_shape=[
            jax.ShapeDtypeStruct((rows, S5_WIDTH), F32),
            jax.ShapeDtypeStruct((rows, EVEN_IN - S5_WIDTH), BF16),
        ],
        scratch_shapes=[pltpu.VMEM((tm, D_MODEL), BF16)],
        compiler_params=_params(("parallel", "arbitrary")),
        name="even_inproj",
    )(x2d, ln, w, w, w, w, cos, sin)


def _s5_disc_kernel(lr_ref, li_ref, ldt_ref, br_ref, bi_ref, are_ref, aim_ref, bbr_ref, bbi_ref):
    lr = lr_ref[...]
    li = li_ref[...]
    dt = jnp.exp(ldt_ref[...])
    mag = jnp.exp(lr * dt)
    ab_re = mag * jnp.cos(li * dt)
    ab_im = mag * jnp.sin(li * dt)
    den = lr * lr + li * li
    nr = ab_re - 1.0
    ni = ab_im
    f_re = (nr * lr + ni * li) / den
    f_im = (ni * lr - nr * li) / den
    are_ref[...] = ab_re
    aim_ref[...] = ab_im
    br = br_ref[...]
    bi = bi_ref[...]
    bbr_ref[...] = f_re[:, None, :] * br - f_im[:, None, :] * bi
    bbi_ref[...] = f_re[:, None, :] * bi + f_im[:, None, :] * br


def _s5_discretise(lam_re, lam_im, log_dt, b_re, b_im):
    g, p, hh = S5_GROUPS, S5_STATE, S5_GROUP
    brt = jnp.swapaxes(b_re, 1, 2)
    bit = jnp.swapaxes(b_im, 1, 2)
    return pl.pallas_call(
        _s5_disc_kernel,
        out_shape=[
            jax.ShapeDtypeStruct((g, p), F32),
            jax.ShapeDtypeStruct((g, p), F32),
            jax.ShapeDtypeStruct((g, hh, p), F32),
            jax.ShapeDtypeStruct((g, hh, p), F32),
        ],
        name="s5_discretise",
    )(lam_re, lam_im, log_dt.reshape(g, 1), brt, bit)


def _s5_pack(ab_re, ab_im, bbr, bbi, c_re, c_im):
    eye = jnp.eye(S5_GB, dtype=F32)
    nb, gb, p, hh = S5_NBLK, S5_GB, S5_STATE, S5_GROUP

    def pack_b(bb):
        return jnp.einsum('gy,aghp->aghyp', eye, bb.reshape(nb, gb, hh, p)).reshape(nb, gb * hh, gb * p)

    def pack_c(cc):
        return jnp.einsum('gy,aghp->agpyh', eye, cc.reshape(nb, gb, hh, p)).reshape(nb, gb * p, gb * hh)

    bblk = jnp.concatenate([pack_b(bbr), pack_b(bbi)], axis=-1).astype(BF16)
    cblk = jnp.concatenate([pack_c(c_re), pack_c(-c_im)], axis=1).astype(BF16)
    return ab_re.reshape(1, S5_NSTATE), ab_im.reshape(1, S5_NSTATE), bblk, cblk


S5_SLAB = 256


def _s5_kernel(u_hbm, bblk_ref, cblk_ref, are_ref, aim_ref, d_ref, gw_ref, gb_ref, y_hbm,
               ubuf, ybuf, in_sem, out_sem, bur_ref, bui_ref, xr_ref, xi_ref, sr_ref, si_ref, *, nb, tc):
    t = pl.program_id(0)
    nt = pl.num_programs(0)
    slot = t % 2
    half = S5_GB * S5_STATE
    wu = S5_GB * S5_GROUP

    def in_copy(step, sl, b):
        return pltpu.make_async_copy(u_hbm.at[b, pl.ds(step * tc, tc), :], ubuf.at[sl, :, b, :], in_sem.at[sl, b])

    def out_copy(step, sl, b):
        return pltpu.make_async_copy(ybuf.at[sl, :, b, :], y_hbm.at[b, pl.ds(step * tc, tc), :], out_sem.at[sl, b])

    @pl.when(t == 0)
    def _():
        for b in range(nb):
            in_copy(0, 0, b).start()
        sr_ref[...] = jnp.zeros_like(sr_ref)
        si_ref[...] = jnp.zeros_like(si_ref)

    @pl.when(t + 1 < nt)
    def _():
        for b in range(nb):
            in_copy(t + 1, 1 - slot, b).start()

    for b in range(nb):
        in_copy(t, slot, b).wait()

    u = ubuf[slot].reshape(tc * nb, S5_WIDTH)
    ub = u.astype(BF16)
    for a in range(S5_NBLK):
        bu = jnp.dot(ub[:, a * wu:(a + 1) * wu], bblk_ref[a], preferred_element_type=F32)
        bur_ref[:, a * half:(a + 1) * half] = bu[:, :half]
        bui_ref[:, a * half:(a + 1) * half] = bu[:, half:]

    for sl in range(S5_NSTATE // S5_SLAB):
        cs = slice(sl * S5_SLAB, (sl + 1) * S5_SLAB)
        ar = jnp.broadcast_to(are_ref[:, cs], (nb, S5_SLAB))
        ai = jnp.broadcast_to(aim_ref[:, cs], (nb, S5_SLAB))

        def step(k, carry, cs=cs, ar=ar, ai=ai):
            xr, xi = carry
            r0 = pl.multiple_of(k * nb, nb)
            nr = ar * xr - ai * xi + bur_ref[pl.ds(r0, nb), cs]
            ni = ar * xi + ai * xr + bui_ref[pl.ds(r0, nb), cs]
            xr_ref[pl.ds(r0, nb), cs] = nr.astype(BF16)
            xi_ref[pl.ds(r0, nb), cs] = ni.astype(BF16)
            return nr, ni

        xr, xi = lax.fori_loop(0, tc, step, (sr_ref[:, cs], si_ref[:, cs]), unroll=4)
        sr_ref[:, cs] = xr
        si_ref[:, cs] = xi

    ys = []
    for a in range(S5_NBLK):
        ya = jnp.dot(xr_ref[:, a * half:(a + 1) * half], cblk_ref[a, :half, :], preferred_element_type=F32)
        ya = ya + jnp.dot(xi_ref[:, a * half:(a + 1) * half], cblk_ref[a, half:, :], preferred_element_type=F32)
        ys.append(ya)
    y = jnp.concatenate(ys, axis=-1) + d_ref[...] * u
    y = jax.nn.gelu(y)
    z = jnp.dot(y.astype(BF16), gw_ref[...], preferred_element_type=F32) + gb_ref[...]
    y = y * jax.nn.sigmoid(z)

    @pl.when(t >= 2)
    def _():
        for b in range(nb):
            out_copy(t - 2, slot, b).wait()

    ybuf[slot] = y.reshape(tc, nb, S5_WIDTH)
    for b in range(nb):
        out_copy(t, slot, b).start()

    @pl.when(t == nt - 1)
    def _():
        for b in range(nb):
            out_copy(t, slot, b).wait()

        @pl.when(nt >= 2)
        def _():
            for b in range(nb):
                out_copy(t - 1, 1 - slot, b).wait()


def _s5(u3, bblk, cblk, a_re, a_im, d_skip, glu_w, glu_b, tc):
    nb, seq, _ = u3.shape
    rows = nb * tc
    return pl.pallas_call(
        functools.partial(_s5_kernel, nb=nb, tc=tc),
        grid=(seq // tc,),
        in_specs=[
            pl.BlockSpec(memory_space=pl.ANY),
            _const_spec(bblk.shape),
            _const_spec(cblk.shape),
            _const_spec((1, S5_NSTATE)),
            _const_spec((1, S5_NSTATE)),
            _const_spec((1, S5_WIDTH)),
            _const_spec((S5_WIDTH, S5_WIDTH)),
            _const_spec((1, S5_WIDTH)),
        ],
        out_specs=pl.BlockSpec(memory_space=pl.ANY),
        out_shape=jax.ShapeDtypeStruct((nb, seq, S5_WIDTH), F32),
        scratch_shapes=[
            pltpu.VMEM((2, tc, nb, S5_WIDTH), F32),
            pltpu.VMEM((2, tc, nb, S5_WIDTH), F32),
            pltpu.SemaphoreType.DMA((2, nb)),
            pltpu.SemaphoreType.DMA((2, nb)),
            pltpu.VMEM((rows, S5_NSTATE), F32),
            pltpu.VMEM((rows, S5_NSTATE), F32),
            pltpu.VMEM((rows, S5_NSTATE), BF16),
            pltpu.VMEM((rows, S5_NSTATE), BF16),
            pltpu.VMEM((nb, S5_NSTATE), F32),
            pltpu.VMEM((nb, S5_NSTATE), F32),
        ],
        compiler_params=_params(("arbitrary",)),
        name="s5_scan",
    )(u3, bblk, cblk, a_re, a_im, d_skip, glu_w, glu_b)


def _retention_consts():
    hs = np.arange(RET_HEADS, dtype=np.float32)
    gamma_log = np.log((1.0 - np.exp2(-5.0 - hs)).astype(np.float32)).astype(np.float32)
    c = RET_CHUNK
    idx = np.arange(c, dtype=np.float32)
    diff = idx[:, None] - idx[None, :]
    dmask = np.where(diff[None] >= 0, np.exp(np.maximum(diff[None], 0.0) * gamma_log[:, None, None]), 0.0)
    xi = np.exp((idx[None, :] + 1.0) * gamma_log[:, None])
    zeta = np.exp((c - 1.0 - idx[None, :]) * gamma_log[:, None])
    g_chunk = np.exp(c * gamma_log)
    xi_b = np.broadcast_to(xi[:, :, None], (RET_HEADS, c, RET_HEAD_DIM))
    zeta_b = np.broadcast_to(zeta[:, :, None], (RET_HEADS, c, RET_HEAD_DIM))
    return (dmask.astype(np.float32), np.ascontiguousarray(xi_b, np.float32),
            np.ascontiguousarray(zeta_b, np.float32), [float(v) for v in g_chunk.astype(np.float32)])


def _retention_kernel(q_ref, k_ref, v_ref, sg_ref, dm_ref, xi_ref, zeta_ref, rn_ref, y_ref, r_ref, *, g_chunk):
    c = pl.program_id(1)

    @pl.when(c == 0)
    def _():
        r_ref[...] = jnp.zeros_like(r_ref)

    for h in range(RET_HEADS):
        hs = slice(h * RET_HEAD_DIM, (h + 1) * RET_HEAD_DIM)
        q = q_ref[:, hs]
        k = k_ref[:, hs]
        v = v_ref[:, hs]
        r = r_ref[h]
        s = lax.dot_general(q, k, (((1,), (1,)), ((), ())), preferred_element_type=F32)
        inner = (s * dm_ref[h]).astype(BF16)
        o = jnp.dot(inner, v, preferred_element_type=F32)
        o = o + xi_ref[h] * jnp.dot(q, r.astype(BF16), preferred_element_type=F32)
        kz = (k.astype(F32) * zeta_ref[h]).astype(BF16)
        r_ref[h] = r * g_chunk[h] + lax.dot_general(kz, v, (((0,), (0,)), ((), ())), preferred_element_type=F32)
        o = o * lax.rsqrt(jnp.mean(o * o, axis=-1, keepdims=True) + EPS)
        o = o * rn_ref[:, hs]
        y_ref[:, hs] = (sg_ref[:, hs].astype(F32) * o).astype(BF16)


def _retention(proj, ret_norm, nb, seq):
    dmask, xi_b, zeta_b, g_chunk = _retention_consts()
    nc = seq // RET_CHUNK
    blk = (RET_CHUNK, RET_WIDTH)
    return pl.pallas_call(
        functools.partial(_retention_kernel, g_chunk=g_chunk),
        grid=(nb, nc),
        in_specs=[
            pl.BlockSpec(blk, lambda b, c: (b * nc + c, 0)),
            pl.BlockSpec(blk, lambda b, c: (b * nc + c, 1)),
            pl.BlockSpec(blk, lambda b, c: (b * nc + c, 2)),
            pl.BlockSpec(blk, lambda b, c: (b * nc + c, 3)),
            _const_spec(dmask.shape),
            _const_spec(xi_b.shape),
            _const_spec(zeta_b.shape),
            _const_spec((1, RET_WIDTH)),
        ],
        out_specs=pl.BlockSpec(blk, lambda b, c: (b * nc + c, 0)),
        out_shape=jax.ShapeDtypeStruct((nb * seq, RET_WIDTH), BF16),
        scratch_shapes=[pltpu.VMEM((RET_HEADS, RET_HEAD_DIM, RET_HEAD_DIM), F32)],
        compiler_params=_params(("parallel", "arbitrary")),
        name="retention",
    )(proj, proj, proj, proj, jnp.asarray(dmask), jnp.asarray(xi_b), jnp.asarray(zeta_b), ret_norm)


def _outproj_kernel(*refs, n_in):
    res_ref = refs[0]
    ys = refs[1:1 + n_in]
    w_ref = refs[1 + n_in]
    o_ref = refs[2 + n_in]
    acc = res_ref[...]
    k0 = 0
    for y in ys:
        kk = y.shape[1]
        acc = acc + jnp.dot(y[...].astype(BF16), w_ref[k0:k0 + kk, :], preferred_element_type=F32)
        k0 += kk
    o_ref[...] = acc


def _outproj(res, ys, w, tm, name):
    rows, d = res.shape
    row = lambda i: (i, 0)
    return pl.pallas_call(
        functools.partial(_outproj_kernel, n_in=len(ys)),
        grid=(rows // tm,),
        in_specs=([pl.BlockSpec((tm, d), row)] + [pl.BlockSpec((tm, y.shape[1]), row) for y in ys]
                  + [_const_spec(w.shape)]),
        out_specs=pl.BlockSpec((tm, d), row),
        out_shape=jax.ShapeDtypeStruct((rows, d), F32),
        compiler_params=_params(("parallel",)),
        name=name,
    )(res, *ys, w)


def _ffn_kernel(h_ref, halo_ref, ln_ref, wg_ref, wv_ref, cwg_ref, cwv_ref, cbg_ref, cbv_ref, wd_ref, lnf_ref,
                o_ref, hn_ref, ag_ref, av_ref, *, nt, tm, nj, final_norm):
    i = pl.program_id(0)
    s = pl.program_id(1)
    hl = BF16_ROWS
    slot = s % 2

    def up(sl):
        hn = hn_ref[...]
        ag_ref[sl] = jnp.dot(hn, wg_ref[...], preferred_element_type=F32)
        av_ref[sl] = jnp.dot(hn, wv_ref[...], preferred_element_type=F32)

    def down(sl):
        ts = tm // FFN_DOWN_SUBTILES
        for sub in range(FFN_DOWN_SUBTILES):
            r0 = sub * ts

            def conv(a_ref, cw_ref, cb_ref, r0=r0):
                cw = cw_ref[...]
                return (a_ref[sl, pl.ds(r0 + hl - 2, ts), :] * cw[0:1] + a_ref[sl, pl.ds(r0 + hl - 1, ts), :] * cw[1:2]
                        + a_ref[sl, pl.ds(r0 + hl, ts), :] * cw[2:3] + cb_ref[...])

            gate = conv(ag_ref, cwg_ref, cbg_ref)
            val = conv(av_ref, cwv_ref, cbv_ref)
            g = (gate * jax.nn.sigmoid(gate) * val).astype(BF16)
            o_ref[r0:r0 + ts, :] += jnp.dot(g, wd_ref[...], preferred_element_type=F32)

    @pl.when(s == 0)
    def _():
        hn_ref[hl:, :] = _rms(h_ref[...], ln_ref[...]).astype(BF16)
        halo = _rms(halo_ref[...], ln_ref[...])
        hn_ref[:hl, :] = jnp.where(i % nt > 0, halo, 0.0).astype(BF16)
        o_ref[...] = h_ref[...]
        up(0)

    @pl.when(jnp.logical_and(s > 0, s < nj))
    def _():
        up(slot)
        down(1 - slot)

    @pl.when(s == nj)
    def _():
        down(1 - slot)
        if final_norm:
            o_ref[...] = _rms(o_ref[...], lnf_ref[...])


def _ffn(h, ln, w_up, conv_w, conv_b, w_down, ln_final, layer, seq, tm, tf, final_norm):
    rows = h.shape[0]
    hl = BF16_ROWS
    nj = D_FF // tf
    nt = seq // tm
    up_j = lambda s: jnp.minimum(s, nj - 1)
    dn_j = lambda s: jnp.maximum(s - 1, 0)
    return pl.pallas_call(
        functools.partial(_ffn_kernel, nt=nt, tm=tm, nj=nj, final_norm=final_norm),
        grid=(rows // tm, nj + 1),
        in_specs=[
            pl.BlockSpec((tm, D_MODEL), lambda i, s: (i, 0), pipeline_mode=pl.Buffered(1)),
            pl.BlockSpec((hl, D_MODEL), lambda i, s: (jnp.maximum(i * (tm // hl) - 1, 0), 0)),
            pl.BlockSpec((1, D_MODEL), lambda i, s: (0, 0)),
            pl.BlockSpec((None, D_MODEL, tf), lambda i, s: (layer, 0, up_j(s))),
            pl.BlockSpec((None, D_MODEL, tf), lambda i, s: (layer, 0, nj + up_j(s))),
            pl.BlockSpec((3, tf), lambda i, s: (0, dn_j(s))),
            pl.BlockSpec((3, tf), lambda i, s: (0, nj + dn_j(s))),
            pl.BlockSpec((1, tf), lambda i, s: (0, dn_j(s))),
            pl.BlockSpec((1, tf), lambda i, s: (0, nj + dn_j(s))),
            pl.BlockSpec((None, tf, D_MODEL), lambda i, s: (layer, dn_j(s), 0)),
            pl.BlockSpec((1, D_MODEL), lambda i, s: (0, 0)),
        ],
        out_specs=pl.BlockSpec((tm, D_MODEL), lambda i, s: (i, 0)),
        out_shape=jax.ShapeDtypeStruct((rows, D_MODEL), F32),
        scratch_shapes=[
            pltpu.VMEM((hl + tm, D_MODEL), BF16),
            pltpu.VMEM((2, hl + tm, tf), F32),
            pltpu.VMEM((2, hl + tm, tf), F32),
        ],
        compiler_params=_params(("parallel", "arbitrary")),
        name="conv_ffn",
    )(h, h, ln, w_up, w_up, conv_w, conv_w, conv_b, conv_b, w_down, ln_final)


MLA_DOWN_PAD = MLA_Q_RANK + MLA_KV_RANK + 2 * LANE
MLA_QN = MLA_HEADS * MLA_NOPE
MLA_QR = MLA_HEADS * MLA_ROPE
MLA_KV = MLA_HEADS * (MLA_NOPE + MLA_V)
MLA_CHUNK = 1024
LOG2E = 1.4426950408889634


def _mla_proj_kernel(h_ref, ln_ref, wd_ref, qn_ref, kvn_ref, wq_ref, wkv_ref, cos_ref, sin_ref,
                     oqn_ref, oqr_ref, okv_ref, okr_ref):
    hn = _rms(h_ref[...], ln_ref[...]).astype(BF16)
    c = jnp.dot(hn, wd_ref[...], preferred_element_type=F32)
    cos = cos_ref[...]
    sin = sin_ref[...]
    scale = (MLA_NOPE + MLA_ROPE) ** -0.5 * LOG2E

    cq = _rms(c[:, :MLA_Q_RANK], qn_ref[...]).astype(BF16)
    for n0 in range(0, MLA_QN, MLA_CHUNK):
        qn = jnp.dot(cq, wq_ref[:, n0:n0 + MLA_CHUNK], preferred_element_type=F32)
        oqn_ref[:, n0:n0 + MLA_CHUNK] = (qn * scale).astype(BF16)
    cs = cos * scale
    ss = sin * scale
    for n0 in range(0, MLA_QR, MLA_CHUNK):
        qr = jnp.dot(cq, wq_ref[:, MLA_QN + n0:MLA_QN + n0 + MLA_CHUNK], preferred_element_type=F32)
        qp = jnp.dot(cq, wq_ref[:, MLA_QN + MLA_QR + n0:MLA_QN + MLA_QR + n0 + MLA_CHUNK],
                     preferred_element_type=F32)
        for m in range(MLA_CHUNK // LANE):
            ls = slice(m * LANE, (m + 1) * LANE)
            oqr_ref[:, n0 + m * LANE:n0 + (m + 1) * LANE] = (qr[:, ls] * cs + qp[:, ls] * ss).astype(BF16)

    ckv = _rms(c[:, MLA_Q_RANK:MLA_Q_RANK + MLA_KV_RANK], kvn_ref[...]).astype(BF16)
    for n0 in range(0, MLA_KV, MLA_CHUNK):
        okv_ref[:, n0:n0 + MLA_CHUNK] = jnp.dot(
            ckv, wkv_ref[:, n0:n0 + MLA_CHUNK], preferred_element_type=F32).astype(BF16)

    o = MLA_Q_RANK + MLA_KV_RANK
    kr2 = c[:, o:o + LANE] * cos + c[:, o + LANE:o + 2 * LANE] * sin
    left = lax.broadcasted_iota(jnp.int32, kr2.shape, 1) < MLA_ROPE
    okr_ref[:, :LANE] = jnp.where(left, kr2, 0.0).astype(BF16)
    okr_ref[:, LANE:] = jnp.where(left, 0.0, kr2).astype(BF16)


def _mla_proj(h, ln, wd, q_norm, kv_norm, wq, wkv, cos4, sin4, seq, tm):
    rows = h.shape[0]
    nt = seq // tm
    row = lambda i: (i, 0)
    return pl.pallas_call(
        _mla_proj_kernel,
        grid=(rows // tm,),
        in_specs=[
            pl.BlockSpec((tm, D_MODEL), row),
            _const_spec((1, D_MODEL)),
            _const_spec(wd.shape),
            _const_spec((1, MLA_Q_RANK)),
            _const_spec((1, MLA_KV_RANK)),
            _const_spec(wq.shape),
            _const_spec(wkv.shape),
            pl.BlockSpec((tm, LANE), lambda i: (i % nt, 0)),
            pl.BlockSpec((tm, LANE), lambda i: (i % nt, 0)),
        ],
        out_specs=[
            pl.BlockSpec((tm, MLA_QN), row),
            pl.BlockSpec((tm, MLA_QR), row),
            pl.BlockSpec((tm, MLA_KV), row),
            pl.BlockSpec((tm, 2 * LANE), row),
        ],
        out_shape=[
            jax.ShapeDtypeStruct((rows, MLA_QN), BF16),
            jax.ShapeDtypeStruct((rows, MLA_QR), BF16),
            jax.ShapeDtypeStruct((rows, MLA_KV), BF16),
            jax.ShapeDtypeStruct((rows, 2 * LANE), BF16),
        ],
        compiler_params=_params(("parallel",)),
        name="mla_proj",
    )(h, ln, wd, q_norm, kv_norm, wq, wkv, cos4, sin4)


def _attn_kernel(qn_ref, qr_ref, kv_ref, kr_ref, o_ref, *, tq, nq):
    row = lax.broadcasted_iota(jnp.int32, (tq, tq), 0)
    col = lax.broadcasted_iota(jnp.int32, (tq, tq), 1)
    m = [[jnp.full((tq, 1), -1e30, F32) for _ in range(nq)] for _ in range(2)]
    l = [[jnp.zeros((tq, 1), F32) for _ in range(nq)] for _ in range(2)]
    acc = [[jnp.zeros((tq, MLA_V), F32) for _ in range(nq)] for _ in range(2)]

    for kb in range(nq):
        r0 = kb * tq
        for s in range(2):
            k = jnp.concatenate([kv_ref[r0:r0 + tq, 2 * s * LANE:(2 * s + 1) * LANE],
                                 kr_ref[r0:r0 + tq, s * LANE:(s + 1) * LANE]], axis=-1)
            v = kv_ref[r0:r0 + tq, (2 * s + 1) * LANE:(2 * s + 2) * LANE]
            q = jnp.concatenate([qn_ref[r0:, s * LANE:(s + 1) * LANE], qr_ref[r0:, :]], axis=-1)
            sc_all = lax.dot_general(q, k, (((1,), (1,)), ((), ())), preferred_element_type=F32)
            ps = []
            for qb in range(kb, nq):
                sc = sc_all[(qb - kb) * tq:(qb - kb + 1) * tq]
                if qb == kb:
                    sc = jnp.where(row >= col, sc, -1e30)
                m_new = jnp.maximum(m[s][qb], jnp.max(sc, axis=-1, keepdims=True))
                alpha = jnp.exp2(m[s][qb] - m_new)
                p = jnp.exp2(sc - m_new)
                l[s][qb] = alpha * l[s][qb] + jnp.sum(p, axis=-1, keepdims=True)
                acc[s][qb] = alpha * acc[s][qb]
                m[s][qb] = m_new
                ps.append(p.astype(BF16))
            pv = jnp.dot(jnp.concatenate(ps, axis=0), v, preferred_element_type=F32)
            for qb in range(kb, nq):
                acc[s][qb] = acc[s][qb] + pv[(qb - kb) * tq:(qb - kb + 1) * tq]
            o_ref[r0:r0 + tq, s * LANE:(s + 1) * LANE] = (acc[s][kb] / l[s][kb]).astype(BF16)


def _attention(qn, qr, kv, krp, nb, seq, tq):
    npair = MLA_HEADS // 2
    nq = seq // tq
    return pl.pallas_call(
        functools.partial(_attn_kernel, tq=tq, nq=nq),
        grid=(nb, npair),
        in_specs=[
            pl.BlockSpec((seq, 2 * LANE), lambda b, h: (b, h)),
            pl.BlockSpec((seq, LANE), lambda b, h: (b, h)),
            pl.BlockSpec((seq, 4 * LANE), lambda b, h: (b, h)),
            pl.BlockSpec((seq, 2 * LANE), lambda b, h: (b, 0)),
        ],
        out_specs=pl.BlockSpec((seq, 2 * LANE), lambda b, h: (b, h)),
        out_shape=jax.ShapeDtypeStruct((nb * seq, MLA_HEADS * MLA_V), BF16),
        compiler_params=_params(("parallel", "parallel")),
        name="mla_attention",
    )(qn, qr, kv, krp)


def _rope_tables(n, dim):
    inv = ROPE_BASE ** (-jnp.arange(0, dim, 2, dtype=F32) / dim)
    ang = jnp.arange(n, dtype=F32)[:, None] * inv[None, :]
    return jnp.cos(ang), jnp.sin(ang)


def _rot_cols(w):
    lead = w.shape[:-1]
    w = w.reshape(lead + (-1, 2, MLA_ROPE // 2))
    return jnp.stack([-w[..., 1, :], w[..., 0, :]], axis=-2).reshape(lead + (-1,))


def kernel(x, ln_mix_even, w_in_even, s5_lambda_re, s5_lambda_im, s5_log_dt, s5_b_re, s5_b_im, s5_c_re, s5_c_im, s5_d, s5_glu_w, s5_glu_b, ret_norm, w_out_even, ln_mix_odd, mla_w_down, mla_q_norm, mla_w_uq, mla_kv_norm, mla_w_ukv, w_out_odd, ln_ffn, ffn_w_up, ffn_conv_w, ffn_conv_b, ffn_w_down, ln_final):
    nb, seq, d = x.shape
    assert d == D_MODEL and nb % 8 == 0 and seq % 128 == 0
    rows = nb * seq
    tm = min(seq, 1024)
    tmo = min(seq, 512)
    tf = 512
    row2 = lambda a: a.reshape(1, -1)

    x2d = x.reshape(rows, d)

    ret_cos, ret_sin = _rope_tables(seq, RET_HEAD_DIM)
    u, proj = _inproj(x2d, row2(ln_mix_even[0]), w_in_even[0].astype(BF16), ret_cos, ret_sin, seq, tm)

    ab_re, ab_im, bbr, bbi = _s5_discretise(s5_lambda_re[0], s5_lambda_im[0], s5_log_dt[0], s5_b_re[0], s5_b_im[0])
    a_re, a_im, bblk, cblk = _s5_pack(ab_re, ab_im, bbr, bbi, s5_c_re[0], s5_c_im[0])
    y_s5 = _s5(u.reshape(nb, seq, S5_WIDTH), bblk, cblk, a_re, a_im, row2(s5_d[0]),
               s5_glu_w[0].astype(BF16), row2(s5_glu_b[0]), tc=min(seq, 32))
    y_ret = _retention(proj, row2(ret_norm[0]), nb, seq)

    h = _outproj(x2d, [y_s5.reshape(rows, S5_WIDTH), y_ret], w_out_even[0].astype(BF16), tmo, "even_outproj")

    w_up_all = ffn_w_up.astype(BF16)
    w_down_all = ffn_w_down.astype(BF16)

    def ffn(h, layer, final_norm):
        return _ffn(h, row2(ln_ffn[layer]), w_up_all, ffn_conv_w[layer], row2(ffn_conv_b[layer]), w_down_all,
                    row2(ln_final), layer, seq, tm, tf, final_norm)

    h = ffn(h, 0, False)

    mla_cos, mla_sin = _rope_tables(seq, MLA_ROPE)
    cos4 = jnp.tile(mla_cos, (1, 4))
    sin4 = jnp.tile(mla_sin, (1, 4))
    wdn = mla_w_down[0]
    w_kr = wdn[:, MLA_Q_RANK + MLA_KV_RANK:]
    w_krp = _rot_cols(w_kr)
    wd = jnp.concatenate([wdn[:, :MLA_Q_RANK + MLA_KV_RANK], w_kr, w_kr, w_krp, w_krp], axis=1).astype(BF16)
    wq3 = mla_w_uq[0].reshape(MLA_Q_RANK, MLA_HEADS, MLA_NOPE + MLA_ROPE)
    wq_n = wq3[:, :, :MLA_NOPE].reshape(MLA_Q_RANK, MLA_QN)
    wq_r = wq3[:, :, MLA_NOPE:].reshape(MLA_Q_RANK, MLA_QR)
    wq = jnp.concatenate([wq_n, wq_r, _rot_cols(wq_r)], axis=1).astype(BF16)
    qn, qr, kv, krp = _mla_proj(h, row2(ln_mix_odd[0]), wd, row2(mla_q_norm[0]), row2(mla_kv_norm[0]),
                                wq, mla_w_ukv[0].astype(BF16), cos4, sin4, seq, tmo)
    o = _attention(qn, qr, kv, krp, nb, seq, tq=min(seq, 512))
    h = _outproj(h, [o], w_out_odd[0].astype(BF16), tmo, "odd_outproj")
    h = ffn(h, 1, True)
    return h.reshape(nb, seq, d)
```

```python
import functools

import numpy as np
import jax
import jax.numpy as jnp
from jax import lax
from jax.experimental import pallas as pl
from jax.experimental.pallas import tpu as pltpu

F32 = jnp.float32
BF16 = jnp.bfloat16

EPS = 1e-6
ROPE_BASE = 10000.0
D_MODEL = 2048
S5_WIDTH = 512
S5_GROUP = 16
S5_GROUPS = 32
S5_STATE = 64
S5_GB = 8
S5_NBLK = S5_GROUPS // S5_GB
S5_NSTATE = S5_GROUPS * S5_STATE
RET_HEADS = 6
RET_HEAD_DIM = 256
RET_WIDTH = RET_HEADS * RET_HEAD_DIM
RET_CHUNK = 256
RET_STEP_CHUNKS = 4
EVEN_IN = S5_WIDTH + 4 * RET_WIDTH
MLA_HEADS = 16
MLA_NOPE = 128
MLA_ROPE = 64
MLA_V = 128
MLA_Q_RANK = 512
MLA_KV_RANK = 512
D_FF = 5632
FFN_DOWN_SUBTILES = 4

LANE = 128
BF16_ROWS = 16
VMEM_LIMIT = 60 * 1024 * 1024


def _params(sem, vmem=VMEM_LIMIT, flags=None):
    return pltpu.CompilerParams(dimension_semantics=sem, vmem_limit_bytes=vmem, flags=flags)


def _rms(x, g):
    return x * lax.rsqrt(jnp.mean(x * x, axis=-1, keepdims=True) + EPS) * g


def _const_spec(shape):
    nd = len(shape)
    return pl.BlockSpec(shape, lambda *_: (0,) * nd, pipeline_mode=pl.Buffered(1))


IN_TN = RET_WIDTH
IN_NJ = 4
IN_CHUNK = 512
IN_SUBTILES = 2


def _inproj_kernel(h_ref, ln_ref, wu_ref, w0_ref, w1_ref, w2_ref, cos_ref, sin_ref, u_ref, p_ref, hn_ref):
    j = pl.program_id(1)
    ts = hn_ref.shape[0] // IN_SUBTILES
    w_ref = (w0_ref, w1_ref, w2_ref)

    def pieces(wrefs):
        for c, wref in enumerate(wrefs):
            for sub in range(IN_SUBTILES):
                rs = slice(sub * ts, (sub + 1) * ts)
                cs = slice(c * IN_CHUNK, (c + 1) * IN_CHUNK)
                yield rs, cs, jnp.dot(hn_ref[rs, :], wref[...], preferred_element_type=F32)

    @pl.when(j == 0)
    def _():
        hn_ref[...] = _rms(h_ref[...], ln_ref[...]).astype(BF16)
        for rs, cs, acc in pieces((wu_ref,)):
            u_ref[rs, cs] = acc

    @pl.when(j <= 1)
    def _():
        sc = jnp.where(j == 1, RET_HEAD_DIM ** -0.5, 1.0).astype(F32)
        for rs, cs, acc in pieces(w_ref):
            c = cos_ref[rs, :] * sc
            s = sin_ref[rs, :] * sc
            for hh in range(IN_CHUNK // RET_HEAD_DIM):
                o = hh * RET_HEAD_DIM
                x1 = acc[:, o:o + LANE]
                x2 = acc[:, o + LANE:o + 2 * LANE]
                p_ref[rs, cs.start + o:cs.start + o + LANE] = (x1 * c - x2 * s).astype(BF16)
                p_ref[rs, cs.start + o + LANE:cs.start + o + 2 * LANE] = (x2 * c + x1 * s).astype(BF16)

    @pl.when(j == 2)
    def _():
        for rs, cs, acc in pieces(w_ref):
            p_ref[rs, cs] = acc.astype(BF16)

    @pl.when(j == 3)
    def _():
        for rs, cs, acc in pieces(w_ref):
            p_ref[rs, cs] = (acc * jax.nn.sigmoid(acc)).astype(BF16)


def _inproj(x2d, ln, w, cos, sin, seq, tm):
    rows = x2d.shape[0]
    nt = seq // tm
    nc = IN_TN // IN_CHUNK
    wblk = lambda c: pl.BlockSpec((D_MODEL, IN_CHUNK), lambda i, j: (0, 1 + nc * j + c))
    return pl.pallas_call(
        _inproj_kernel,
        grid=(rows // tm, IN_NJ),
        in_specs=[
            pl.BlockSpec((tm, D_MODEL), lambda i, j: (i, 0)),
            pl.BlockSpec((1, D_MODEL), lambda i, j: (0, 0)),
            pl.BlockSpec((D_MODEL, S5_WIDTH), lambda i, j: (0, 0), pipeline_mode=pl.Buffered(1)),
            wblk(0), wblk(1), wblk(2),
            pl.BlockSpec((tm, LANE), lambda i, j: (i % nt, 0)),
            pl.BlockSpec((tm, LANE), lambda i, j: (i % nt, 0)),
        ],
        out_specs=[
            pl.BlockSpec((tm, S5_WIDTH), lambda i, j: (i, 0)),
            pl.BlockSpec((tm, IN_TN), lambda i, j: (i, j)),
        ],
        out_shape=[
            jax.ShapeDtypeStruct((rows, S5_WIDTH), F32),
            jax.ShapeDtypeStruct((rows, EVEN_IN - S5_WIDTH), BF16),
        ],
        scratch_shapes=[pltpu.VMEM((tm, D_MODEL), BF16)],
        compiler_params=_params(("parallel", "arbitrary")),
        name="even_inproj",
    )(x2d, ln, w, w, w, w, cos, sin)


def _s5_disc_kernel(lr_ref, li_ref, ldt_ref, br_ref, bi_ref, are_ref, aim_ref, bbr_ref, bbi_ref):
    lr = lr_ref[...]
    li = li_ref[...]
    dt = jnp.exp(ldt_ref[...])
    mag = jnp.exp(lr * dt)
    ab_re = mag * jnp.cos(li * dt)
    ab_im = mag * jnp.sin(li * dt)
    den = lr * lr + li * li
    nr = ab_re - 1.0
    ni = ab_im
    f_re = (nr * lr + ni * li) / den
    f_im = (ni * lr - nr * li) / den
    are_ref[...] = ab_re
    aim_ref[...] = ab_im
    br = br_ref[...]
    bi = bi_ref[...]
    bbr_ref[...] = f_re[:, None, :] * br - f_im[:, None, :] * bi
    bbi_ref[...] = f_re[:, None, :] * bi + f_im[:, None, :] * br


def _s5_discretise(lam_re, lam_im, log_dt, b_re, b_im):
    g, p, hh = S5_GROUPS, S5_STATE, S5_GROUP
    brt = jnp.swapaxes(b_re, 1, 2)
    bit = jnp.swapaxes(b_im, 1, 2)
    return pl.pallas_call(
        _s5_disc_kernel,
        out_shape=[
            jax.ShapeDtypeStruct((g, p), F32),
            jax.ShapeDtypeStruct((g, p), F32),
            jax.ShapeDtypeStruct((g, hh, p), F32),
            jax.ShapeDtypeStruct((g, hh, p), F32),
        ],
        name="s5_discretise",
    )(lam_re, lam_im, log_dt.reshape(g, 1), brt, bit)


def _s5_pack(ab_re, ab_im, bbr, bbi, c_re, c_im):
    eye = jnp.eye(S5_GB, dtype=F32)
    nb, gb, p, hh = S5_NBLK, S5_GB, S5_STATE, S5_GROUP

    def pack_b(bb):
        return jnp.einsum('gy,aghp->aghyp', eye, bb.reshape(nb, gb, hh, p)).reshape(nb, gb * hh, gb * p)

    def pack_c(cc):
        return jnp.einsum('gy,aghp->agpyh', eye, cc.reshape(nb, gb, hh, p)).reshape(nb, gb * p, gb * hh)

    bblk = jnp.concatenate([pack_b(bbr), pack_b(bbi)], axis=-1).astype(BF16)
    cblk = jnp.concatenate([pack_c(c_re), pack_c(-c_im)], axis=1).astype(BF16)
    return ab_re.reshape(1, S5_NSTATE), ab_im.reshape(1, S5_NSTATE), bblk, cblk


S5_SLAB = 256


def _s5_kernel(u_hbm, bblk_ref, cblk_ref, are_ref, aim_ref, d_ref, gw_ref, gb_ref, y_hbm,
               ubuf, ybuf, in_sem, out_sem, bur_ref, bui_ref, xr_ref, xi_ref, sr_ref, si_ref, *, nb, tc):
    t = pl.program_id(0)
    nt = pl.num_programs(0)
    slot = t % 2
    half = S5_GB * S5_STATE
    wu = S5_GB * S5_GROUP

    def in_copy(step, sl, b):
        return pltpu.make_async_copy(u_hbm.at[b, pl.ds(step * tc, tc), :], ubuf.at[sl, :, b, :], in_sem.at[sl, b])

    def out_copy(step, sl, b):
        return pltpu.make_async_copy(ybuf.at[sl, :, b, :], y_hbm.at[b, pl.ds(step * tc, tc), :], out_sem.at[sl, b])

    @pl.when(t == 0)
    def _():
        for b in range(nb):
            in_copy(0, 0, b).start()
        sr_ref[...] = jnp.zeros_like(sr_ref)
        si_ref[...] = jnp.zeros_like(si_ref)

    @pl.when(t + 1 < nt)
    def _():
        for b in range(nb):
            in_copy(t + 1, 1 - slot, b).start()

    for b in range(nb):
        in_copy(t, slot, b).wait()

    u = ubuf[slot].reshape(tc * nb, S5_WIDTH)
    ub = u.astype(BF16)
    for a in range(S5_NBLK):
        bu = jnp.dot(ub[:, a * wu:(a + 1) * wu], bblk_ref[a], preferred_element_type=F32)
        bur_ref[:, a * half:(a + 1) * half] = bu[:, :half]
        bui_ref[:, a * half:(a + 1) * half] = bu[:, half:]

    for sl in range(S5_NSTATE // S5_SLAB):
        cs = slice(sl * S5_SLAB, (sl + 1) * S5_SLAB)
        ar = jnp.broadcast_to(are_ref[:, cs], (nb, S5_SLAB))
        ai = jnp.broadcast_to(aim_ref[:, cs], (nb, S5_SLAB))

        def step(k, carry, cs=cs, ar=ar, ai=ai):
            xr, xi = carry
            r0 = pl.multiple_of(k * nb, nb)
            nr = ar * xr - ai * xi + bur_ref[pl.ds(r0, nb), cs]
            ni = ar * xi + ai * xr + bui_ref[pl.ds(r0, nb), cs]
            xr_ref[pl.ds(r0, nb), cs] = nr.astype(BF16)
            xi_ref[pl.ds(r0, nb), cs] = ni.astype(BF16)
            return nr, ni

        xr, xi = lax.fori_loop(0, tc, step, (sr_ref[:, cs], si_ref[:, cs]), unroll=4)
        sr_ref[:, cs] = xr
        si_ref[:, cs] = xi

    ys = []
    for a in range(S5_NBLK):
        ya = jnp.dot(xr_ref[:, a * half:(a + 1) * half], cblk_ref[a, :half, :], preferred_element_type=F32)
        ya = ya + jnp.dot(xi_ref[:, a * half:(a + 1) * half], cblk_ref[a, half:, :], preferred_element_type=F32)
        ys.append(ya)
    y = jnp.concatenate(ys, axis=-1) + d_ref[...] * u
    y = jax.nn.gelu(y)
    z = jnp.dot(y.astype(BF16), gw_ref[...], preferred_element_type=F32) + gb_ref[...]
    y = y * jax.nn.sigmoid(z)

    @pl.when(t >= 2)
    def _():
        for b in range(nb):
            out_copy(t - 2, slot, b).wait()

    ybuf[slot] = y.reshape(tc, nb, S5_WIDTH)
    for b in range(nb):
        out_copy(t, slot, b).start()

    @pl.when(t == nt - 1)
    def _():
        for b in range(nb):
            out_copy(t, slot, b).wait()

        @pl.when(nt >= 2)
        def _():
            for b in range(nb):
                out_copy(t - 1, 1 - slot, b).wait()


def _s5(u3, bblk, cblk, a_re, a_im, d_skip, glu_w, glu_b, tc):
    nb, seq, _ = u3.shape
    rows = nb * tc
    return pl.pallas_call(
        functools.partial(_s5_kernel, nb=nb, tc=tc),
        grid=(seq // tc,),
        in_specs=[
            pl.BlockSpec(memory_space=pl.ANY),
            _const_spec(bblk.shape),
            _const_spec(cblk.shape),
            _const_spec((1, S5_NSTATE)),
            _const_spec((1, S5_NSTATE)),
            _const_spec((1, S5_WIDTH)),
            _const_spec((S5_WIDTH, S5_WIDTH)),
            _const_spec((1, S5_WIDTH)),
        ],
        out_specs=pl.BlockSpec(memory_space=pl.ANY),
        out_shape=jax.ShapeDtypeStruct((nb, seq, S5_WIDTH), F32),
        scratch_shapes=[
            pltpu.VMEM((2, tc, nb, S5_WIDTH), F32),
            pltpu.VMEM((2, tc, nb, S5_WIDTH), F32),
            pltpu.SemaphoreType.DMA((2, nb)),
            pltpu.SemaphoreType.DMA((2, nb)),
            pltpu.VMEM((rows, S5_NSTATE), F32),
            pltpu.VMEM((rows, S5_NSTATE), F32),
            pltpu.VMEM((rows, S5_NSTATE), BF16),
            pltpu.VMEM((rows, S5_NSTATE), BF16),
            pltpu.VMEM((nb, S5_NSTATE), F32),
            pltpu.VMEM((nb, S5_NSTATE), F32),
        ],
        compiler_params=_params(("arbitrary",)),
        name="s5_scan",
    )(u3, bblk, cblk, a_re, a_im, d_skip, glu_w, glu_b)


def _retention_consts():
    hs = np.arange(RET_HEADS, dtype=np.float32)
    gamma_log = np.log((1.0 - np.exp2(-5.0 - hs)).astype(np.float32)).astype(np.float32)
    c = RET_CHUNK
    idx = np.arange(c, dtype=np.float32)
    diff = idx[:, None] - idx[None, :]
    dmask = np.where(diff[None] >= 0, np.exp(np.maximum(diff[None], 0.0) * gamma_log[:, None, None]), 0.0)
    xi = np.exp((idx[None, :] + 1.0) * gamma_log[:, None])
    zeta = np.exp((c - 1.0 - idx[None, :]) * gamma_log[:, None])
    g_chunk = np.exp(c * gamma_log)
    xi_b = np.broadcast_to(xi[:, :, None], (RET_HEADS, c, RET_HEAD_DIM))
    zeta_b = np.broadcast_to(zeta[:, :, None], (RET_HEADS, c, RET_HEAD_DIM))
    return (dmask.astype(np.float32), np.ascontiguousarray(xi_b, np.float32),
            np.ascontiguousarray(zeta_b, np.float32), [float(v) for v in g_chunk.astype(np.float32)])


def _retention_kernel(q_ref, k_ref, v_ref, sg_ref, dm_ref, xi_ref, zeta_ref, rn_ref, y_ref, r_ref, *, g_chunk, nsub):
    c = pl.program_id(1)

    @pl.when(c == 0)
    def _():
        r_ref[...] = jnp.zeros_like(r_ref)

    for h in range(RET_HEADS):
        hs = slice(h * RET_HEAD_DIM, (h + 1) * RET_HEAD_DIM)
        r = r_ref[h]
        for sc in range(nsub):
            rows = slice(sc * RET_CHUNK, (sc + 1) * RET_CHUNK)
            q = q_ref[rows, hs]
            k = k_ref[rows, hs]
            v = v_ref[rows, hs]
            s = lax.dot_general(q, k, (((1,), (1,)), ((), ())), preferred_element_type=F32)
            inner = (s * dm_ref[h]).astype(BF16)
            o = jnp.dot(inner, v, preferred_element_type=F32)
            o = o + xi_ref[h] * jnp.dot(q, r.astype(BF16), preferred_element_type=F32)
            kz = (k.astype(F32) * zeta_ref[h]).astype(BF16)
            r = r * g_chunk[h] + lax.dot_general(kz, v, (((0,), (0,)), ((), ())), preferred_element_type=F32)
            o = o * lax.rsqrt(jnp.mean(o * o, axis=-1, keepdims=True) + EPS)
            o = o * rn_ref[:, hs]
            y_ref[rows, hs] = (sg_ref[rows, hs].astype(F32) * o).astype(BF16)
        r_ref[h] = r


def _retention(proj, ret_norm, nb, seq):
    dmask, xi_b, zeta_b, g_chunk = _retention_consts()
    nsub = min(RET_STEP_CHUNKS, seq // RET_CHUNK)
    step_rows = RET_CHUNK * nsub
    nc = seq // step_rows
    blk = (step_rows, RET_WIDTH)
    return pl.pallas_call(
        functools.partial(_retention_kernel, g_chunk=g_chunk, nsub=nsub),
        grid=(nb, nc),
        in_specs=[
            pl.BlockSpec(blk, lambda b, c: (b * nc + c, 0)),
            pl.BlockSpec(blk, lambda b, c: (b * nc + c, 1)),
            pl.BlockSpec(blk, lambda b, c: (b * nc + c, 2)),
            pl.BlockSpec(blk, lambda b, c: (b * nc + c, 3)),
            _const_spec(dmask.shape),
            _const_spec(xi_b.shape),
            _const_spec(zeta_b.shape),
            _const_spec((1, RET_WIDTH)),
        ],
        out_specs=pl.BlockSpec(blk, lambda b, c: (b * nc + c, 0)),
        out_shape=jax.ShapeDtypeStruct((nb * seq, RET_WIDTH), BF16),
        scratch_shapes=[pltpu.VMEM((RET_HEADS, RET_HEAD_DIM, RET_HEAD_DIM), F32)],
        compiler_params=_params(("parallel", "arbitrary")),
        name="retention",
    )(proj, proj, proj, proj, jnp.asarray(dmask), jnp.asarray(xi_b), jnp.asarray(zeta_b), ret_norm)


def _outproj_kernel(*refs, n_in):
    res_ref = refs[0]
    ys = refs[1:1 + n_in]
    w_ref = refs[1 + n_in]
    o_ref = refs[2 + n_in]
    acc = res_ref[...]
    k0 = 0
    for y in ys:
        kk = y.shape[1]
        acc = acc + jnp.dot(y[...].astype(BF16), w_ref[k0:k0 + kk, :], preferred_element_type=F32)
        k0 += kk
    o_ref[...] = acc


def _outproj(res, ys, w, tm, name):
    rows, d = res.shape
    row = lambda i: (i, 0)
    return pl.pallas_call(
        functools.partial(_outproj_kernel, n_in=len(ys)),
        grid=(rows // tm,),
        in_specs=([pl.BlockSpec((tm, d), row)] + [pl.BlockSpec((tm, y.shape[1]), row) for y in ys]
                  + [_const_spec(w.shape)]),
        out_specs=pl.BlockSpec((tm, d), row),
        out_shape=jax.ShapeDtypeStruct((rows, d), F32),
        compiler_params=_params(("parallel",)),
        name=name,
    )(res, *ys, w)


def _ffn_kernel(h_ref, halo_ref, ln_ref, wg_ref, wv_ref, cwg_ref, cwv_ref, cbg_ref, cbv_ref, wd_ref, lnf_ref,
                o_ref, hn_ref, ag_ref, av_ref, *, nt, tm, nj, final_norm):
    i = pl.program_id(0)
    s = pl.program_id(1)
    hl = BF16_ROWS
    slot = s % 2

    def up(sl):
        hn = hn_ref[...]
        ag_ref[sl] = jnp.dot(hn, wg_ref[...], preferred_element_type=F32)
        av_ref[sl] = jnp.dot(hn, wv_ref[...], preferred_element_type=F32)

    def down(sl):
        ts = tm // FFN_DOWN_SUBTILES
        for sub in range(FFN_DOWN_SUBTILES):
            r0 = sub * ts

            def conv(a_ref, cw_ref, cb_ref, r0=r0):
                cw = cw_ref[...]
                return (a_ref[sl, pl.ds(r0 + hl - 2, ts), :] * cw[0:1] + a_ref[sl, pl.ds(r0 + hl - 1, ts), :] * cw[1:2]
                        + a_ref[sl, pl.ds(r0 + hl, ts), :] * cw[2:3] + cb_ref[...])

            gate = conv(ag_ref, cwg_ref, cbg_ref)
            val = conv(av_ref, cwv_ref, cbv_ref)
            g = (gate * jax.nn.sigmoid(gate) * val).astype(BF16)
            o_ref[r0:r0 + ts, :] += jnp.dot(g, wd_ref[...], preferred_element_type=F32)

    @pl.when(s == 0)
    def _():
        hn_ref[hl:, :] = _rms(h_ref[...], ln_ref[...]).astype(BF16)
        halo = _rms(halo_ref[...], ln_ref[...])
        hn_ref[:hl, :] = jnp.where(i % nt > 0, halo, 0.0).astype(BF16)
        o_ref[...] = h_ref[...]
        up(0)

    @pl.when(jnp.logical_and(s > 0, s < nj))
    def _():
        up(slot)
        down(1 - slot)

    @pl.when(s == nj)
    def _():
        down(1 - slot)
        if final_norm:
            o_ref[...] = _rms(o_ref[...], lnf_ref[...])


def _ffn(h, ln, w_up, conv_w, conv_b, w_down, ln_final, layer, seq, tm, tf, final_norm):
    rows = h.shape[0]
    hl = BF16_ROWS
    nj = D_FF // tf
    nt = seq // tm
    up_j = lambda s: jnp.minimum(s, nj - 1)
    dn_j = lambda s: jnp.maximum(s - 1, 0)
    return pl.pallas_call(
        functools.partial(_ffn_kernel, nt=nt, tm=tm, nj=nj, final_norm=final_norm),
        grid=(rows // tm, nj + 1),
        in_specs=[
            pl.BlockSpec((tm, D_MODEL), lambda i, s: (i, 0), pipeline_mode=pl.Buffered(1)),
            pl.BlockSpec((hl, D_MODEL), lambda i, s: (jnp.maximum(i * (tm // hl) - 1, 0), 0)),
            pl.BlockSpec((1, D_MODEL), lambda i, s: (0, 0)),
            pl.BlockSpec((None, D_MODEL, tf), lambda i, s: (layer, 0, up_j(s))),
            pl.BlockSpec((None, D_MODEL, tf), lambda i, s: (layer, 0, nj + up_j(s))),
            pl.BlockSpec((3, tf), lambda i, s: (0, dn_j(s))),
            pl.BlockSpec((3, tf), lambda i, s: (0, nj + dn_j(s))),
            pl.BlockSpec((1, tf), lambda i, s: (0, dn_j(s))),
            pl.BlockSpec((1, tf), lambda i, s: (0, nj + dn_j(s))),
            pl.BlockSpec((None, tf, D_MODEL), lambda i, s: (layer, dn_j(s), 0)),
            pl.BlockSpec((1, D_MODEL), lambda i, s: (0, 0)),
        ],
        out_specs=pl.BlockSpec((tm, D_MODEL), lambda i, s: (i, 0)),
        out_shape=jax.ShapeDtypeStruct((rows, D_MODEL), F32),
        scratch_shapes=[
            pltpu.VMEM((hl + tm, D_MODEL), BF16),
            pltpu.VMEM((2, hl + tm, tf), F32),
            pltpu.VMEM((2, hl + tm, tf), F32),
        ],
        compiler_params=_params(("parallel", "arbitrary")),
        name="conv_ffn",
    )(h, h, ln, w_up, w_up, conv_w, conv_w, conv_b, conv_b, w_down, ln_final)


MLA_DOWN_PAD = MLA_Q_RANK + MLA_KV_RANK + 2 * LANE
MLA_QN = MLA_HEADS * MLA_NOPE
MLA_QR = MLA_HEADS * MLA_ROPE
MLA_KV = MLA_HEADS * (MLA_NOPE + MLA_V)
MLA_CHUNK = 1024
LOG2E = 1.4426950408889634


def _mla_proj_kernel(h_ref, ln_ref, wd_ref, qn_ref, kvn_ref, wq_ref, wkv_ref, cos_ref, sin_ref,
                     oqn_ref, oqr_ref, okv_ref, okr_ref):
    hn = _rms(h_ref[...], ln_ref[...]).astype(BF16)
    c = jnp.dot(hn, wd_ref[...], preferred_element_type=F32)
    cos = cos_ref[...]
    sin = sin_ref[...]
    scale = (MLA_NOPE + MLA_ROPE) ** -0.5 * LOG2E

    cq = _rms(c[:, :MLA_Q_RANK], qn_ref[...]).astype(BF16)
    for n0 in range(0, MLA_QN, MLA_CHUNK):
        qn = jnp.dot(cq, wq_ref[:, n0:n0 + MLA_CHUNK], preferred_element_type=F32)
        oqn_ref[:, n0:n0 + MLA_CHUNK] = (qn * scale).astype(BF16)
    cs = cos * scale
    ss = sin * scale
    for n0 in range(0, MLA_QR, MLA_CHUNK):
        qr = jnp.dot(cq, wq_ref[:, MLA_QN + n0:MLA_QN + n0 + MLA_CHUNK], preferred_element_type=F32)
        qp = jnp.dot(cq, wq_ref[:, MLA_QN + MLA_QR + n0:MLA_QN + MLA_QR + n0 + MLA_CHUNK],
                     preferred_element_type=F32)
        for m in range(MLA_CHUNK // LANE):
            ls = slice(m * LANE, (m + 1) * LANE)
            oqr_ref[:, n0 + m * LANE:n0 + (m + 1) * LANE] = (qr[:, ls] * cs + qp[:, ls] * ss).astype(BF16)

    ckv = _rms(c[:, MLA_Q_RANK:MLA_Q_RANK + MLA_KV_RANK], kvn_ref[...]).astype(BF16)
    for n0 in range(0, MLA_KV, MLA_CHUNK):
        okv_ref[:, n0:n0 + MLA_CHUNK] = jnp.dot(
            ckv, wkv_ref[:, n0:n0 + MLA_CHUNK], preferred_element_type=F32).astype(BF16)

    o = MLA_Q_RANK + MLA_KV_RANK
    kr2 = c[:, o:o + LANE] * cos + c[:, o + LANE:o + 2 * LANE] * sin
    left = lax.broadcasted_iota(jnp.int32, kr2.shape, 1) < MLA_ROPE
    okr_ref[:, :LANE] = jnp.where(left, kr2, 0.0).astype(BF16)
    okr_ref[:, LANE:] = jnp.where(left, 0.0, kr2).astype(BF16)


def _mla_proj(h, ln, wd, q_norm, kv_norm, wq, wkv, cos4, sin4, seq, tm):
    rows = h.shape[0]
    nt = seq // tm
    row = lambda i: (i, 0)
    return pl.pallas_call(
        _mla_proj_kernel,
        grid=(rows // tm,),
        in_specs=[
            pl.BlockSpec((tm, D_MODEL), row),
            _const_spec((1, D_MODEL)),
            _const_spec(wd.shape),
            _const_spec((1, MLA_Q_RANK)),
            _const_spec((1, MLA_KV_RANK)),
            _const_spec(wq.shape),
            _const_spec(wkv.shape),
            pl.BlockSpec((tm, LANE), lambda i: (i % nt, 0)),
            pl.BlockSpec((tm, LANE), lambda i: (i % nt, 0)),
        ],
        out_specs=[
            pl.BlockSpec((tm, MLA_QN), row),
            pl.BlockSpec((tm, MLA_QR), row),
            pl.BlockSpec((tm, MLA_KV), row),
            pl.BlockSpec((tm, 2 * LANE), row),
        ],
        out_shape=[
            jax.ShapeDtypeStruct((rows, MLA_QN), BF16),
            jax.ShapeDtypeStruct((rows, MLA_QR), BF16),
            jax.ShapeDtypeStruct((rows, MLA_KV), BF16),
            jax.ShapeDtypeStruct((rows, 2 * LANE), BF16),
        ],
        compiler_params=_params(("parallel",)),
        name="mla_proj",
    )(h, ln, wd, q_norm, kv_norm, wq, wkv, cos4, sin4)


def _attn_kernel(qn_ref, qr_ref, kv_ref, kr_ref, o_ref, *, tq, nq):
    row = lax.broadcasted_iota(jnp.int32, (tq, tq), 0)
    col = lax.broadcasted_iota(jnp.int32, (tq, tq), 1)
    m = [[jnp.full((tq, 1), -1e30, F32) for _ in range(nq)] for _ in range(2)]
    l = [[jnp.zeros((tq, 1), F32) for _ in range(nq)] for _ in range(2)]
    acc = [[jnp.zeros((tq, MLA_V), F32) for _ in range(nq)] for _ in range(2)]

    for kb in range(nq):
        r0 = kb * tq
        for s in range(2):
            k = jnp.concatenate([kv_ref[r0:r0 + tq, 2 * s * LANE:(2 * s + 1) * LANE],
                                 kr_ref[r0:r0 + tq, s * LANE:(s + 1) * LANE]], axis=-1)
            v = kv_ref[r0:r0 + tq, (2 * s + 1) * LANE:(2 * s + 2) * LANE]
            q = jnp.concatenate([qn_ref[r0:, s * LANE:(s + 1) * LANE], qr_ref[r0:, :]], axis=-1)
            sc_all = lax.dot_general(q, k, (((1,), (1,)), ((), ())), preferred_element_type=F32)
            ps = []
            for qb in range(kb, nq):
                sc = sc_all[(qb - kb) * tq:(qb - kb + 1) * tq]
                if qb == kb:
                    sc = jnp.where(row >= col, sc, -1e30)
                m_new = jnp.maximum(m[s][qb], jnp.max(sc, axis=-1, keepdims=True))
                alpha = jnp.exp2(m[s][qb] - m_new)
                p = jnp.exp2(sc - m_new)
                l[s][qb] = alpha * l[s][qb] + jnp.sum(p, axis=-1, keepdims=True)
                acc[s][qb] = alpha * acc[s][qb]
                m[s][qb] = m_new
                ps.append(p.astype(BF16))
            pv = jnp.dot(jnp.concatenate(ps, axis=0), v, preferred_element_type=F32)
            for qb in range(kb, nq):
                acc[s][qb] = acc[s][qb] + pv[(qb - kb) * tq:(qb - kb + 1) * tq]
            o_ref[r0:r0 + tq, s * LANE:(s + 1) * LANE] = (acc[s][kb] / l[s][kb]).astype(BF16)


def _attention(qn, qr, kv, krp, nb, seq, tq):
    npair = MLA_HEADS // 2
    nq = seq // tq
    return pl.pallas_call(
        functools.partial(_attn_kernel, tq=tq, nq=nq),
        grid=(nb, npair),
        in_specs=[
            pl.BlockSpec((seq, 2 * LANE), lambda b, h: (b, h)),
            pl.BlockSpec((seq, LANE), lambda b, h: (b, h)),
            pl.BlockSpec((seq, 4 * LANE), lambda b, h: (b, h)),
            pl.BlockSpec((seq, 2 * LANE), lambda b, h: (b, 0)),
        ],
        out_specs=pl.BlockSpec((seq, 2 * LANE), lambda b, h: (b, h)),
        out_shape=jax.ShapeDtypeStruct((nb * seq, MLA_HEADS * MLA_V), BF16),
        compiler_params=_params(("parallel", "parallel")),
        name="mla_attention",
    )(qn, qr, kv, krp)


def _rope_tables(n, dim):
    inv = ROPE_BASE ** (-jnp.arange(0, dim, 2, dtype=F32) / dim)
    ang = jnp.arange(n, dtype=F32)[:, None] * inv[None, :]
    return jnp.cos(ang), jnp.sin(ang)


def _rot_cols(w):
    lead = w.shape[:-1]
    w = w.reshape(lead + (-1, 2, MLA_ROPE // 2))
    return jnp.stack([-w[..., 1, :], w[..., 0, :]], axis=-2).reshape(lead + (-1,))


def kernel(x, ln_mix_even, w_in_even, s5_lambda_re, s5_lambda_im, s5_log_dt, s5_b_re, s5_b_im, s5_c_re, s5_c_im, s5_d, s5_glu_w, s5_glu_b, ret_norm, w_out_even, ln_mix_odd, mla_w_down, mla_q_norm, mla_w_uq, mla_kv_norm, mla_w_ukv, w_out_odd, ln_ffn, ffn_w_up, ffn_conv_w, ffn_conv_b, ffn_w_down, ln_final):
    nb, seq, d = x.shape
    assert d == D_MODEL and nb % 8 == 0 and seq % RET_CHUNK == 0
    rows = nb * seq
    tm = min(seq, 1024)
    tmo = min(seq, 512)
    tf = 512
    row2 = lambda a: a.reshape(1, -1)

    x2d = x.reshape(rows, d)

    ret_cos, ret_sin = _rope_tables(seq, RET_HEAD_DIM)
    u, proj = _inproj(x2d, row2(ln_mix_even[0]), w_in_even[0].astype(BF16), ret_cos, ret_sin, seq, tm)

    ab_re, ab_im, bbr, bbi = _s5_discretise(s5_lambda_re[0], s5_lambda_im[0], s5_log_dt[0], s5_b_re[0], s5_b_im[0])
    a_re, a_im, bblk, cblk = _s5_pack(ab_re, ab_im, bbr, bbi, s5_c_re[0], s5_c_im[0])
    y_s5 = _s5(u.reshape(nb, seq, S5_WIDTH), bblk, cblk, a_re, a_im, row2(s5_d[0]),
               s5_glu_w[0].astype(BF16), row2(s5_glu_b[0]), tc=min(seq, 32))
    y_ret = _retention(proj, row2(ret_norm[0]), nb, seq)

    h = _outproj(x2d, [y_s5.reshape(rows, S5_WIDTH), y_ret], w_out_even[0].astype(BF16), tmo, "even_outproj")

    w_up_all = ffn_w_up.astype(BF16)
    w_down_all = ffn_w_down.astype(BF16)

    def ffn(h, layer, final_norm):
        return _ffn(h, row2(ln_ffn[layer]), w_up_all, ffn_conv_w[layer], row2(ffn_conv_b[layer]), w_down_all,
                    row2(ln_final), layer, seq, tm, tf, final_norm)

    h = ffn(h, 0, False)

    mla_cos, mla_sin = _rope_tables(seq, MLA_ROPE)
    cos4 = jnp.tile(mla_cos, (1, 4))
    sin4 = jnp.tile(mla_sin, (1, 4))
    wdn = mla_w_down[0]
    w_kr = wdn[:, MLA_Q_RANK + MLA_KV_RANK:]
    w_krp = _rot_cols(w_kr)
    wd = jnp.concatenate([wdn[:, :MLA_Q_RANK + MLA_KV_RANK], w_kr, w_kr, w_krp, w_krp], axis=1).astype(BF16)
    wq3 = mla_w_uq[0].reshape(MLA_Q_RANK, MLA_HEADS, MLA_NOPE + MLA_ROPE)
    wq_n = wq3[:, :, :MLA_NOPE].reshape(MLA_Q_RANK, MLA_QN)
    wq_r = wq3[:, :, MLA_NOPE:].reshape(MLA_Q_RANK, MLA_QR)
    wq = jnp.concatenate([wq_n, wq_r, _rot_cols(wq_r)], axis=1).astype(BF16)
    qn, qr, kv, krp = _mla_proj(h, row2(ln_mix_odd[0]), wd, row2(mla_q_norm[0]), row2(mla_kv_norm[0]),
                                wq, mla_w_ukv[0].astype(BF16), cos4, sin4, seq, tmo)
    o = _attention(qn, qr, kv, krp, nb, seq, tq=min(seq, 512))
    h = _outproj(h, [o], w_out_odd[0].astype(BF16), tmo, "odd_outproj")
    h = ffn(h, 1, True)
    return h.reshape(nb, seq, d)
```

```python
import functools

import numpy as np
import jax
import jax.numpy as jnp
from jax import lax
from jax.experimental import pallas as pl
from jax.experimental.pallas import tpu as pltpu

F32 = jnp.float32
BF16 = jnp.bfloat16

EPS = 1e-6
ROPE_BASE = 10000.0
D_MODEL = 2048
S5_WIDTH = 512
S5_GROUP = 16
S5_GROUPS = 32
S5_STATE = 64
S5_GB = 8
S5_NBLK = S5_GROUPS // S5_GB
S5_NSTATE = S5_GROUPS * S5_STATE
RET_HEADS = 6
RET_HEAD_DIM = 256
RET_WIDTH = RET_HEADS * RET_HEAD_DIM
RET_CHUNK = 256
RET_STEP_CHUNKS = 4
EVEN_IN = S5_WIDTH + 4 * RET_WIDTH
MLA_HEADS = 16
MLA_NOPE = 128
MLA_ROPE = 64
MLA_V = 128
MLA_Q_RANK = 512
MLA_KV_RANK = 512
D_FF = 5632
FFN_DOWN_SUBTILES = 4

LANE = 128
BF16_ROWS = 16
VMEM_LIMIT = 60 * 1024 * 1024


def _params(sem, vmem=VMEM_LIMIT, flags=None):
    return pltpu.CompilerParams(dimension_semantics=sem, vmem_limit_bytes=vmem, flags=flags)


def _rms(x, g):
    return x * lax.rsqrt(jnp.mean(x * x, axis=-1, keepdims=True) + EPS) * g


def _const_spec(shape):
    nd = len(shape)
    return pl.BlockSpec(shape, lambda *_: (0,) * nd, pipeline_mode=pl.Buffered(1))


IN_TN = RET_WIDTH
IN_NJ = 4
IN_CHUNK = 512
IN_SUBTILES = 2


def _inproj_kernel(h_ref, ln_ref, wu_ref, w0_ref, w1_ref, w2_ref, cos_ref, sin_ref, u_ref, p_ref, hn_ref):
    j = pl.program_id(1)
    ts = hn_ref.shape[0] // IN_SUBTILES
    w_ref = (w0_ref, w1_ref, w2_ref)

    def pieces(wrefs):
        for c, wref in enumerate(wrefs):
            for sub in range(IN_SUBTILES):
                rs = slice(sub * ts, (sub + 1) * ts)
                cs = slice(c * IN_CHUNK, (c + 1) * IN_CHUNK)
                yield rs, cs, jnp.dot(hn_ref[rs, :], wref[...], preferred_element_type=F32)

    @pl.when(j == 0)
    def _():
        hn_ref[...] = _rms(h_ref[...], ln_ref[...]).astype(BF16)
        for rs, cs, acc in pieces((wu_ref,)):
            u_ref[rs, cs] = acc

    @pl.when(j <= 1)
    def _():
        sc = jnp.where(j == 1, RET_HEAD_DIM ** -0.5, 1.0).astype(F32)
        for rs, cs, acc in pieces(w_ref):
            c = cos_ref[rs, :] * sc
            s = sin_ref[rs, :] * sc
            for hh in range(IN_CHUNK // RET_HEAD_DIM):
                o = hh * RET_HEAD_DIM
                x1 = acc[:, o:o + LANE]
                x2 = acc[:, o + LANE:o + 2 * LANE]
                p_ref[rs, cs.start + o:cs.start + o + LANE] = (x1 * c - x2 * s).astype(BF16)
                p_ref[rs, cs.start + o + LANE:cs.start + o + 2 * LANE] = (x2 * c + x1 * s).astype(BF16)

    @pl.when(j == 2)
    def _():
        for rs, cs, acc in pieces(w_ref):
            p_ref[rs, cs] = acc.astype(BF16)

    @pl.when(j == 3)
    def _():
        for rs, cs, acc in pieces(w_ref):
            p_ref[rs, cs] = (acc * jax.nn.sigmoid(acc)).astype(BF16)


def _inproj(x2d, ln, w, cos, sin, seq, tm):
    rows = x2d.shape[0]
    nt = seq // tm
    nc = IN_TN // IN_CHUNK
    wblk = lambda c: pl.BlockSpec((D_MODEL, IN_CHUNK), lambda i, j: (0, 1 + nc * j + c))
    return pl.pallas_call(
        _inproj_kernel,
        grid=(rows // tm, IN_NJ),
        in_specs=[
            pl.BlockSpec((tm, D_MODEL), lambda i, j: (i, 0)),
            pl.BlockSpec((1, D_MODEL), lambda i, j: (0, 0)),
            pl.BlockSpec((D_MODEL, S5_WIDTH), lambda i, j: (0, 0), pipeline_mode=pl.Buffered(1)),
            wblk(0), wblk(1), wblk(2),
            pl.BlockSpec((tm, LANE), lambda i, j: (i % nt, 0)),
            pl.BlockSpec((tm, LANE), lambda i, j: (i % nt, 0)),
        ],
        out_specs=[
            pl.BlockSpec((tm, S5_WIDTH), lambda i, j: (i, 0)),
            pl.BlockSpec((tm, IN_TN), lambda i, j: (i, j)),
        ],
        out_shape=[
            jax.ShapeDtypeStruct((rows, S5_WIDTH), F32),
            jax.ShapeDtypeStruct((rows, EVEN_IN - S5_WIDTH), BF16),
        ],
        scratch_shapes=[pltpu.VMEM((tm, D_MODEL), BF16)],
        compiler_params=_params(("parallel", "arbitrary")),
        name="even_inproj",
    )(x2d, ln, w, w, w, w, cos, sin)


def _s5_disc_kernel(lr_ref, li_ref, ldt_ref, br_ref, bi_ref, are_ref, aim_ref, bbr_ref, bbi_ref):
    lr = lr_ref[...]
    li = li_ref[...]
    dt = jnp.exp(ldt_ref[...])
    mag = jnp.exp(lr * dt)
    ab_re = mag * jnp.cos(li * dt)
    ab_im = mag * jnp.sin(li * dt)
    den = lr * lr + li * li
    nr = ab_re - 1.0
    ni = ab_im
    f_re = (nr * lr + ni * li) / den
    f_im = (ni * lr - nr * li) / den
    are_ref[...] = ab_re
    aim_ref[...] = ab_im
    br = br_ref[...]
    bi = bi_ref[...]
    bbr_ref[...] = f_re[:, None, :] * br - f_im[:, None, :] * bi
    bbi_ref[...] = f_re[:, None, :] * bi + f_im[:, None, :] * br


def _s5_discretise(lam_re, lam_im, log_dt, b_re, b_im):
    g, p, hh = S5_GROUPS, S5_STATE, S5_GROUP
    brt = jnp.swapaxes(b_re, 1, 2)
    bit = jnp.swapaxes(b_im, 1, 2)
    return pl.pallas_call(
        _s5_disc_kernel,
        out_shape=[
            jax.ShapeDtypeStruct((g, p), F32),
            jax.ShapeDtypeStruct((g, p), F32),
            jax.ShapeDtypeStruct((g, hh, p), F32),
            jax.ShapeDtypeStruct((g, hh, p), F32),
        ],
        name="s5_discretise",
    )(lam_re, lam_im, log_dt.reshape(g, 1), brt, bit)


def _s5_pack(ab_re, ab_im, bbr, bbi, c_re, c_im):
    eye = jnp.eye(S5_GB, dtype=F32)
    nb, gb, p, hh = S5_NBLK, S5_GB, S5_STATE, S5_GROUP

    def pack_b(bb):
        return jnp.einsum('gy,aghp->aghyp', eye, bb.reshape(nb, gb, hh, p)).reshape(nb, gb * hh, gb * p)

    def pack_c(cc):
        return jnp.einsum('gy,aghp->agpyh', eye, cc.reshape(nb, gb, hh, p)).reshape(nb, gb * p, gb * hh)

    bblk = jnp.concatenate([pack_b(bbr), pack_b(bbi)], axis=-1).astype(BF16)
    cblk = jnp.concatenate([pack_c(c_re), pack_c(-c_im)], axis=1).astype(BF16)
    return ab_re.reshape(1, S5_NSTATE), ab_im.reshape(1, S5_NSTATE), bblk, cblk


S5_SLAB = 256


def _s5_kernel(u_hbm, bblk_ref, cblk_ref, are_ref, aim_ref, d_ref, gw_ref, gb_ref, y_hbm,
               ubuf, ybuf, in_sem, out_sem, bur_ref, bui_ref, xr_ref, xi_ref, sr_ref, si_ref, *, nb, tc):
    t = pl.program_id(0)
    nt = pl.num_programs(0)
    slot = t % 2
    half = S5_GB * S5_STATE
    wu = S5_GB * S5_GROUP

    def in_copy(step, sl, b):
        return pltpu.make_async_copy(u_hbm.at[b, pl.ds(step * tc, tc), :], ubuf.at[sl, :, b, :], in_sem.at[sl, b])

    def out_copy(step, sl, b):
        return pltpu.make_async_copy(ybuf.at[sl, :, b, :], y_hbm.at[b, pl.ds(step * tc, tc), :], out_sem.at[sl, b])

    @pl.when(t == 0)
    def _():
        for b in range(nb):
            in_copy(0, 0, b).start()
        sr_ref[...] = jnp.zeros_like(sr_ref)
        si_ref[...] = jnp.zeros_like(si_ref)

    @pl.when(t + 1 < nt)
    def _():
        for b in range(nb):
            in_copy(t + 1, 1 - slot, b).start()

    for b in range(nb):
        in_copy(t, slot, b).wait()

    u = ubuf[slot].reshape(tc * nb, S5_WIDTH)
    ub = u.astype(BF16)
    for a in range(S5_NBLK):
        bu = jnp.dot(ub[:, a * wu:(a + 1) * wu], bblk_ref[a], preferred_element_type=F32)
        bur_ref[:, a * half:(a + 1) * half] = bu[:, :half]
        bui_ref[:, a * half:(a + 1) * half] = bu[:, half:]

    for sl in range(S5_NSTATE // S5_SLAB):
        cs = slice(sl * S5_SLAB, (sl + 1) * S5_SLAB)
        ar = jnp.broadcast_to(are_ref[:, cs], (nb, S5_SLAB))
        ai = jnp.broadcast_to(aim_ref[:, cs], (nb, S5_SLAB))

        def step(k, carry, cs=cs, ar=ar, ai=ai):
            xr, xi = carry
            r0 = pl.multiple_of(k * nb, nb)
            nr = ar * xr - ai * xi + bur_ref[pl.ds(r0, nb), cs]
            ni = ar * xi + ai * xr + bui_ref[pl.ds(r0, nb), cs]
            xr_ref[pl.ds(r0, nb), cs] = nr.astype(BF16)
            xi_ref[pl.ds(r0, nb), cs] = ni.astype(BF16)
            return nr, ni

        xr, xi = lax.fori_loop(0, tc, step, (sr_ref[:, cs], si_ref[:, cs]), unroll=8)
        sr_ref[:, cs] = xr
        si_ref[:, cs] = xi

    ys = []
    for a in range(S5_NBLK):
        ya = jnp.dot(xr_ref[:, a * half:(a + 1) * half], cblk_ref[a, :half, :], preferred_element_type=F32)
        ya = ya + jnp.dot(xi_ref[:, a * half:(a + 1) * half], cblk_ref[a, half:, :], preferred_element_type=F32)
        ys.append(ya)
    y = jnp.concatenate(ys, axis=-1) + d_ref[...] * u
    y = jax.nn.gelu(y)
    z = jnp.dot(y.astype(BF16), gw_ref[...], preferred_element_type=F32) + gb_ref[...]
    y = y * jax.nn.sigmoid(z)

    @pl.when(t >= 2)
    def _():
        for b in range(nb):
            out_copy(t - 2, slot, b).wait()

    ybuf[slot] = y.reshape(tc, nb, S5_WIDTH)
    for b in range(nb):
        out_copy(t, slot, b).start()

    @pl.when(t == nt - 1)
    def _():
        for b in range(nb):
            out_copy(t, slot, b).wait()

        @pl.when(nt >= 2)
        def _():
            for b in range(nb):
                out_copy(t - 1, 1 - slot, b).wait()


def _s5(u3, bblk, cblk, a_re, a_im, d_skip, glu_w, glu_b, tc):
    nb, seq, _ = u3.shape
    rows = nb * tc
    return pl.pallas_call(
        functools.partial(_s5_kernel, nb=nb, tc=tc),
        grid=(seq // tc,),
        in_specs=[
            pl.BlockSpec(memory_space=pl.ANY),
            _const_spec(bblk.shape),
            _const_spec(cblk.shape),
            _const_spec((1, S5_NSTATE)),
            _const_spec((1, S5_NSTATE)),
            _const_spec((1, S5_WIDTH)),
            _const_spec((S5_WIDTH, S5_WIDTH)),
            _const_spec((1, S5_WIDTH)),
        ],
        out_specs=pl.BlockSpec(memory_space=pl.ANY),
        out_shape=jax.ShapeDtypeStruct((nb, seq, S5_WIDTH), F32),
        scratch_shapes=[
            pltpu.VMEM((2, tc, nb, S5_WIDTH), F32),
            pltpu.VMEM((2, tc, nb, S5_WIDTH), F32),
            pltpu.SemaphoreType.DMA((2, nb)),
            pltpu.SemaphoreType.DMA((2, nb)),
            pltpu.VMEM((rows, S5_NSTATE), F32),
            pltpu.VMEM((rows, S5_NSTATE), F32),
            pltpu.VMEM((rows, S5_NSTATE), BF16),
            pltpu.VMEM((rows, S5_NSTATE), BF16),
            pltpu.VMEM((nb, S5_NSTATE), F32),
            pltpu.VMEM((nb, S5_NSTATE), F32),
        ],
        compiler_params=_params(("arbitrary",)),
        name="s5_scan",
    )(u3, bblk, cblk, a_re, a_im, d_skip, glu_w, glu_b)


def _retention_consts():
    hs = np.arange(RET_HEADS, dtype=np.float32)
    gamma_log = np.log((1.0 - np.exp2(-5.0 - hs)).astype(np.float32)).astype(np.float32)
    c = RET_CHUNK
    idx = np.arange(c, dtype=np.float32)
    diff = idx[:, None] - idx[None, :]
    dmask = np.where(diff[None] >= 0, np.exp(np.maximum(diff[None], 0.0) * gamma_log[:, None, None]), 0.0)
    xi = np.exp((idx[None, :] + 1.0) * gamma_log[:, None])
    zeta = np.exp((c - 1.0 - idx[None, :]) * gamma_log[:, None])
    g_chunk = np.exp(c * gamma_log)
    xi_b = np.broadcast_to(xi[:, :, None], (RET_HEADS, c, RET_HEAD_DIM))
    zeta_b = np.broadcast_to(zeta[:, :, None], (RET_HEADS, c, RET_HEAD_DIM))
    return (dmask.astype(np.float32), np.ascontiguousarray(xi_b, np.float32),
            np.ascontiguousarray(zeta_b, np.float32), [float(v) for v in g_chunk.astype(np.float32)])


def _retention_kernel(q_ref, k_ref, v_ref, sg_ref, dm_ref, xi_ref, zeta_ref, rn_ref, y_ref, r_ref, *, g_chunk, nsub):
    c = pl.program_id(1)

    @pl.when(c == 0)
    def _():
        r_ref[...] = jnp.zeros_like(r_ref)

    for h in range(RET_HEADS):
        hs = slice(h * RET_HEAD_DIM, (h + 1) * RET_HEAD_DIM)
        r = r_ref[h]
        for sc in range(nsub):
            rows = slice(sc * RET_CHUNK, (sc + 1) * RET_CHUNK)
            q = q_ref[rows, hs]
            k = k_ref[rows, hs]
            v = v_ref[rows, hs]
            s = lax.dot_general(q, k, (((1,), (1,)), ((), ())), preferred_element_type=F32)
            inner = (s * dm_ref[h]).astype(BF16)
            o = jnp.dot(inner, v, preferred_element_type=F32)
            o = o + xi_ref[h] * jnp.dot(q, r.astype(BF16), preferred_element_type=F32)
            kz = (k.astype(F32) * zeta_ref[h]).astype(BF16)
            r = r * g_chunk[h] + lax.dot_general(kz, v, (((0,), (0,)), ((), ())), preferred_element_type=F32)
            o = o * lax.rsqrt(jnp.mean(o * o, axis=-1, keepdims=True) + EPS)
            o = o * rn_ref[:, hs]
            y_ref[rows, hs] = (sg_ref[rows, hs].astype(F32) * o).astype(BF16)
        r_ref[h] = r


def _retention(proj, ret_norm, nb, seq):
    dmask, xi_b, zeta_b, g_chunk = _retention_consts()
    nsub = min(RET_STEP_CHUNKS, seq // RET_CHUNK)
    step_rows = RET_CHUNK * nsub
    nc = seq // step_rows
    blk = (step_rows, RET_WIDTH)
    return pl.pallas_call(
        functools.partial(_retention_kernel, g_chunk=g_chunk, nsub=nsub),
        grid=(nb, nc),
        in_specs=[
            pl.BlockSpec(blk, lambda b, c: (b * nc + c, 0)),
            pl.BlockSpec(blk, lambda b, c: (b * nc + c, 1)),
            pl.BlockSpec(blk, lambda b, c: (b * nc + c, 2)),
            pl.BlockSpec(blk, lambda b, c: (b * nc + c, 3)),
            _const_spec(dmask.shape),
            _const_spec(xi_b.shape),
            _const_spec(zeta_b.shape),
            _const_spec((1, RET_WIDTH)),
        ],
        out_specs=pl.BlockSpec(blk, lambda b, c: (b * nc + c, 0)),
        out_shape=jax.ShapeDtypeStruct((nb * seq, RET_WIDTH), BF16),
        scratch_shapes=[pltpu.VMEM((RET_HEADS, RET_HEAD_DIM, RET_HEAD_DIM), F32)],
        compiler_params=_params(("parallel", "arbitrary")),
        name="retention",
    )(proj, proj, proj, proj, jnp.asarray(dmask), jnp.asarray(xi_b), jnp.asarray(zeta_b), ret_norm)


def _outproj_kernel(*refs, n_in):
    res_ref = refs[0]
    ys = refs[1:1 + n_in]
    w_ref = refs[1 + n_in]
    o_ref = refs[2 + n_in]
    acc = res_ref[...]
    k0 = 0
    for y in ys:
        kk = y.shape[1]
        acc = acc + jnp.dot(y[...].astype(BF16), w_ref[k0:k0 + kk, :], preferred_element_type=F32)
        k0 += kk
    o_ref[...] = acc


def _outproj(res, ys, w, tm, name):
    rows, d = res.shape
    row = lambda i: (i, 0)
    return pl.pallas_call(
        functools.partial(_outproj_kernel, n_in=len(ys)),
        grid=(rows // tm,),
        in_specs=([pl.BlockSpec((tm, d), row)] + [pl.BlockSpec((tm, y.shape[1]), row) for y in ys]
                  + [_const_spec(w.shape)]),
        out_specs=pl.BlockSpec((tm, d), row),
        out_shape=jax.ShapeDtypeStruct((rows, d), F32),
        compiler_params=_params(("parallel",)),
        name=name,
    )(res, *ys, w)


def _ffn_kernel(h_ref, halo_ref, ln_ref, wg_ref, wv_ref, cwg_ref, cwv_ref, cbg_ref, cbv_ref, wd_ref, lnf_ref,
                o_ref, hn_ref, ag_ref, av_ref, *, nt, tm, nj, final_norm):
    i = pl.program_id(0)
    s = pl.program_id(1)
    hl = BF16_ROWS
    slot = s % 2

    def up(sl):
        hn = hn_ref[...]
        ag_ref[sl] = jnp.dot(hn, wg_ref[...], preferred_element_type=F32)
        av_ref[sl] = jnp.dot(hn, wv_ref[...], preferred_element_type=F32)

    def down(sl):
        ts = tm // FFN_DOWN_SUBTILES
        for sub in range(FFN_DOWN_SUBTILES):
            r0 = sub * ts

            def conv(a_ref, cw_ref, cb_ref, r0=r0):
                cw = cw_ref[...]
                return (a_ref[sl, pl.ds(r0 + hl - 2, ts), :] * cw[0:1] + a_ref[sl, pl.ds(r0 + hl - 1, ts), :] * cw[1:2]
                        + a_ref[sl, pl.ds(r0 + hl, ts), :] * cw[2:3] + cb_ref[...])

            gate = conv(ag_ref, cwg_ref, cbg_ref)
            val = conv(av_ref, cwv_ref, cbv_ref)
            g = (gate * jax.nn.sigmoid(gate) * val).astype(BF16)
            o_ref[r0:r0 + ts, :] += jnp.dot(g, wd_ref[...], preferred_element_type=F32)

    @pl.when(s == 0)
    def _():
        hn_ref[hl:, :] = _rms(h_ref[...], ln_ref[...]).astype(BF16)
        halo = _rms(halo_ref[...], ln_ref[...])
        hn_ref[:hl, :] = jnp.where(i % nt > 0, halo, 0.0).astype(BF16)
        o_ref[...] = h_ref[...]
        up(0)

    @pl.when(jnp.logical_and(s > 0, s < nj))
    def _():
        up(slot)
        down(1 - slot)

    @pl.when(s == nj)
    def _():
        down(1 - slot)
        if final_norm:
            o_ref[...] = _rms(o_ref[...], lnf_ref[...])


def _ffn(h, ln, w_up, conv_w, conv_b, w_down, ln_final, layer, seq, tm, tf, final_norm):
    rows = h.shape[0]
    hl = BF16_ROWS
    nj = D_FF // tf
    nt = seq // tm
    up_j = lambda s: jnp.minimum(s, nj - 1)
    dn_j = lambda s: jnp.maximum(s - 1, 0)
    return pl.pallas_call(
        functools.partial(_ffn_kernel, nt=nt, tm=tm, nj=nj, final_norm=final_norm),
        grid=(rows // tm, nj + 1),
        in_specs=[
            pl.BlockSpec((tm, D_MODEL), lambda i, s: (i, 0), pipeline_mode=pl.Buffered(1)),
            pl.BlockSpec((hl, D_MODEL), lambda i, s: (jnp.maximum(i * (tm // hl) - 1, 0), 0)),
            pl.BlockSpec((1, D_MODEL), lambda i, s: (0, 0)),
            pl.BlockSpec((None, D_MODEL, tf), lambda i, s: (layer, 0, up_j(s))),
            pl.BlockSpec((None, D_MODEL, tf), lambda i, s: (layer, 0, nj + up_j(s))),
            pl.BlockSpec((3, tf), lambda i, s: (0, dn_j(s))),
            pl.BlockSpec((3, tf), lambda i, s: (0, nj + dn_j(s))),
            pl.BlockSpec((1, tf), lambda i, s: (0, dn_j(s))),
            pl.BlockSpec((1, tf), lambda i, s: (0, nj + dn_j(s))),
            pl.BlockSpec((None, tf, D_MODEL), lambda i, s: (layer, dn_j(s), 0)),
            pl.BlockSpec((1, D_MODEL), lambda i, s: (0, 0)),
        ],
        out_specs=pl.BlockSpec((tm, D_MODEL), lambda i, s: (i, 0)),
        out_shape=jax.ShapeDtypeStruct((rows, D_MODEL), F32),
        scratch_shapes=[
            pltpu.VMEM((hl + tm, D_MODEL), BF16),
            pltpu.VMEM((2, hl + tm, tf), F32),
            pltpu.VMEM((2, hl + tm, tf), F32),
        ],
        compiler_params=_params(("parallel", "arbitrary")),
        name="conv_ffn",
    )(h, h, ln, w_up, w_up, conv_w, conv_w, conv_b, conv_b, w_down, ln_final)


MLA_DOWN_PAD = MLA_Q_RANK + MLA_KV_RANK + 2 * LANE
MLA_QN = MLA_HEADS * MLA_NOPE
MLA_QR = MLA_HEADS * MLA_ROPE
MLA_KV = MLA_HEADS * (MLA_NOPE + MLA_V)
MLA_CHUNK = 1024
LOG2E = 1.4426950408889634


def _mla_proj_kernel(h_ref, ln_ref, wd_ref, qn_ref, kvn_ref, wq_ref, wkv_ref, cos_ref, sin_ref,
                     oqn_ref, oqr_ref, okv_ref, okr_ref):
    hn = _rms(h_ref[...], ln_ref[...]).astype(BF16)
    c = jnp.dot(hn, wd_ref[...], preferred_element_type=F32)
    cos = cos_ref[...]
    sin = sin_ref[...]
    scale = (MLA_NOPE + MLA_ROPE) ** -0.5 * LOG2E

    cq = _rms(c[:, :MLA_Q_RANK], qn_ref[...]).astype(BF16)
    for n0 in range(0, MLA_QN, MLA_CHUNK):
        qn = jnp.dot(cq, wq_ref[:, n0:n0 + MLA_CHUNK], preferred_element_type=F32)
        oqn_ref[:, n0:n0 + MLA_CHUNK] = (qn * scale).astype(BF16)
    cs = cos * scale
    ss = sin * scale
    for n0 in range(0, MLA_QR, MLA_CHUNK):
        qr = jnp.dot(cq, wq_ref[:, MLA_QN + n0:MLA_QN + n0 + MLA_CHUNK], preferred_element_type=F32)
        qp = jnp.dot(cq, wq_ref[:, MLA_QN + MLA_QR + n0:MLA_QN + MLA_QR + n0 + MLA_CHUNK],
                     preferred_element_type=F32)
        for m in range(MLA_CHUNK // LANE):
            ls = slice(m * LANE, (m + 1) * LANE)
            oqr_ref[:, n0 + m * LANE:n0 + (m + 1) * LANE] = (qr[:, ls] * cs + qp[:, ls] * ss).astype(BF16)

    ckv = _rms(c[:, MLA_Q_RANK:MLA_Q_RANK + MLA_KV_RANK], kvn_ref[...]).astype(BF16)
    for n0 in range(0, MLA_KV, MLA_CHUNK):
        okv_ref[:, n0:n0 + MLA_CHUNK] = jnp.dot(
            ckv, wkv_ref[:, n0:n0 + MLA_CHUNK], preferred_element_type=F32).astype(BF16)

    o = MLA_Q_RANK + MLA_KV_RANK
    kr2 = c[:, o:o + LANE] * cos + c[:, o + LANE:o + 2 * LANE] * sin
    left = lax.broadcasted_iota(jnp.int32, kr2.shape, 1) < MLA_ROPE
    okr_ref[:, :LANE] = jnp.where(left, kr2, 0.0).astype(BF16)
    okr_ref[:, LANE:] = jnp.where(left, 0.0, kr2).astype(BF16)


def _mla_proj(h, ln, wd, q_norm, kv_norm, wq, wkv, cos4, sin4, seq, tm):
    rows = h.shape[0]
    nt = seq // tm
    row = lambda i: (i, 0)
    return pl.pallas_call(
        _mla_proj_kernel,
        grid=(rows // tm,),
        in_specs=[
            pl.BlockSpec((tm, D_MODEL), row),
            _const_spec((1, D_MODEL)),
            _const_spec(wd.shape),
            _const_spec((1, MLA_Q_RANK)),
            _const_spec((1, MLA_KV_RANK)),
            _const_spec(wq.shape),
            _const_spec(wkv.shape),
            pl.BlockSpec((tm, LANE), lambda i: (i % nt, 0)),
            pl.BlockSpec((tm, LANE), lambda i: (i % nt, 0)),
        ],
        out_specs=[
            pl.BlockSpec((tm, MLA_QN), row),
            pl.BlockSpec((tm, MLA_QR), row),
            pl.BlockSpec((tm, MLA_KV), row),
            pl.BlockSpec((tm, 2 * LANE), row),
        ],
        out_shape=[
            jax.ShapeDtypeStruct((rows, MLA_QN), BF16),
            jax.ShapeDtypeStruct((rows, MLA_QR), BF16),
            jax.ShapeDtypeStruct((rows, MLA_KV), BF16),
            jax.ShapeDtypeStruct((rows, 2 * LANE), BF16),
        ],
        compiler_params=_params(("parallel",)),
        name="mla_proj",
    )(h, ln, wd, q_norm, kv_norm, wq, wkv, cos4, sin4)


def _attn_kernel(qn_ref, qr_ref, kv_ref, kr_ref, o_ref, *, tq, nq):
    row = lax.broadcasted_iota(jnp.int32, (tq, tq), 0)
    col = lax.broadcasted_iota(jnp.int32, (tq, tq), 1)
    m = [[jnp.full((tq, 1), -1e30, F32) for _ in range(nq)] for _ in range(2)]
    l = [[jnp.zeros((tq, 1), F32) for _ in range(nq)] for _ in range(2)]
    acc = [[jnp.zeros((tq, MLA_V), F32) for _ in range(nq)] for _ in range(2)]

    for kb in range(nq):
        r0 = kb * tq
        for s in range(2):
            k = jnp.concatenate([kv_ref[r0:r0 + tq, 2 * s * LANE:(2 * s + 1) * LANE],
                                 kr_ref[r0:r0 + tq, s * LANE:(s + 1) * LANE]], axis=-1)
            v = kv_ref[r0:r0 + tq, (2 * s + 1) * LANE:(2 * s + 2) * LANE]
            q = jnp.concatenate([qn_ref[r0:, s * LANE:(s + 1) * LANE], qr_ref[r0:, :]], axis=-1)
            sc_all = lax.dot_general(q, k, (((1,), (1,)), ((), ())), preferred_element_type=F32)
            ps = []
            for qb in range(kb, nq):
                sc = sc_all[(qb - kb) * tq:(qb - kb + 1) * tq]
                if qb == kb:
                    sc = jnp.where(row >= col, sc, -1e30)
                m_new = jnp.maximum(m[s][qb], jnp.max(sc, axis=-1, keepdims=True))
                alpha = jnp.exp2(m[s][qb] - m_new)
                p = jnp.exp2(sc - m_new)
                l[s][qb] = alpha * l[s][qb] + jnp.sum(p, axis=-1, keepdims=True)
                acc[s][qb] = alpha * acc[s][qb]
                m[s][qb] = m_new
                ps.append(p.astype(BF16))
            pv = jnp.dot(jnp.concatenate(ps, axis=0), v, preferred_element_type=F32)
            for qb in range(kb, nq):
                acc[s][qb] = acc[s][qb] + pv[(qb - kb) * tq:(qb - kb + 1) * tq]
            o_ref[r0:r0 + tq, s * LANE:(s + 1) * LANE] = (acc[s][kb] / l[s][kb]).astype(BF16)


def _attention(qn, qr, kv, krp, nb, seq, tq):
    npair = MLA_HEADS // 2
    nq = seq // tq
    return pl.pallas_call(
        functools.partial(_attn_kernel, tq=tq, nq=nq),
        grid=(nb, npair),
        in_specs=[
            pl.BlockSpec((seq, 2 * LANE), lambda b, h: (b, h)),
            pl.BlockSpec((seq, LANE), lambda b, h: (b, h)),
            pl.BlockSpec((seq, 4 * LANE), lambda b, h: (b, h)),
            pl.BlockSpec((seq, 2 * LANE), lambda b, h: (b, 0)),
        ],
        out_specs=pl.BlockSpec((seq, 2 * LANE), lambda b, h: (b, h)),
        out_shape=jax.ShapeDtypeStruct((nb * seq, MLA_HEADS * MLA_V), BF16),
        compiler_params=_params(("parallel", "parallel")),
        name="mla_attention",
    )(qn, qr, kv, krp)


def _rope_tables(n, dim):
    inv = ROPE_BASE ** (-jnp.arange(0, dim, 2, dtype=F32) / dim)
    ang = jnp.arange(n, dtype=F32)[:, None] * inv[None, :]
    return jnp.cos(ang), jnp.sin(ang)


def _rot_cols(w):
    lead = w.shape[:-1]
    w = w.reshape(lead + (-1, 2, MLA_ROPE // 2))
    return jnp.stack([-w[..., 1, :], w[..., 0, :]], axis=-2).reshape(lead + (-1,))


def kernel(x, ln_mix_even, w_in_even, s5_lambda_re, s5_lambda_im, s5_log_dt, s5_b_re, s5_b_im, s5_c_re, s5_c_im, s5_d, s5_glu_w, s5_glu_b, ret_norm, w_out_even, ln_mix_odd, mla_w_down, mla_q_norm, mla_w_uq, mla_kv_norm, mla_w_ukv, w_out_odd, ln_ffn, ffn_w_up, ffn_conv_w, ffn_conv_b, ffn_w_down, ln_final):
    nb, seq, d = x.shape
    assert d == D_MODEL and nb % 8 == 0 and seq % RET_CHUNK == 0
    rows = nb * seq
    tm = min(seq, 1024)
    tmo = min(seq, 512)
    tf = 512
    row2 = lambda a: a.reshape(1, -1)

    x2d = x.reshape(rows, d)

    ret_cos, ret_sin = _rope_tables(seq, RET_HEAD_DIM)
    u, proj = _inproj(x2d, row2(ln_mix_even[0]), w_in_even[0].astype(BF16), ret_cos, ret_sin, seq, tm)

    ab_re, ab_im, bbr, bbi = _s5_discretise(s5_lambda_re[0], s5_lambda_im[0], s5_log_dt[0], s5_b_re[0], s5_b_im[0])
    a_re, a_im, bblk, cblk = _s5_pack(ab_re, ab_im, bbr, bbi, s5_c_re[0], s5_c_im[0])
    y_s5 = _s5(u.reshape(nb, seq, S5_WIDTH), bblk, cblk, a_re, a_im, row2(s5_d[0]),
               s5_glu_w[0].astype(BF16), row2(s5_glu_b[0]), tc=min(seq, 64))
    y_ret = _retention(proj, row2(ret_norm[0]), nb, seq)

    h = _outproj(x2d, [y_s5.reshape(rows, S5_WIDTH), y_ret], w_out_even[0].astype(BF16), tmo, "even_outproj")

    w_up_all = ffn_w_up.astype(BF16)
    w_down_all = ffn_w_down.astype(BF16)

    def ffn(h, layer, final_norm):
        return _ffn(h, row2(ln_ffn[layer]), w_up_all, ffn_conv_w[layer], row2(ffn_conv_b[layer]), w_down_all,
                    row2(ln_final), layer, seq, tm, tf, final_norm)

    h = ffn(h, 0, False)

    mla_cos, mla_sin = _rope_tables(seq, MLA_ROPE)
    cos4 = jnp.tile(mla_cos, (1, 4))
    sin4 = jnp.tile(mla_sin, (1, 4))
    wdn = mla_w_down[0]
    w_kr = wdn[:, MLA_Q_RANK + MLA_KV_RANK:]
    w_krp = _rot_cols(w_kr)
    wd = jnp.concatenate([wdn[:, :MLA_Q_RANK + MLA_KV_RANK], w_kr, w_kr, w_krp, w_krp], axis=1).astype(BF16)
    wq3 = mla_w_uq[0].reshape(MLA_Q_RANK, MLA_HEADS, MLA_NOPE + MLA_ROPE)
    wq_n = wq3[:, :, :MLA_NOPE].reshape(MLA_Q_RANK, MLA_QN)
    wq_r = wq3[:, :, MLA_NOPE:].reshape(MLA_Q_RANK, MLA_QR)
    wq = jnp.concatenate([wq_n, wq_r, _rot_cols(wq_r)], axis=1).astype(BF16)
    qn, qr, kv, krp = _mla_proj(h, row2(ln_mix_odd[0]), wd, row2(mla_q_norm[0]), row2(mla_kv_norm[0]),
                                wq, mla_w_ukv[0].astype(BF16), cos4, sin4, seq, tmo)
    o = _attention(qn, qr, kv, krp, nb, seq, tq=min(seq, 512))
    h = _outproj(h, [o], w_out_odd[0].astype(BF16), tmo, "odd_outproj")
    h = ffn(h, 1, True)
    return h.reshape(nb, seq, d)
```

```python
import functools

import numpy as np
import jax
import jax.numpy as jnp
from jax import lax
from jax.experimental import pallas as pl
from jax.experimental.pallas import tpu as pltpu

F32 = jnp.float32
BF16 = jnp.bfloat16

EPS = 1e-6
ROPE_BASE = 10000.0
D_MODEL = 2048
S5_WIDTH = 512
S5_GROUP = 16
S5_GROUPS = 32
S5_STATE = 64
S5_GB = 8
S5_NBLK = S5_GROUPS // S5_GB
S5_NSTATE = S5_GROUPS * S5_STATE
RET_HEADS = 6
RET_HEAD_DIM = 256
RET_WIDTH = RET_HEADS * RET_HEAD_DIM
RET_CHUNK = 256
RET_STEP_CHUNKS = 4
EVEN_IN = S5_WIDTH + 4 * RET_WIDTH
MLA_HEADS = 16
MLA_NOPE = 128
MLA_ROPE = 64
MLA_V = 128
MLA_Q_RANK = 512
MLA_KV_RANK = 512
D_FF = 5632
FFN_DOWN_SUBTILES = 4

LANE = 128
BF16_ROWS = 16
VMEM_LIMIT = 60 * 1024 * 1024
FFN_VMEM_LIMIT = 63 * 1024 * 1024


def _params(sem, vmem=VMEM_LIMIT, flags=None):
    return pltpu.CompilerParams(dimension_semantics=sem, vmem_limit_bytes=vmem, flags=flags)


def _rms(x, g):
    return x * lax.rsqrt(jnp.mean(x * x, axis=-1, keepdims=True) + EPS) * g


def _const_spec(shape):
    nd = len(shape)
    return pl.BlockSpec(shape, lambda *_: (0,) * nd, pipeline_mode=pl.Buffered(1))


IN_TN = RET_WIDTH
IN_NJ = 4
IN_CHUNK = 512
IN_SUBTILES = 2


def _inproj_kernel(h_ref, ln_ref, wu_ref, w0_ref, w1_ref, w2_ref, cos_ref, sin_ref, u_ref, p_ref, hn_ref):
    j = pl.program_id(1)
    ts = hn_ref.shape[0] // IN_SUBTILES
    w_ref = (w0_ref, w1_ref, w2_ref)

    def pieces(wrefs):
        for c, wref in enumerate(wrefs):
            for sub in range(IN_SUBTILES):
                rs = slice(sub * ts, (sub + 1) * ts)
                cs = slice(c * IN_CHUNK, (c + 1) * IN_CHUNK)
                yield rs, cs, jnp.dot(hn_ref[rs, :], wref[...], preferred_element_type=F32)

    @pl.when(j == 0)
    def _():
        hn_ref[...] = _rms(h_ref[...], ln_ref[...]).astype(BF16)
        for rs, cs, acc in pieces((wu_ref,)):
            u_ref[rs, cs] = acc

    @pl.when(j <= 1)
    def _():
        sc = jnp.where(j == 1, RET_HEAD_DIM ** -0.5, 1.0).astype(F32)
        for rs, cs, acc in pieces(w_ref):
            c = cos_ref[rs, :] * sc
            s = sin_ref[rs, :] * sc
            for hh in range(IN_CHUNK // RET_HEAD_DIM):
                o = hh * RET_HEAD_DIM
                x1 = acc[:, o:o + LANE]
                x2 = acc[:, o + LANE:o + 2 * LANE]
                p_ref[rs, cs.start + o:cs.start + o + LANE] = (x1 * c - x2 * s).astype(BF16)
                p_ref[rs, cs.start + o + LANE:cs.start + o + 2 * LANE] = (x2 * c + x1 * s).astype(BF16)

    @pl.when(j == 2)
    def _():
        for rs, cs, acc in pieces(w_ref):
            p_ref[rs, cs] = acc.astype(BF16)

    @pl.when(j == 3)
    def _():
        for rs, cs, acc in pieces(w_ref):
            p_ref[rs, cs] = (acc * jax.nn.sigmoid(acc)).astype(BF16)


def _inproj(x2d, ln, w, cos, sin, seq, tm):
    rows = x2d.shape[0]
    nt = seq // tm
    nc = IN_TN // IN_CHUNK
    wblk = lambda c: pl.BlockSpec((D_MODEL, IN_CHUNK), lambda i, j: (0, 1 + nc * j + c))
    return pl.pallas_call(
        _inproj_kernel,
        grid=(rows // tm, IN_NJ),
        in_specs=[
            pl.BlockSpec((tm, D_MODEL), lambda i, j: (i, 0)),
            pl.BlockSpec((1, D_MODEL), lambda i, j: (0, 0)),
            pl.BlockSpec((D_MODEL, S5_WIDTH), lambda i, j: (0, 0), pipeline_mode=pl.Buffered(1)),
            wblk(0), wblk(1), wblk(2),
            pl.BlockSpec((tm, LANE), lambda i, j: (i % nt, 0)),
            pl.BlockSpec((tm, LANE), lambda i, j: (i % nt, 0)),
        ],
        out_specs=[
            pl.BlockSpec((tm, S5_WIDTH), lambda i, j: (i, 0)),
            pl.BlockSpec((tm, IN_TN), lambda i, j: (i, j)),
        ],
        out_shape=[
            jax.ShapeDtypeStruct((rows, S5_WIDTH), F32),
            jax.ShapeDtypeStruct((rows, EVEN_IN - S5_WIDTH), BF16),
        ],
        scratch_shapes=[pltpu.VMEM((tm, D_MODEL), BF16)],
        compiler_params=_params(("parallel", "arbitrary")),
        name="even_inproj",
    )(x2d, ln, w, w, w, w, cos, sin)


def _s5_disc_kernel(lr_ref, li_ref, ldt_ref, br_ref, bi_ref, are_ref, aim_ref, bbr_ref, bbi_ref):
    lr = lr_ref[...]
    li = li_ref[...]
    dt = jnp.exp(ldt_ref[...])
    mag = jnp.exp(lr * dt)
    ab_re = mag * jnp.cos(li * dt)
    ab_im = mag * jnp.sin(li * dt)
    den = lr * lr + li * li
    nr = ab_re - 1.0
    ni = ab_im
    f_re = (nr * lr + ni * li) / den
    f_im = (ni * lr - nr * li) / den
    are_ref[...] = ab_re
    aim_ref[...] = ab_im
    br = br_ref[...]
    bi = bi_ref[...]
    bbr_ref[...] = f_re[:, None, :] * br - f_im[:, None, :] * bi
    bbi_ref[...] = f_re[:, None, :] * bi + f_im[:, None, :] * br


def _s5_discretise(lam_re, lam_im, log_dt, b_re, b_im):
    g, p, hh = S5_GROUPS, S5_STATE, S5_GROUP
    brt = jnp.swapaxes(b_re, 1, 2)
    bit = jnp.swapaxes(b_im, 1, 2)
    return pl.pallas_call(
        _s5_disc_kernel,
        out_shape=[
            jax.ShapeDtypeStruct((g, p), F32),
            jax.ShapeDtypeStruct((g, p), F32),
            jax.ShapeDtypeStruct((g, hh, p), F32),
            jax.ShapeDtypeStruct((g, hh, p), F32),
        ],
        name="s5_discretise",
    )(lam_re, lam_im, log_dt.reshape(g, 1), brt, bit)


def _s5_pack(ab_re, ab_im, bbr, bbi, c_re, c_im):
    eye = jnp.eye(S5_GB, dtype=F32)
    nb, gb, p, hh = S5_NBLK, S5_GB, S5_STATE, S5_GROUP

    def pack_b(bb):
        return jnp.einsum('gy,aghp->aghyp', eye, bb.reshape(nb, gb, hh, p)).reshape(nb, gb * hh, gb * p)

    def pack_c(cc):
        return jnp.einsum('gy,aghp->agpyh', eye, cc.reshape(nb, gb, hh, p)).reshape(nb, gb * p, gb * hh)

    bblk = jnp.concatenate([pack_b(bbr), pack_b(bbi)], axis=-1).astype(BF16)
    cblk = jnp.concatenate([pack_c(c_re), pack_c(-c_im)], axis=1).astype(BF16)
    return ab_re.reshape(1, S5_NSTATE), ab_im.reshape(1, S5_NSTATE), bblk, cblk


S5_SLAB = 256


def _s5_kernel(u_hbm, bblk_ref, cblk_ref, are_ref, aim_ref, d_ref, gw_ref, gb_ref, y_hbm,
               ubuf, ybuf, in_sem, out_sem, bur_ref, bui_ref, xr_ref, xi_ref, sr_ref, si_ref, *, nb, tc):
    t = pl.program_id(0)
    nt = pl.num_programs(0)
    slot = t % 2
    half = S5_GB * S5_STATE
    wu = S5_GB * S5_GROUP

    def in_copy(step, sl, b):
        return pltpu.make_async_copy(u_hbm.at[b, pl.ds(step * tc, tc), :], ubuf.at[sl, :, b, :], in_sem.at[sl, b])

    def out_copy(step, sl, b):
        return pltpu.make_async_copy(ybuf.at[sl, :, b, :], y_hbm.at[b, pl.ds(step * tc, tc), :], out_sem.at[sl, b])

    @pl.when(t == 0)
    def _():
        for b in range(nb):
            in_copy(0, 0, b).start()
        sr_ref[...] = jnp.zeros_like(sr_ref)
        si_ref[...] = jnp.zeros_like(si_ref)

    @pl.when(t + 1 < nt)
    def _():
        for b in range(nb):
            in_copy(t + 1, 1 - slot, b).start()

    for b in range(nb):
        in_copy(t, slot, b).wait()

    u = ubuf[slot].reshape(tc * nb, S5_WIDTH)
    ub = u.astype(BF16)
    for a in range(S5_NBLK):
        bu = jnp.dot(ub[:, a * wu:(a + 1) * wu], bblk_ref[a], preferred_element_type=F32)
        bur_ref[:, a * half:(a + 1) * half] = bu[:, :half]
        bui_ref[:, a * half:(a + 1) * half] = bu[:, half:]

    for sl in range(S5_NSTATE // S5_SLAB):
        cs = slice(sl * S5_SLAB, (sl + 1) * S5_SLAB)
        ar = jnp.broadcast_to(are_ref[:, cs], (nb, S5_SLAB))
        ai = jnp.broadcast_to(aim_ref[:, cs], (nb, S5_SLAB))

        def step(k, carry, cs=cs, ar=ar, ai=ai):
            xr, xi = carry
            r0 = pl.multiple_of(k * nb, nb)
            nr = ar * xr - ai * xi + bur_ref[pl.ds(r0, nb), cs]
            ni = ar * xi + ai * xr + bui_ref[pl.ds(r0, nb), cs]
            xr_ref[pl.ds(r0, nb), cs] = nr.astype(BF16)
            xi_ref[pl.ds(r0, nb), cs] = ni.astype(BF16)
            return nr, ni

        xr, xi = lax.fori_loop(0, tc, step, (sr_ref[:, cs], si_ref[:, cs]), unroll=8)
        sr_ref[:, cs] = xr
        si_ref[:, cs] = xi

    ys = []
    for a in range(S5_NBLK):
        ya = jnp.dot(xr_ref[:, a * half:(a + 1) * half], cblk_ref[a, :half, :], preferred_element_type=F32)
        ya = ya + jnp.dot(xi_ref[:, a * half:(a + 1) * half], cblk_ref[a, half:, :], preferred_element_type=F32)
        ys.append(ya)
    y = jnp.concatenate(ys, axis=-1) + d_ref[...] * u
    y = jax.nn.gelu(y)
    z = jnp.dot(y.astype(BF16), gw_ref[...], preferred_element_type=F32) + gb_ref[...]
    y = y * jax.nn.sigmoid(z)

    @pl.when(t >= 2)
    def _():
        for b in range(nb):
            out_copy(t - 2, slot, b).wait()

    ybuf[slot] = y.reshape(tc, nb, S5_WIDTH)
    for b in range(nb):
        out_copy(t, slot, b).start()

    @pl.when(t == nt - 1)
    def _():
        for b in range(nb):
            out_copy(t, slot, b).wait()

        @pl.when(nt >= 2)
        def _():
            for b in range(nb):
                out_copy(t - 1, 1 - slot, b).wait()


def _s5(u3, bblk, cblk, a_re, a_im, d_skip, glu_w, glu_b, tc):
    nb, seq, _ = u3.shape
    rows = nb * tc
    return pl.pallas_call(
        functools.partial(_s5_kernel, nb=nb, tc=tc),
        grid=(seq // tc,),
        in_specs=[
            pl.BlockSpec(memory_space=pl.ANY),
            _const_spec(bblk.shape),
            _const_spec(cblk.shape),
            _const_spec((1, S5_NSTATE)),
            _const_spec((1, S5_NSTATE)),
            _const_spec((1, S5_WIDTH)),
            _const_spec((S5_WIDTH, S5_WIDTH)),
            _const_spec((1, S5_WIDTH)),
        ],
        out_specs=pl.BlockSpec(memory_space=pl.ANY),
        out_shape=jax.ShapeDtypeStruct((nb, seq, S5_WIDTH), F32),
        scratch_shapes=[
            pltpu.VMEM((2, tc, nb, S5_WIDTH), F32),
            pltpu.VMEM((2, tc, nb, S5_WIDTH), F32),
            pltpu.SemaphoreType.DMA((2, nb)),
            pltpu.SemaphoreType.DMA((2, nb)),
            pltpu.VMEM((rows, S5_NSTATE), F32),
            pltpu.VMEM((rows, S5_NSTATE), F32),
            pltpu.VMEM((rows, S5_NSTATE), BF16),
            pltpu.VMEM((rows, S5_NSTATE), BF16),
            pltpu.VMEM((nb, S5_NSTATE), F32),
            pltpu.VMEM((nb, S5_NSTATE), F32),
        ],
        compiler_params=_params(("arbitrary",)),
        name="s5_scan",
    )(u3, bblk, cblk, a_re, a_im, d_skip, glu_w, glu_b)


def _retention_consts():
    hs = np.arange(RET_HEADS, dtype=np.float32)
    gamma_log = np.log((1.0 - np.exp2(-5.0 - hs)).astype(np.float32)).astype(np.float32)
    c = RET_CHUNK
    idx = np.arange(c, dtype=np.float32)
    diff = idx[:, None] - idx[None, :]
    dmask = np.where(diff[None] >= 0, np.exp(np.maximum(diff[None], 0.0) * gamma_log[:, None, None]), 0.0)
    xi = np.exp((idx[None, :] + 1.0) * gamma_log[:, None])
    zeta = np.exp((c - 1.0 - idx[None, :]) * gamma_log[:, None])
    g_chunk = np.exp(c * gamma_log)
    xi_b = np.broadcast_to(xi[:, :, None], (RET_HEADS, c, RET_HEAD_DIM))
    zeta_b = np.broadcast_to(zeta[:, :, None], (RET_HEADS, c, RET_HEAD_DIM))
    return (dmask.astype(np.float32), np.ascontiguousarray(xi_b, np.float32),
            np.ascontiguousarray(zeta_b, np.float32), [float(v) for v in g_chunk.astype(np.float32)])


def _retention_kernel(q_ref, k_ref, v_ref, sg_ref, dm_ref, xi_ref, zeta_ref, rn_ref, y_ref, r_ref, *, g_chunk, nsub):
    c = pl.program_id(1)

    @pl.when(c == 0)
    def _():
        r_ref[...] = jnp.zeros_like(r_ref)

    for h in range(RET_HEADS):
        hs = slice(h * RET_HEAD_DIM, (h + 1) * RET_HEAD_DIM)
        r = r_ref[h]
        for sc in range(nsub):
            rows = slice(sc * RET_CHUNK, (sc + 1) * RET_CHUNK)
            q = q_ref[rows, hs]
            k = k_ref[rows, hs]
            v = v_ref[rows, hs]
            s = lax.dot_general(q, k, (((1,), (1,)), ((), ())), preferred_element_type=F32)
            inner = (s * dm_ref[h]).astype(BF16)
            o = jnp.dot(inner, v, preferred_element_type=F32)
            o = o + xi_ref[h] * jnp.dot(q, r.astype(BF16), preferred_element_type=F32)
            kz = (k.astype(F32) * zeta_ref[h]).astype(BF16)
            r = r * g_chunk[h] + lax.dot_general(kz, v, (((0,), (0,)), ((), ())), preferred_element_type=F32)
            o = o * lax.rsqrt(jnp.mean(o * o, axis=-1, keepdims=True) + EPS)
            o = o * rn_ref[:, hs]
            y_ref[rows, hs] = (sg_ref[rows, hs].astype(F32) * o).astype(BF16)
        r_ref[h] = r


def _retention(proj, ret_norm, nb, seq):
    dmask, xi_b, zeta_b, g_chunk = _retention_consts()
    nsub = min(RET_STEP_CHUNKS, seq // RET_CHUNK)
    step_rows = RET_CHUNK * nsub
    nc = seq // step_rows
    blk = (step_rows, RET_WIDTH)
    return pl.pallas_call(
        functools.partial(_retention_kernel, g_chunk=g_chunk, nsub=nsub),
        grid=(nb, nc),
        in_specs=[
            pl.BlockSpec(blk, lambda b, c: (b * nc + c, 0)),
            pl.BlockSpec(blk, lambda b, c: (b * nc + c, 1)),
            pl.BlockSpec(blk, lambda b, c: (b * nc + c, 2)),
            pl.BlockSpec(blk, lambda b, c: (b * nc + c, 3)),
            _const_spec(dmask.shape),
            _const_spec(xi_b.shape),
            _const_spec(zeta_b.shape),
            _const_spec((1, RET_WIDTH)),
        ],
        out_specs=pl.BlockSpec(blk, lambda b, c: (b * nc + c, 0)),
        out_shape=jax.ShapeDtypeStruct((nb * seq, RET_WIDTH), BF16),
        scratch_shapes=[pltpu.VMEM((RET_HEADS, RET_HEAD_DIM, RET_HEAD_DIM), F32)],
        compiler_params=_params(("parallel", "arbitrary")),
        name="retention",
    )(proj, proj, proj, proj, jnp.asarray(dmask), jnp.asarray(xi_b), jnp.asarray(zeta_b), ret_norm)


def _outproj_kernel(*refs, n_in):
    res_ref = refs[0]
    ys = refs[1:1 + n_in]
    w_ref = refs[1 + n_in]
    o_ref = refs[2 + n_in]
    acc = res_ref[...]
    k0 = 0
    for y in ys:
        kk = y.shape[1]
        acc = acc + jnp.dot(y[...].astype(BF16), w_ref[k0:k0 + kk, :], preferred_element_type=F32)
        k0 += kk
    o_ref[...] = acc


def _outproj(res, ys, w, tm, name):
    rows, d = res.shape
    row = lambda i: (i, 0)
    return pl.pallas_call(
        functools.partial(_outproj_kernel, n_in=len(ys)),
        grid=(rows // tm,),
        in_specs=([pl.BlockSpec((tm, d), row)] + [pl.BlockSpec((tm, y.shape[1]), row) for y in ys]
                  + [_const_spec(w.shape)]),
        out_specs=pl.BlockSpec((tm, d), row),
        out_shape=jax.ShapeDtypeStruct((rows, d), F32),
        compiler_params=_params(("parallel",)),
        name=name,
    )(res, *ys, w)


def _ffn_kernel(h_ref, halo_ref, ln_ref, wg_ref, wv_ref, cwg_ref, cwv_ref, cbg_ref, cbv_ref, wd_ref, lnf_ref,
                o_ref, hn_ref, ag_ref, av_ref, *, nt, tm, nj, final_norm):
    i = pl.program_id(0)
    s = pl.program_id(1)
    hl = BF16_ROWS
    slot = s % 2

    def up(sl):
        hn = hn_ref[...]
        ag_ref[sl] = jnp.dot(hn, wg_ref[...], preferred_element_type=F32)
        av_ref[sl] = jnp.dot(hn, wv_ref[...], preferred_element_type=F32)

    def down(sl):
        ts = tm // FFN_DOWN_SUBTILES
        for sub in range(FFN_DOWN_SUBTILES):
            r0 = sub * ts

            def conv(a_ref, cw_ref, cb_ref, r0=r0):
                cw = cw_ref[...]
                return (a_ref[sl, pl.ds(r0 + hl - 2, ts), :] * cw[0:1] + a_ref[sl, pl.ds(r0 + hl - 1, ts), :] * cw[1:2]
                        + a_ref[sl, pl.ds(r0 + hl, ts), :] * cw[2:3] + cb_ref[...])

            gate = conv(ag_ref, cwg_ref, cbg_ref)
            val = conv(av_ref, cwv_ref, cbv_ref)
            g = (gate * jax.nn.sigmoid(gate) * val).astype(BF16)
            o_ref[r0:r0 + ts, :] += jnp.dot(g, wd_ref[...], preferred_element_type=F32)

    @pl.when(s == 0)
    def _():
        hn_ref[hl:, :] = _rms(h_ref[...], ln_ref[...]).astype(BF16)
        halo = _rms(halo_ref[...], ln_ref[...])
        hn_ref[:hl, :] = jnp.where(i % nt > 0, halo, 0.0).astype(BF16)
        o_ref[...] = h_ref[...]
        up(0)

    @pl.when(jnp.logical_and(s > 0, s < nj))
    def _():
        up(slot)
        down(1 - slot)

    @pl.when(s == nj)
    def _():
        down(1 - slot)
        if final_norm:
            o_ref[...] = _rms(o_ref[...], lnf_ref[...])


def _ffn(h, ln, w_up, conv_w, conv_b, w_down, ln_final, layer, seq, tm, tf, final_norm):
    rows = h.shape[0]
    hl = BF16_ROWS
    nj = D_FF // tf
    nt = seq // tm
    up_j = lambda s: jnp.minimum(s, nj - 1)
    dn_j = lambda s: jnp.maximum(s - 1, 0)
    return pl.pallas_call(
        functools.partial(_ffn_kernel, nt=nt, tm=tm, nj=nj, final_norm=final_norm),
        grid=(rows // tm, nj + 1),
        in_specs=[
            pl.BlockSpec((tm, D_MODEL), lambda i, s: (i, 0)),
            pl.BlockSpec((hl, D_MODEL), lambda i, s: (jnp.maximum(i * (tm // hl) - 1, 0), 0)),
            pl.BlockSpec((1, D_MODEL), lambda i, s: (0, 0)),
            pl.BlockSpec((None, D_MODEL, tf), lambda i, s: (layer, 0, up_j(s))),
            pl.BlockSpec((None, D_MODEL, tf), lambda i, s: (layer, 0, nj + up_j(s))),
            pl.BlockSpec((3, tf), lambda i, s: (0, dn_j(s))),
            pl.BlockSpec((3, tf), lambda i, s: (0, nj + dn_j(s))),
            pl.BlockSpec((1, tf), lambda i, s: (0, dn_j(s))),
            pl.BlockSpec((1, tf), lambda i, s: (0, nj + dn_j(s))),
            pl.BlockSpec((None, tf, D_MODEL), lambda i, s: (layer, dn_j(s), 0)),
            pl.BlockSpec((1, D_MODEL), lambda i, s: (0, 0)),
        ],
        out_specs=pl.BlockSpec((tm, D_MODEL), lambda i, s: (i, 0)),
        out_shape=jax.ShapeDtypeStruct((rows, D_MODEL), F32),
        scratch_shapes=[
            pltpu.VMEM((hl + tm, D_MODEL), BF16),
            pltpu.VMEM((2, hl + tm, tf), F32),
            pltpu.VMEM((2, hl + tm, tf), F32),
        ],
        compiler_params=_params(("parallel", "arbitrary"), vmem=FFN_VMEM_LIMIT),
        name="conv_ffn",
    )(h, h, ln, w_up, w_up, conv_w, conv_w, conv_b, conv_b, w_down, ln_final)


MLA_DOWN_PAD = MLA_Q_RANK + MLA_KV_RANK + 2 * LANE
MLA_QN = MLA_HEADS * MLA_NOPE
MLA_QR = MLA_HEADS * MLA_ROPE
MLA_KV = MLA_HEADS * (MLA_NOPE + MLA_V)
MLA_CHUNK = 1024
LOG2E = 1.4426950408889634


def _mla_proj_kernel(h_ref, ln_ref, wd_ref, qn_ref, kvn_ref, wq_ref, wkv_ref, cos_ref, sin_ref,
                     oqn_ref, oqr_ref, okv_ref, okr_ref):
    hn = _rms(h_ref[...], ln_ref[...]).astype(BF16)
    c = jnp.dot(hn, wd_ref[...], preferred_element_type=F32)
    cos = cos_ref[...]
    sin = sin_ref[...]
    scale = (MLA_NOPE + MLA_ROPE) ** -0.5 * LOG2E

    cq = _rms(c[:, :MLA_Q_RANK], qn_ref[...]).astype(BF16)
    for n0 in range(0, MLA_QN, MLA_CHUNK):
        qn = jnp.dot(cq, wq_ref[:, n0:n0 + MLA_CHUNK], preferred_element_type=F32)
        oqn_ref[:, n0:n0 + MLA_CHUNK] = (qn * scale).astype(BF16)
    cs = cos * scale
    ss = sin * scale
    for n0 in range(0, MLA_QR, MLA_CHUNK):
        qr = jnp.dot(cq, wq_ref[:, MLA_QN + n0:MLA_QN + n0 + MLA_CHUNK], preferred_element_type=F32)
        qp = jnp.dot(cq, wq_ref[:, MLA_QN + MLA_QR + n0:MLA_QN + MLA_QR + n0 + MLA_CHUNK],
                     preferred_element_type=F32)
        for m in range(MLA_CHUNK // LANE):
            ls = slice(m * LANE, (m + 1) * LANE)
            oqr_ref[:, n0 + m * LANE:n0 + (m + 1) * LANE] = (qr[:, ls] * cs + qp[:, ls] * ss).astype(BF16)

    ckv = _rms(c[:, MLA_Q_RANK:MLA_Q_RANK + MLA_KV_RANK], kvn_ref[...]).astype(BF16)
    for n0 in range(0, MLA_KV, MLA_CHUNK):
        okv_ref[:, n0:n0 + MLA_CHUNK] = jnp.dot(
            ckv, wkv_ref[:, n0:n0 + MLA_CHUNK], preferred_element_type=F32).astype(BF16)

    o = MLA_Q_RANK + MLA_KV_RANK
    kr2 = c[:, o:o + LANE] * cos + c[:, o + LANE:o + 2 * LANE] * sin
    left = lax.broadcasted_iota(jnp.int32, kr2.shape, 1) < MLA_ROPE
    okr_ref[:, :LANE] = jnp.where(left, kr2, 0.0).astype(BF16)
    okr_ref[:, LANE:] = jnp.where(left, 0.0, kr2).astype(BF16)


def _mla_proj(h, ln, wd, q_norm, kv_norm, wq, wkv, cos4, sin4, seq, tm):
    rows = h.shape[0]
    nt = seq // tm
    row = lambda i: (i, 0)
    return pl.pallas_call(
        _mla_proj_kernel,
        grid=(rows // tm,),
        in_specs=[
            pl.BlockSpec((tm, D_MODEL), row),
            _const_spec((1, D_MODEL)),
            _const_spec(wd.shape),
            _const_spec((1, MLA_Q_RANK)),
            _const_spec((1, MLA_KV_RANK)),
            _const_spec(wq.shape),
            _const_spec(wkv.shape),
            pl.BlockSpec((tm, LANE), lambda i: (i % nt, 0)),
            pl.BlockSpec((tm, LANE), lambda i: (i % nt, 0)),
        ],
        out_specs=[
            pl.BlockSpec((tm, MLA_QN), row),
            pl.BlockSpec((tm, MLA_QR), row),
            pl.BlockSpec((tm, MLA_KV), row),
            pl.BlockSpec((tm, 2 * LANE), row),
        ],
        out_shape=[
            jax.ShapeDtypeStruct((rows, MLA_QN), BF16),
            jax.ShapeDtypeStruct((rows, MLA_QR), BF16),
            jax.ShapeDtypeStruct((rows, MLA_KV), BF16),
            jax.ShapeDtypeStruct((rows, 2 * LANE), BF16),
        ],
        compiler_params=_params(("parallel",)),
        name="mla_proj",
    )(h, ln, wd, q_norm, kv_norm, wq, wkv, cos4, sin4)


def _attn_kernel(qn_ref, qr_ref, kv_ref, kr_ref, o_ref, *, tq, nq):
    row = lax.broadcasted_iota(jnp.int32, (tq, tq), 0)
    col = lax.broadcasted_iota(jnp.int32, (tq, tq), 1)
    m = [[jnp.full((tq, 1), -1e30, F32) for _ in range(nq)] for _ in range(2)]
    l = [[jnp.zeros((tq, 1), F32) for _ in range(nq)] for _ in range(2)]
    acc = [[jnp.zeros((tq, MLA_V), F32) for _ in range(nq)] for _ in range(2)]

    for kb in range(nq):
        r0 = kb * tq
        for s in range(2):
            k = jnp.concatenate([kv_ref[r0:r0 + tq, 2 * s * LANE:(2 * s + 1) * LANE],
                                 kr_ref[r0:r0 + tq, s * LANE:(s + 1) * LANE]], axis=-1)
            v = kv_ref[r0:r0 + tq, (2 * s + 1) * LANE:(2 * s + 2) * LANE]
            q = jnp.concatenate([qn_ref[r0:, s * LANE:(s + 1) * LANE], qr_ref[r0:, :]], axis=-1)
            sc_all = lax.dot_general(q, k, (((1,), (1,)), ((), ())), preferred_element_type=F32)
            ps = []
            for qb in range(kb, nq):
                sc = sc_all[(qb - kb) * tq:(qb - kb + 1) * tq]
                if qb == kb:
                    sc = jnp.where(row >= col, sc, -1e30)
                m_new = jnp.maximum(m[s][qb], jnp.max(sc, axis=-1, keepdims=True))
                alpha = jnp.exp2(m[s][qb] - m_new)
                p = jnp.exp2(sc - m_new)
                l[s][qb] = alpha * l[s][qb] + jnp.sum(p, axis=-1, keepdims=True)
                acc[s][qb] = alpha * acc[s][qb]
                m[s][qb] = m_new
                ps.append(p.astype(BF16))
            pv = jnp.dot(jnp.concatenate(ps, axis=0), v, preferred_element_type=F32)
            for qb in range(kb, nq):
                acc[s][qb] = acc[s][qb] + pv[(qb - kb) * tq:(qb - kb + 1) * tq]
            o_ref[r0:r0 + tq, s * LANE:(s + 1) * LANE] = (acc[s][kb] / l[s][kb]).astype(BF16)


def _attention(qn, qr, kv, krp, nb, seq, tq):
    npair = MLA_HEADS // 2
    nq = seq // tq
    return pl.pallas_call(
        functools.partial(_attn_kernel, tq=tq, nq=nq),
        grid=(nb, npair),
        in_specs=[
            pl.BlockSpec((seq, 2 * LANE), lambda b, h: (b, h)),
            pl.BlockSpec((seq, LANE), lambda b, h: (b, h)),
            pl.BlockSpec((seq, 4 * LANE), lambda b, h: (b, h)),
            pl.BlockSpec((seq, 2 * LANE), lambda b, h: (b, 0)),
        ],
        out_specs=pl.BlockSpec((seq, 2 * LANE), lambda b, h: (b, h)),
        out_shape=jax.ShapeDtypeStruct((nb * seq, MLA_HEADS * MLA_V), BF16),
        compiler_params=_params(("parallel", "parallel")),
        name="mla_attention",
    )(qn, qr, kv, krp)


def _rope_tables(n, dim):
    inv = ROPE_BASE ** (-jnp.arange(0, dim, 2, dtype=F32) / dim)
    ang = jnp.arange(n, dtype=F32)[:, None] * inv[None, :]
    return jnp.cos(ang), jnp.sin(ang)


def _rot_cols(w):
    lead = w.shape[:-1]
    w = w.reshape(lead + (-1, 2, MLA_ROPE // 2))
    return jnp.stack([-w[..., 1, :], w[..., 0, :]], axis=-2).reshape(lead + (-1,))


def kernel(x, ln_mix_even, w_in_even, s5_lambda_re, s5_lambda_im, s5_log_dt, s5_b_re, s5_b_im, s5_c_re, s5_c_im, s5_d, s5_glu_w, s5_glu_b, ret_norm, w_out_even, ln_mix_odd, mla_w_down, mla_q_norm, mla_w_uq, mla_kv_norm, mla_w_ukv, w_out_odd, ln_ffn, ffn_w_up, ffn_conv_w, ffn_conv_b, ffn_w_down, ln_final):
    nb, seq, d = x.shape
    assert d == D_MODEL and nb % 8 == 0 and seq % RET_CHUNK == 0
    rows = nb * seq
    tm = min(seq, 1024)
    tmo = min(seq, 512)
    tf = 512
    row2 = lambda a: a.reshape(1, -1)

    x2d = x.reshape(rows, d)

    ret_cos, ret_sin = _rope_tables(seq, RET_HEAD_DIM)
    u, proj = _inproj(x2d, row2(ln_mix_even[0]), w_in_even[0].astype(BF16), ret_cos, ret_sin, seq, tm)

    ab_re, ab_im, bbr, bbi = _s5_discretise(s5_lambda_re[0], s5_lambda_im[0], s5_log_dt[0], s5_b_re[0], s5_b_im[0])
    a_re, a_im, bblk, cblk = _s5_pack(ab_re, ab_im, bbr, bbi, s5_c_re[0], s5_c_im[0])
    y_s5 = _s5(u.reshape(nb, seq, S5_WIDTH), bblk, cblk, a_re, a_im, row2(s5_d[0]),
               s5_glu_w[0].astype(BF16), row2(s5_glu_b[0]), tc=min(seq, 64))
    y_ret = _retention(proj, row2(ret_norm[0]), nb, seq)

    h = _outproj(x2d, [y_s5.reshape(rows, S5_WIDTH), y_ret], w_out_even[0].astype(BF16), tmo, "even_outproj")

    w_up_all = ffn_w_up.astype(BF16)
    w_down_all = ffn_w_down.astype(BF16)

    def ffn(h, layer, final_norm):
        return _ffn(h, row2(ln_ffn[layer]), w_up_all, ffn_conv_w[layer], row2(ffn_conv_b[layer]), w_down_all,
                    row2(ln_final), layer, seq, tm, tf, final_norm)

    h = ffn(h, 0, False)

    mla_cos, mla_sin = _rope_tables(seq, MLA_ROPE)
    cos4 = jnp.tile(mla_cos, (1, 4))
    sin4 = jnp.tile(mla_sin, (1, 4))
    wdn = mla_w_down[0]
    w_kr = wdn[:, MLA_Q_RANK + MLA_KV_RANK:]
    w_krp = _rot_cols(w_kr)
    wd = jnp.concatenate([wdn[:, :MLA_Q_RANK + MLA_KV_RANK], w_kr, w_kr, w_krp, w_krp], axis=1).astype(BF16)
    wq3 = mla_w_uq[0].reshape(MLA_Q_RANK, MLA_HEADS, MLA_NOPE + MLA_ROPE)
    wq_n = wq3[:, :, :MLA_NOPE].reshape(MLA_Q_RANK, MLA_QN)
    wq_r = wq3[:, :, MLA_NOPE:].reshape(MLA_Q_RANK, MLA_QR)
    wq = jnp.concatenate([wq_n, wq_r, _rot_cols(wq_r)], axis=1).astype(BF16)
    qn, qr, kv, krp = _mla_proj(h, row2(ln_mix_odd[0]), wd, row2(mla_q_norm[0]), row2(mla_kv_norm[0]),
                                wq, mla_w_ukv[0].astype(BF16), cos4, sin4, seq, tmo)
    o = _attention(qn, qr, kv, krp, nb, seq, tq=min(seq, 512))
    h = _outproj(h, [o], w_out_odd[0].astype(BF16), tmo, "odd_outproj")
    h = ffn(h, 1, True)
    return h.reshape(nb, seq, d)
```

```python
import functools

import numpy as np
import jax
import jax.numpy as jnp
from jax import lax
from jax.experimental import pallas as pl
from jax.experimental.pallas import tpu as pltpu

F32 = jnp.float32
BF16 = jnp.bfloat16

EPS = 1e-6
ROPE_BASE = 10000.0
D_MODEL = 2048
S5_WIDTH = 512
S5_GROUP = 16
S5_GROUPS = 32
S5_STATE = 64
S5_GB = 8
S5_NBLK = S5_GROUPS // S5_GB
S5_NSTATE = S5_GROUPS * S5_STATE
RET_HEADS = 6
RET_HEAD_DIM = 256
RET_WIDTH = RET_HEADS * RET_HEAD_DIM
RET_CHUNK = 256
RET_STEP_CHUNKS = 4
EVEN_IN = S5_WIDTH + 4 * RET_WIDTH
MLA_HEADS = 16
MLA_NOPE = 128
MLA_ROPE = 64
MLA_V = 128
MLA_Q_RANK = 512
MLA_KV_RANK = 512
D_FF = 5632
FFN_DOWN_SUBTILES = 4

LANE = 128
BF16_ROWS = 16
VMEM_LIMIT = 60 * 1024 * 1024
FFN_VMEM_LIMIT = 63 * 1024 * 1024


def _params(sem, vmem=VMEM_LIMIT, flags=None):
    return pltpu.CompilerParams(dimension_semantics=sem, vmem_limit_bytes=vmem, flags=flags)


def _rms(x, g):
    return x * lax.rsqrt(jnp.mean(x * x, axis=-1, keepdims=True) + EPS) * g


def _const_spec(shape):
    nd = len(shape)
    return pl.BlockSpec(shape, lambda *_: (0,) * nd, pipeline_mode=pl.Buffered(1))


IN_TN = RET_WIDTH
IN_NJ = 4
IN_CHUNK = 512
IN_SUBTILES = 2


def _inproj_kernel(h_ref, ln_ref, wu_ref, w0_ref, w1_ref, w2_ref, cos_ref, sin_ref, u_ref, p_ref, hn_ref):
    j = pl.program_id(1)
    ts = hn_ref.shape[0] // IN_SUBTILES
    w_ref = (w0_ref, w1_ref, w2_ref)

    def pieces(wrefs):
        for c, wref in enumerate(wrefs):
            for sub in range(IN_SUBTILES):
                rs = slice(sub * ts, (sub + 1) * ts)
                cs = slice(c * IN_CHUNK, (c + 1) * IN_CHUNK)
                yield rs, cs, jnp.dot(hn_ref[rs, :], wref[...], preferred_element_type=F32)

    @pl.when(j == 0)
    def _():
        hn_ref[...] = _rms(h_ref[...], ln_ref[...]).astype(BF16)
        for rs, cs, acc in pieces((wu_ref,)):
            u_ref[rs, cs] = acc

    @pl.when(j <= 1)
    def _():
        sc = jnp.where(j == 1, RET_HEAD_DIM ** -0.5, 1.0).astype(F32)
        for rs, cs, acc in pieces(w_ref):
            c = cos_ref[rs, :] * sc
            s = sin_ref[rs, :] * sc
            for hh in range(IN_CHUNK // RET_HEAD_DIM):
                o = hh * RET_HEAD_DIM
                x1 = acc[:, o:o + LANE]
                x2 = acc[:, o + LANE:o + 2 * LANE]
                p_ref[rs, cs.start + o:cs.start + o + LANE] = (x1 * c - x2 * s).astype(BF16)
                p_ref[rs, cs.start + o + LANE:cs.start + o + 2 * LANE] = (x2 * c + x1 * s).astype(BF16)

    @pl.when(j == 2)
    def _():
        for rs, cs, acc in pieces(w_ref):
            p_ref[rs, cs] = acc.astype(BF16)

    @pl.when(j == 3)
    def _():
        for rs, cs, acc in pieces(w_ref):
            p_ref[rs, cs] = (acc * jax.nn.sigmoid(acc)).astype(BF16)


def _inproj(x2d, ln, w, cos, sin, seq, tm):
    rows = x2d.shape[0]
    nt = seq // tm
    nc = IN_TN // IN_CHUNK
    wblk = lambda c: pl.BlockSpec((D_MODEL, IN_CHUNK), lambda i, j: (0, 1 + nc * j + c))
    return pl.pallas_call(
        _inproj_kernel,
        grid=(rows // tm, IN_NJ),
        in_specs=[
            pl.BlockSpec((tm, D_MODEL), lambda i, j: (i, 0)),
            pl.BlockSpec((1, D_MODEL), lambda i, j: (0, 0)),
            pl.BlockSpec((D_MODEL, S5_WIDTH), lambda i, j: (0, 0), pipeline_mode=pl.Buffered(1)),
            wblk(0), wblk(1), wblk(2),
            pl.BlockSpec((tm, LANE), lambda i, j: (i % nt, 0)),
            pl.BlockSpec((tm, LANE), lambda i, j: (i % nt, 0)),
        ],
        out_specs=[
            pl.BlockSpec((tm, S5_WIDTH), lambda i, j: (i, 0)),
            pl.BlockSpec((tm, IN_TN), lambda i, j: (i, j)),
        ],
        out_shape=[
            jax.ShapeDtypeStruct((rows, S5_WIDTH), F32),
            jax.ShapeDtypeStruct((rows, EVEN_IN - S5_WIDTH), BF16),
        ],
        scratch_shapes=[pltpu.VMEM((tm, D_MODEL), BF16)],
        compiler_params=_params(("parallel", "arbitrary")),
        name="even_inproj",
    )(x2d, ln, w, w, w, w, cos, sin)


def _s5_disc_kernel(lr_ref, li_ref, ldt_ref, br_ref, bi_ref, are_ref, aim_ref, bbr_ref, bbi_ref):
    lr = lr_ref[...]
    li = li_ref[...]
    dt = jnp.exp(ldt_ref[...])
    mag = jnp.exp(lr * dt)
    ab_re = mag * jnp.cos(li * dt)
    ab_im = mag * jnp.sin(li * dt)
    den = lr * lr + li * li
    nr = ab_re - 1.0
    ni = ab_im
    f_re = (nr * lr + ni * li) / den
    f_im = (ni * lr - nr * li) / den
    are_ref[...] = ab_re
    aim_ref[...] = ab_im
    br = br_ref[...]
    bi = bi_ref[...]
    bbr_ref[...] = f_re[:, None, :] * br - f_im[:, None, :] * bi
    bbi_ref[...] = f_re[:, None, :] * bi + f_im[:, None, :] * br


def _s5_discretise(lam_re, lam_im, log_dt, b_re, b_im):
    g, p, hh = S5_GROUPS, S5_STATE, S5_GROUP
    brt = jnp.swapaxes(b_re, 1, 2)
    bit = jnp.swapaxes(b_im, 1, 2)
    return pl.pallas_call(
        _s5_disc_kernel,
        out_shape=[
            jax.ShapeDtypeStruct((g, p), F32),
            jax.ShapeDtypeStruct((g, p), F32),
            jax.ShapeDtypeStruct((g, hh, p), F32),
            jax.ShapeDtypeStruct((g, hh, p), F32),
        ],
        name="s5_discretise",
    )(lam_re, lam_im, log_dt.reshape(g, 1), brt, bit)


def _s5_pack(ab_re, ab_im, bbr, bbi, c_re, c_im):
    eye = jnp.eye(S5_GB, dtype=F32)
    nb, gb, p, hh = S5_NBLK, S5_GB, S5_STATE, S5_GROUP

    def pack_b(bb):
        return jnp.einsum('gy,aghp->aghyp', eye, bb.reshape(nb, gb, hh, p)).reshape(nb, gb * hh, gb * p)

    def pack_c(cc):
        return jnp.einsum('gy,aghp->agpyh', eye, cc.reshape(nb, gb, hh, p)).reshape(nb, gb * p, gb * hh)

    bblk = jnp.concatenate([pack_b(bbr), pack_b(bbi)], axis=-1).astype(BF16)
    cblk = jnp.concatenate([pack_c(c_re), pack_c(-c_im)], axis=1).astype(BF16)
    return ab_re.reshape(1, S5_NSTATE), ab_im.reshape(1, S5_NSTATE), bblk, cblk


S5_SLAB = 256


def _s5_kernel(u_hbm, bblk_ref, cblk_ref, are_ref, aim_ref, d_ref, gw_ref, gb_ref, y_hbm,
               ubuf, ybuf, in_sem, out_sem, bur_ref, bui_ref, xr_ref, xi_ref, sr_ref, si_ref, *, nb, tc):
    t = pl.program_id(0)
    nt = pl.num_programs(0)
    slot = t % 2
    half = S5_GB * S5_STATE
    wu = S5_GB * S5_GROUP

    def in_copy(step, sl, b):
        return pltpu.make_async_copy(u_hbm.at[b, pl.ds(step * tc, tc), :], ubuf.at[sl, :, b, :], in_sem.at[sl, b])

    def out_copy(step, sl, b):
        return pltpu.make_async_copy(ybuf.at[sl, :, b, :], y_hbm.at[b, pl.ds(step * tc, tc), :], out_sem.at[sl, b])

    @pl.when(t == 0)
    def _():
        for b in range(nb):
            in_copy(0, 0, b).start()
        sr_ref[...] = jnp.zeros_like(sr_ref)
        si_ref[...] = jnp.zeros_like(si_ref)

    @pl.when(t + 1 < nt)
    def _():
        for b in range(nb):
            in_copy(t + 1, 1 - slot, b).start()

    for b in range(nb):
        in_copy(t, slot, b).wait()

    u = ubuf[slot].reshape(tc * nb, S5_WIDTH)
    ub = u.astype(BF16)
    for a in range(S5_NBLK):
        bu = jnp.dot(ub[:, a * wu:(a + 1) * wu], bblk_ref[a], preferred_element_type=F32)
        bur_ref[:, a * half:(a + 1) * half] = bu[:, :half]
        bui_ref[:, a * half:(a + 1) * half] = bu[:, half:]

    for sl in range(S5_NSTATE // S5_SLAB):
        cs = slice(sl * S5_SLAB, (sl + 1) * S5_SLAB)
        ar = jnp.broadcast_to(are_ref[:, cs], (nb, S5_SLAB))
        ai = jnp.broadcast_to(aim_ref[:, cs], (nb, S5_SLAB))

        def step(k, carry, cs=cs, ar=ar, ai=ai):
            xr, xi = carry
            r0 = pl.multiple_of(k * nb, nb)
            nr = ar * xr - ai * xi + bur_ref[pl.ds(r0, nb), cs]
            ni = ar * xi + ai * xr + bui_ref[pl.ds(r0, nb), cs]
            xr_ref[pl.ds(r0, nb), cs] = nr.astype(BF16)
            xi_ref[pl.ds(r0, nb), cs] = ni.astype(BF16)
            return nr, ni

        xr, xi = lax.fori_loop(0, tc, step, (sr_ref[:, cs], si_ref[:, cs]), unroll=8)
        sr_ref[:, cs] = xr
        si_ref[:, cs] = xi

    ys = []
    for a in range(S5_NBLK):
        ya = jnp.dot(xr_ref[:, a * half:(a + 1) * half], cblk_ref[a, :half, :], preferred_element_type=F32)
        ya = ya + jnp.dot(xi_ref[:, a * half:(a + 1) * half], cblk_ref[a, half:, :], preferred_element_type=F32)
        ys.append(ya)
    y = jnp.concatenate(ys, axis=-1) + d_ref[...] * u
    y = jax.nn.gelu(y)
    z = jnp.dot(y.astype(BF16), gw_ref[...], preferred_element_type=F32) + gb_ref[...]
    y = y * jax.nn.sigmoid(z)

    @pl.when(t >= 2)
    def _():
        for b in range(nb):
            out_copy(t - 2, slot, b).wait()

    ybuf[slot] = y.reshape(tc, nb, S5_WIDTH)
    for b in range(nb):
        out_copy(t, slot, b).start()

    @pl.when(t == nt - 1)
    def _():
        for b in range(nb):
            out_copy(t, slot, b).wait()

        @pl.when(nt >= 2)
        def _():
            for b in range(nb):
                out_copy(t - 1, 1 - slot, b).wait()


def _s5(u3, bblk, cblk, a_re, a_im, d_skip, glu_w, glu_b, tc):
    nb, seq, _ = u3.shape
    rows = nb * tc
    return pl.pallas_call(
        functools.partial(_s5_kernel, nb=nb, tc=tc),
        grid=(seq // tc,),
        in_specs=[
            pl.BlockSpec(memory_space=pl.ANY),
            _const_spec(bblk.shape),
            _const_spec(cblk.shape),
            _const_spec((1, S5_NSTATE)),
            _const_spec((1, S5_NSTATE)),
            _const_spec((1, S5_WIDTH)),
            _const_spec((S5_WIDTH, S5_WIDTH)),
            _const_spec((1, S5_WIDTH)),
        ],
        out_specs=pl.BlockSpec(memory_space=pl.ANY),
        out_shape=jax.ShapeDtypeStruct((nb, seq, S5_WIDTH), F32),
        scratch_shapes=[
            pltpu.VMEM((2, tc, nb, S5_WIDTH), F32),
            pltpu.VMEM((2, tc, nb, S5_WIDTH), F32),
            pltpu.SemaphoreType.DMA((2, nb)),
            pltpu.SemaphoreType.DMA((2, nb)),
            pltpu.VMEM((rows, S5_NSTATE), F32),
            pltpu.VMEM((rows, S5_NSTATE), F32),
            pltpu.VMEM((rows, S5_NSTATE), BF16),
            pltpu.VMEM((rows, S5_NSTATE), BF16),
            pltpu.VMEM((nb, S5_NSTATE), F32),
            pltpu.VMEM((nb, S5_NSTATE), F32),
        ],
        compiler_params=_params(("arbitrary",)),
        name="s5_scan",
    )(u3, bblk, cblk, a_re, a_im, d_skip, glu_w, glu_b)


def _retention_consts():
    hs = np.arange(RET_HEADS, dtype=np.float32)
    gamma_log = np.log((1.0 - np.exp2(-5.0 - hs)).astype(np.float32)).astype(np.float32)
    c = RET_CHUNK
    idx = np.arange(c, dtype=np.float32)
    diff = idx[:, None] - idx[None, :]
    dmask = np.where(diff[None] >= 0, np.exp(np.maximum(diff[None], 0.0) * gamma_log[:, None, None]), 0.0)
    xi = np.exp((idx[None, :] + 1.0) * gamma_log[:, None])
    zeta = np.exp((c - 1.0 - idx[None, :]) * gamma_log[:, None])
    g_chunk = np.exp(c * gamma_log)
    xi_b = np.broadcast_to(xi[:, :, None], (RET_HEADS, c, RET_HEAD_DIM))
    zeta_b = np.broadcast_to(zeta[:, :, None], (RET_HEADS, c, RET_HEAD_DIM))
    return (dmask.astype(np.float32), np.ascontiguousarray(xi_b, np.float32),
            np.ascontiguousarray(zeta_b, np.float32), [float(v) for v in g_chunk.astype(np.float32)])


def _retention_kernel(q_ref, k_ref, v_ref, sg_ref, dm_ref, xi_ref, zeta_ref, rn_ref, y_ref, r_ref, *, g_chunk, nsub):
    c = pl.program_id(1)

    @pl.when(c == 0)
    def _():
        r_ref[...] = jnp.zeros_like(r_ref)

    for h in range(RET_HEADS):
        hs = slice(h * RET_HEAD_DIM, (h + 1) * RET_HEAD_DIM)
        r = r_ref[h]
        for sc in range(nsub):
            rows = slice(sc * RET_CHUNK, (sc + 1) * RET_CHUNK)
            q = q_ref[rows, hs]
            k = k_ref[rows, hs]
            v = v_ref[rows, hs]
            s = lax.dot_general(q, k, (((1,), (1,)), ((), ())), preferred_element_type=F32)
            inner = (s * dm_ref[h]).astype(BF16)
            o = jnp.dot(inner, v, preferred_element_type=F32)
            o = o + xi_ref[h] * jnp.dot(q, r.astype(BF16), preferred_element_type=F32)
            kz = (k.astype(F32) * zeta_ref[h]).astype(BF16)
            r = r * g_chunk[h] + lax.dot_general(kz, v, (((0,), (0,)), ((), ())), preferred_element_type=F32)
            o = o * lax.rsqrt(jnp.mean(o * o, axis=-1, keepdims=True) + EPS)
            o = o * rn_ref[:, hs]
            y_ref[rows, hs] = (sg_ref[rows, hs].astype(F32) * o).astype(BF16)
        r_ref[h] = r


def _retention(proj, ret_norm, nb, seq):
    dmask, xi_b, zeta_b, g_chunk = _retention_consts()
    nsub = min(RET_STEP_CHUNKS, seq // RET_CHUNK)
    step_rows = RET_CHUNK * nsub
    nc = seq // step_rows
    blk = (step_rows, RET_WIDTH)
    return pl.pallas_call(
        functools.partial(_retention_kernel, g_chunk=g_chunk, nsub=nsub),
        grid=(nb, nc),
        in_specs=[
            pl.BlockSpec(blk, lambda b, c: (b * nc + c, 0)),
            pl.BlockSpec(blk, lambda b, c: (b * nc + c, 1)),
            pl.BlockSpec(blk, lambda b, c: (b * nc + c, 2)),
            pl.BlockSpec(blk, lambda b, c: (b * nc + c, 3)),
            _const_spec(dmask.shape),
            _const_spec(xi_b.shape),
            _const_spec(zeta_b.shape),
            _const_spec((1, RET_WIDTH)),
        ],
        out_specs=pl.BlockSpec(blk, lambda b, c: (b * nc + c, 0)),
        out_shape=jax.ShapeDtypeStruct((nb * seq, RET_WIDTH), BF16),
        scratch_shapes=[pltpu.VMEM((RET_HEADS, RET_HEAD_DIM, RET_HEAD_DIM), F32)],
        compiler_params=_params(("parallel", "arbitrary")),
        name="retention",
    )(proj, proj, proj, proj, jnp.asarray(dmask), jnp.asarray(xi_b), jnp.asarray(zeta_b), ret_norm)


def _outproj_kernel(*refs, n_in):
    res_ref = refs[0]
    ys = refs[1:1 + n_in]
    w_ref = refs[1 + n_in]
    o_ref = refs[2 + n_in]
    acc = res_ref[...]
    k0 = 0
    for y in ys:
        kk = y.shape[1]
        acc = acc + jnp.dot(y[...].astype(BF16), w_ref[k0:k0 + kk, :], preferred_element_type=F32)
        k0 += kk
    o_ref[...] = acc


def _outproj(res, ys, w, tm, name):
    rows, d = res.shape
    row = lambda i: (i, 0)
    return pl.pallas_call(
        functools.partial(_outproj_kernel, n_in=len(ys)),
        grid=(rows // tm,),
        in_specs=([pl.BlockSpec((tm, d), row)] + [pl.BlockSpec((tm, y.shape[1]), row) for y in ys]
                  + [_const_spec(w.shape)]),
        out_specs=pl.BlockSpec((tm, d), row),
        out_shape=jax.ShapeDtypeStruct((rows, d), F32),
        compiler_params=_params(("parallel",)),
        name=name,
    )(res, *ys, w)


def _ffn_kernel(h_ref, halo_ref, ln_ref, wg_ref, wv_ref, cwg_ref, cwv_ref, cbg_ref, cbv_ref, wd_ref, lnf_ref,
                o_ref, hn_ref, ag_ref, av_ref, *, nt, tm, nj, final_norm):
    i = pl.program_id(0)
    s = pl.program_id(1)
    hl = BF16_ROWS
    slot = s % 2

    def up(sl):
        hn = hn_ref[...]
        ag_ref[sl] = jnp.dot(hn, wg_ref[...], preferred_element_type=F32)
        av_ref[sl] = jnp.dot(hn, wv_ref[...], preferred_element_type=F32)

    def down(sl):
        ts = tm // FFN_DOWN_SUBTILES
        for sub in range(FFN_DOWN_SUBTILES):
            r0 = sub * ts

            def conv(a_ref, cw_ref, cb_ref, r0=r0):
                cw = cw_ref[...]
                return (a_ref[sl, pl.ds(r0 + hl - 2, ts), :] * cw[0:1] + a_ref[sl, pl.ds(r0 + hl - 1, ts), :] * cw[1:2]
                        + a_ref[sl, pl.ds(r0 + hl, ts), :] * cw[2:3] + cb_ref[...])

            gate = conv(ag_ref, cwg_ref, cbg_ref)
            val = conv(av_ref, cwv_ref, cbv_ref)
            g = (gate * jax.nn.sigmoid(gate) * val).astype(BF16)
            o_ref[r0:r0 + ts, :] += jnp.dot(g, wd_ref[...], preferred_element_type=F32)

    @pl.when(s == 0)
    def _():
        hn_ref[hl:, :] = _rms(h_ref[...], ln_ref[...]).astype(BF16)
        halo = _rms(halo_ref[...], ln_ref[...])
        hn_ref[:hl, :] = jnp.where(i % nt > 0, halo, 0.0).astype(BF16)
        o_ref[...] = h_ref[...]
        up(0)

    @pl.when(jnp.logical_and(s > 0, s < nj))
    def _():
        up(slot)
        down(1 - slot)

    @pl.when(s == nj)
    def _():
        down(1 - slot)
        if final_norm:
            o_ref[...] = _rms(o_ref[...], lnf_ref[...])


def _ffn(h, ln, w_up, conv_w, conv_b, w_down, ln_final, layer, seq, tm, tf, final_norm):
    rows = h.shape[0]
    hl = BF16_ROWS
    nj = D_FF // tf
    nt = seq // tm
    up_j = lambda s: jnp.minimum(s, nj - 1)
    dn_j = lambda s: jnp.maximum(s - 1, 0)
    return pl.pallas_call(
        functools.partial(_ffn_kernel, nt=nt, tm=tm, nj=nj, final_norm=final_norm),
        grid=(rows // tm, nj + 1),
        in_specs=[
            pl.BlockSpec((tm, D_MODEL), lambda i, s: (i, 0)),
            pl.BlockSpec((hl, D_MODEL), lambda i, s: (jnp.maximum(i * (tm // hl) - 1, 0), 0)),
            pl.BlockSpec((1, D_MODEL), lambda i, s: (0, 0)),
            pl.BlockSpec((None, D_MODEL, tf), lambda i, s: (layer, 0, up_j(s))),
            pl.BlockSpec((None, D_MODEL, tf), lambda i, s: (layer, 0, nj + up_j(s))),
            pl.BlockSpec((3, tf), lambda i, s: (0, dn_j(s))),
            pl.BlockSpec((3, tf), lambda i, s: (0, nj + dn_j(s))),
            pl.BlockSpec((1, tf), lambda i, s: (0, dn_j(s))),
            pl.BlockSpec((1, tf), lambda i, s: (0, nj + dn_j(s))),
            pl.BlockSpec((None, tf, D_MODEL), lambda i, s: (layer, dn_j(s), 0)),
            pl.BlockSpec((1, D_MODEL), lambda i, s: (0, 0)),
        ],
        out_specs=pl.BlockSpec((tm, D_MODEL), lambda i, s: (i, 0)),
        out_shape=jax.ShapeDtypeStruct((rows, D_MODEL), F32),
        scratch_shapes=[
            pltpu.VMEM((hl + tm, D_MODEL), BF16),
            pltpu.VMEM((2, hl + tm, tf), F32),
            pltpu.VMEM((2, hl + tm, tf), F32),
        ],
        compiler_params=_params(("parallel", "arbitrary"), vmem=FFN_VMEM_LIMIT),
        name="conv_ffn",
    )(h, h, ln, w_up, w_up, conv_w, conv_w, conv_b, conv_b, w_down, ln_final)


MLA_DOWN_PAD = MLA_Q_RANK + MLA_KV_RANK + 2 * LANE
MLA_QN = MLA_HEADS * MLA_NOPE
MLA_QR = MLA_HEADS * MLA_ROPE
MLA_KV = MLA_HEADS * (MLA_NOPE + MLA_V)
MLA_CHUNK = 1024
LOG2E = 1.4426950408889634


def _mla_proj_kernel(h_ref, ln_ref, wd_ref, qn_ref, kvn_ref, wq_ref, wkv_ref, cos_ref, sin_ref,
                     oqn_ref, oqr_ref, okv_ref, okr_ref):
    hn = _rms(h_ref[...], ln_ref[...]).astype(BF16)
    c = jnp.dot(hn, wd_ref[...], preferred_element_type=F32)
    cos = cos_ref[...]
    sin = sin_ref[...]
    scale = (MLA_NOPE + MLA_ROPE) ** -0.5 * LOG2E

    cq = _rms(c[:, :MLA_Q_RANK], qn_ref[...]).astype(BF16)
    for n0 in range(0, MLA_QN, MLA_CHUNK):
        qn = jnp.dot(cq, wq_ref[:, n0:n0 + MLA_CHUNK], preferred_element_type=F32)
        oqn_ref[:, n0:n0 + MLA_CHUNK] = (qn * scale).astype(BF16)
    cs = cos * scale
    ss = sin * scale
    for n0 in range(0, MLA_QR, MLA_CHUNK):
        qr = jnp.dot(cq, wq_ref[:, MLA_QN + n0:MLA_QN + n0 + MLA_CHUNK], preferred_element_type=F32)
        qp = jnp.dot(cq, wq_ref[:, MLA_QN + MLA_QR + n0:MLA_QN + MLA_QR + n0 + MLA_CHUNK],
                     preferred_element_type=F32)
        for m in range(MLA_CHUNK // LANE):
            ls = slice(m * LANE, (m + 1) * LANE)
            oqr_ref[:, n0 + m * LANE:n0 + (m + 1) * LANE] = (qr[:, ls] * cs + qp[:, ls] * ss).astype(BF16)

    ckv = _rms(c[:, MLA_Q_RANK:MLA_Q_RANK + MLA_KV_RANK], kvn_ref[...]).astype(BF16)
    for n0 in range(0, MLA_KV, MLA_CHUNK):
        okv_ref[:, n0:n0 + MLA_CHUNK] = jnp.dot(
            ckv, wkv_ref[:, n0:n0 + MLA_CHUNK], preferred_element_type=F32).astype(BF16)

    o = MLA_Q_RANK + MLA_KV_RANK
    kr2 = c[:, o:o + LANE] * cos + c[:, o + LANE:o + 2 * LANE] * sin
    left = lax.broadcasted_iota(jnp.int32, kr2.shape, 1) < MLA_ROPE
    okr_ref[:, :LANE] = jnp.where(left, kr2, 0.0).astype(BF16)
    okr_ref[:, LANE:] = jnp.where(left, 0.0, kr2).astype(BF16)


def _mla_proj(h, ln, wd, q_norm, kv_norm, wq, wkv, cos4, sin4, seq, tm):
    rows = h.shape[0]
    nt = seq // tm
    row = lambda i: (i, 0)
    return pl.pallas_call(
        _mla_proj_kernel,
        grid=(rows // tm,),
        in_specs=[
            pl.BlockSpec((tm, D_MODEL), row),
            _const_spec((1, D_MODEL)),
            _const_spec(wd.shape),
            _const_spec((1, MLA_Q_RANK)),
            _const_spec((1, MLA_KV_RANK)),
            _const_spec(wq.shape),
            _const_spec(wkv.shape),
            pl.BlockSpec((tm, LANE), lambda i: (i % nt, 0)),
            pl.BlockSpec((tm, LANE), lambda i: (i % nt, 0)),
        ],
        out_specs=[
            pl.BlockSpec((tm, MLA_QN), row),
            pl.BlockSpec((tm, MLA_QR), row),
            pl.BlockSpec((tm, MLA_KV), row),
            pl.BlockSpec((tm, 2 * LANE), row),
        ],
        out_shape=[
            jax.ShapeDtypeStruct((rows, MLA_QN), BF16),
            jax.ShapeDtypeStruct((rows, MLA_QR), BF16),
            jax.ShapeDtypeStruct((rows, MLA_KV), BF16),
            jax.ShapeDtypeStruct((rows, 2 * LANE), BF16),
        ],
        compiler_params=_params(("parallel",)),
        name="mla_proj",
    )(h, ln, wd, q_norm, kv_norm, wq, wkv, cos4, sin4)


def _attn_kernel(qn_ref, qr_ref, kv_ref, kr_ref, o_ref, *, tq, nq):
    row = lax.broadcasted_iota(jnp.int32, (tq, tq), 0)
    col = lax.broadcasted_iota(jnp.int32, (tq, tq), 1)
    ones = jnp.ones((tq, LANE), BF16)
    m = [[jnp.full((tq, 1), -1e30, F32) for _ in range(nq)] for _ in range(2)]
    acc = [[jnp.zeros((tq, 2 * MLA_V), F32) for _ in range(nq)] for _ in range(2)]

    for kb in range(nq):
        r0 = kb * tq
        for s in range(2):
            k = jnp.concatenate([kv_ref[r0:r0 + tq, 2 * s * LANE:(2 * s + 1) * LANE],
                                 kr_ref[r0:r0 + tq, s * LANE:(s + 1) * LANE]], axis=-1)
            v1 = jnp.concatenate([kv_ref[r0:r0 + tq, (2 * s + 1) * LANE:(2 * s + 2) * LANE], ones], axis=-1)
            q = jnp.concatenate([qn_ref[r0:, s * LANE:(s + 1) * LANE], qr_ref[r0:, :]], axis=-1)
            sc_all = lax.dot_general(q, k, (((1,), (1,)), ((), ())), preferred_element_type=F32)
            ps = []
            for qb in range(kb, nq):
                sc = sc_all[(qb - kb) * tq:(qb - kb + 1) * tq]
                if qb == kb:
                    sc = jnp.where(row >= col, sc, -1e30)
                m_new = jnp.maximum(m[s][qb], jnp.max(sc, axis=-1, keepdims=True))
                acc[s][qb] = jnp.exp2(m[s][qb] - m_new) * acc[s][qb]
                m[s][qb] = m_new
                ps.append(jnp.exp2(sc - m_new).astype(BF16))
            pv = jnp.dot(jnp.concatenate(ps, axis=0), v1, preferred_element_type=F32)
            for qb in range(kb, nq):
                acc[s][qb] = acc[s][qb] + pv[(qb - kb) * tq:(qb - kb + 1) * tq]
            a = acc[s][kb]
            o_ref[r0:r0 + tq, s * LANE:(s + 1) * LANE] = (a[:, :MLA_V] / a[:, MLA_V:]).astype(BF16)


def _attention(qn, qr, kv, krp, nb, seq, tq):
    npair = MLA_HEADS // 2
    nq = seq // tq
    return pl.pallas_call(
        functools.partial(_attn_kernel, tq=tq, nq=nq),
        grid=(nb, npair),
        in_specs=[
            pl.BlockSpec((seq, 2 * LANE), lambda b, h: (b, h)),
            pl.BlockSpec((seq, LANE), lambda b, h: (b, h)),
            pl.BlockSpec((seq, 4 * LANE), lambda b, h: (b, h)),
            pl.BlockSpec((seq, 2 * LANE), lambda b, h: (b, 0)),
        ],
        out_specs=pl.BlockSpec((seq, 2 * LANE), lambda b, h: (b, h)),
        out_shape=jax.ShapeDtypeStruct((nb * seq, MLA_HEADS * MLA_V), BF16),
        compiler_params=_params(("parallel", "parallel")),
        name="mla_attention",
    )(qn, qr, kv, krp)


def _rope_tables(n, dim):
    inv = ROPE_BASE ** (-jnp.arange(0, dim, 2, dtype=F32) / dim)
    ang = jnp.arange(n, dtype=F32)[:, None] * inv[None, :]
    return jnp.cos(ang), jnp.sin(ang)


def _rot_cols(w):
    lead = w.shape[:-1]
    w = w.reshape(lead + (-1, 2, MLA_ROPE // 2))
    return jnp.stack([-w[..., 1, :], w[..., 0, :]], axis=-2).reshape(lead + (-1,))


def kernel(x, ln_mix_even, w_in_even, s5_lambda_re, s5_lambda_im, s5_log_dt, s5_b_re, s5_b_im, s5_c_re, s5_c_im, s5_d, s5_glu_w, s5_glu_b, ret_norm, w_out_even, ln_mix_odd, mla_w_down, mla_q_norm, mla_w_uq, mla_kv_norm, mla_w_ukv, w_out_odd, ln_ffn, ffn_w_up, ffn_conv_w, ffn_conv_b, ffn_w_down, ln_final):
    nb, seq, d = x.shape
    assert d == D_MODEL and nb % 8 == 0 and seq % RET_CHUNK == 0
    rows = nb * seq
    tm = min(seq, 1024)
    tmo = min(seq, 512)
    tf = 512
    row2 = lambda a: a.reshape(1, -1)

    x2d = x.reshape(rows, d)

    ret_cos, ret_sin = _rope_tables(seq, RET_HEAD_DIM)
    u, proj = _inproj(x2d, row2(ln_mix_even[0]), w_in_even[0].astype(BF16), ret_cos, ret_sin, seq, tm)

    ab_re, ab_im, bbr, bbi = _s5_discretise(s5_lambda_re[0], s5_lambda_im[0], s5_log_dt[0], s5_b_re[0], s5_b_im[0])
    a_re, a_im, bblk, cblk = _s5_pack(ab_re, ab_im, bbr, bbi, s5_c_re[0], s5_c_im[0])
    y_s5 = _s5(u.reshape(nb, seq, S5_WIDTH), bblk, cblk, a_re, a_im, row2(s5_d[0]),
               s5_glu_w[0].astype(BF16), row2(s5_glu_b[0]), tc=min(seq, 64))
    y_ret = _retention(proj, row2(ret_norm[0]), nb, seq)

    h = _outproj(x2d, [y_s5.reshape(rows, S5_WIDTH), y_ret], w_out_even[0].astype(BF16), tmo, "even_outproj")

    w_up_all = ffn_w_up.astype(BF16)
    w_down_all = ffn_w_down.astype(BF16)

    def ffn(h, layer, final_norm):
        return _ffn(h, row2(ln_ffn[layer]), w_up_all, ffn_conv_w[layer], row2(ffn_conv_b[layer]), w_down_all,
                    row2(ln_final), layer, seq, tm, tf, final_norm)

    h = ffn(h, 0, False)

    mla_cos, mla_sin = _rope_tables(seq, MLA_ROPE)
    cos4 = jnp.tile(mla_cos, (1, 4))
    sin4 = jnp.tile(mla_sin, (1, 4))
    wdn = mla_w_down[0]
    w_kr = wdn[:, MLA_Q_RANK + MLA_KV_RANK:]
    w_krp = _rot_cols(w_kr)
    wd = jnp.concatenate([wdn[:, :MLA_Q_RANK + MLA_KV_RANK], w_kr, w_kr, w_krp, w_krp], axis=1).astype(BF16)
    wq3 = mla_w_uq[0].reshape(MLA_Q_RANK, MLA_HEADS, MLA_NOPE + MLA_ROPE)
    wq_n = wq3[:, :, :MLA_NOPE].reshape(MLA_Q_RANK, MLA_QN)
    wq_r = wq3[:, :, MLA_NOPE:].reshape(MLA_Q_RANK, MLA_QR)
    wq = jnp.concatenate([wq_n, wq_r, _rot_cols(wq_r)], axis=1).astype(BF16)
    qn, qr, kv, krp = _mla_proj(h, row2(ln_mix_odd[0]), wd, row2(mla_q_norm[0]), row2(mla_kv_norm[0]),
                                wq, mla_w_ukv[0].astype(BF16), cos4, sin4, seq, tmo)
    o = _attention(qn, qr, kv, krp, nb, seq, tq=min(seq, 512))
    h = _outproj(h, [o], w_out_odd[0].astype(BF16), tmo, "odd_outproj")
    h = ffn(h, 1, True)
    return h.reshape(nb, seq, d)
```
